```python
import math
import jax
import jax.numpy as jnp
from jax import lax
import numpy as np


D_MODEL = 1024
BATCH = 4
SEQ = 4096
DEPTH = 4

N_BRANCH = 3
CHUNK = 64
EPS = 1e-6
GLA_HEADS = 4
GLA_DK = 64
GLA_DV = 128
GLA_WIDTH = GLA_HEADS * GLA_DV
GLA_LOWRANK = 16
GLA_GATE_NORM = 16.0
S5_GROUP_CH = 16
S5_GROUPS = 32
S5_WIDTH = S5_GROUPS * S5_GROUP_CH
S5_STATE = 64
S5_DT_MIN = 0.001
S5_DT_MAX = 0.1
GDN_HEADS = 4
GDN_DK = 128
GDN_DV = 128
GDN_WIDTH = GDN_HEADS * GDN_DV
GDN_QKV = 2 * GDN_HEADS * GDN_DK + GDN_WIDTH
GDN_CONV = 4
D_FF = ((8 * D_MODEL + 3 * 256 - 1) // (3 * 256)) * 256
IN_SIZES = (GLA_HEADS * GLA_DK, GLA_HEADS * GLA_DK, GLA_WIDTH, GLA_LOWRANK, GLA_WIDTH,
            S5_WIDTH,
            GDN_QKV, GDN_HEADS, GDN_HEADS, GDN_WIDTH,
            N_BRANCH * D_MODEL)
D_IN = sum(IN_SIZES)

kernel_name = 'hybrid_gla_s5_gdn_adaln_block'


def rmsnorm(x, g):
    xf = x.astype(jnp.float32)
    y = xf * lax.rsqrt(jnp.mean(xf * xf, axis=-1, keepdims=True) + EPS)
    return (y * g.astype(jnp.float32)).astype(x.dtype)


def l2norm(x):
    return x * lax.rsqrt(jnp.sum(x * x, axis=-1, keepdims=True) + EPS)


def to_chunks(x):
    b, l, h, d = x.shape
    return x.reshape(b, l // CHUNK, CHUNK, h, d).transpose(0, 3, 1, 2, 4)


def to_chunks_scalar(x):
    b, l, h = x.shape
    return x.reshape(b, l // CHUNK, CHUNK, h).transpose(0, 3, 1, 2)


def from_chunks(x):
    b, h, n, c, d = x.shape
    return x.transpose(0, 2, 3, 1, 4).reshape(b, n * c, h, d)


def causal_dwconv(x, w):
    k = w.shape[0]
    return lax.conv_general_dilated(x, w[:, None, :], window_strides=(1,), padding=[(k - 1, 0)],
                                    dimension_numbers=('NWC', 'WIO', 'NWC'),
                                    feature_group_count=x.shape[-1])


def gla_mixer(q, k, v, lr, og, w_lr, b_lr, norm_g):
    f32 = jnp.float32
    bsz, l, _ = q.shape
    gk = jax.nn.log_sigmoid(lr.astype(f32) @ w_lr.astype(f32) + b_lr.astype(f32)) / GLA_GATE_NORM
    qc = to_chunks(q.astype(f32).reshape(bsz, l, GLA_HEADS, GLA_DK)) * GLA_DK ** -0.5
    kc = to_chunks(k.astype(f32).reshape(bsz, l, GLA_HEADS, GLA_DK))
    vc = to_chunks(v.astype(f32).reshape(bsz, l, GLA_HEADS, GLA_DV))
    gc = to_chunks(gk.reshape(bsz, l, GLA_HEADS, GLA_DK))
    bcum = jnp.cumsum(gc, axis=3)
    b_last = bcum[:, :, :, -1:, :]
    q_e = qc * jnp.exp(bcum)
    k_e = kc * jnp.exp(-bcum)
    idx = jnp.arange(CHUNK)
    incl = idx[:, None] >= idx[None, :]
    scores = jnp.where(incl, jnp.einsum('bhnik,bhnjk->bhnij', q_e, k_e), 0.0)
    o_intra = jnp.einsum('bhnij,bhnjv->bhniv', scores, vc)
    kv = jnp.einsum('bhnck,bhncv->bhnkv', kc * jnp.exp(b_last - bcum), vc)
    decay = jnp.exp(b_last[:, :, :, 0, :])

    def step(s, inp):
        d_n, kv_n = inp
        return d_n[..., None] * s + kv_n, s

    s0 = jnp.zeros((bsz, GLA_HEADS, GLA_DK, GLA_DV), f32)
    _, s_prev = lax.scan(step, s0, (jnp.moveaxis(decay, 2, 0), jnp.moveaxis(kv, 2, 0)))
    o_inter = jnp.einsum('bhnck,nbhkv->bhncv', q_e, s_prev)
    o = from_chunks(o_intra + o_inter)
    o = rmsnorm(o, norm_g) * jax.nn.silu(og.astype(f32).reshape(bsz, l, GLA_HEADS, GLA_DV))
    return o.reshape(bsz, l, GLA_WIDTH)


def s5_mixer(u, lam_re, lam_im, log_dt, b_re, b_im, c_re, c_im, d, w_glu):
    f32 = jnp.float32
    bsz, l, _ = u.shape
    dt = jnp.exp(log_dt.astype(f32))[:, None]
    lr, li = lam_re.astype(f32), lam_im.astype(f32)
    mag = jnp.exp(lr * dt)
    ab_re = mag * jnp.cos(li * dt)
    ab_im = mag * jnp.sin(li * dt)
    den = lr * lr + li * li
    nr, ni = ab_re - 1.0, ab_im
    w_re = ((nr * lr + ni * li) / den)[..., None]
    w_im = ((ni * lr - nr * li) / den)[..., None]
    br, bi = b_re.astype(f32), b_im.astype(f32)
    bb_re = w_re * br - w_im * bi
    bb_im = w_re * bi + w_im * br
    ug = u.astype(f32).reshape(bsz, l, S5_GROUPS, S5_GROUP_CH)
    bu_re = jnp.einsum('blgh,gph->blgp', ug, bb_re)
    bu_im = jnp.einsum('blgh,gph->blgp', ug, bb_im)
    a_re = jnp.broadcast_to(ab_re, (1, l, S5_GROUPS, S5_STATE))
    a_im = jnp.broadcast_to(ab_im, (1, l, S5_GROUPS, S5_STATE))

    def combine(e1, e2):
        ar1, ai1, br1, bi1 = e1
        ar2, ai2, br2, bi2 = e2
        return (ar2 * ar1 - ai2 * ai1, ar2 * ai1 + ai2 * ar1,
                ar2 * br1 - ai2 * bi1 + br2, ar2 * bi1 + ai2 * br1 + bi2)

    _, _, s_re, s_im = lax.associative_scan(combine, (a_re, a_im, bu_re, bu_im), axis=1)
    y = (jnp.einsum('blgp,ghp->blgh', s_re, c_re.astype(f32))
         - jnp.einsum('blgp,ghp->blgh', s_im, c_im.astype(f32))
         + d.astype(f32) * ug)
    y = jax.nn.gelu(y).reshape(bsz, l, S5_WIDTH)
    ga, gb = jnp.split(y @ w_glu.astype(f32), 2, axis=-1)
    return ga * jax.nn.sigmoid(gb)


def gdn_mixer(qkv, beta_pre, a_pre, og, conv_w, a_log, dt_bias, norm_g):
    f32 = jnp.float32
    bsz, l, _ = qkv.shape
    qkv = jax.nn.silu(causal_dwconv(qkv.astype(f32), conv_w.astype(f32)))
    q, k, v = jnp.split(qkv, [GDN_HEADS * GDN_DK, 2 * GDN_HEADS * GDN_DK], axis=-1)
    q = l2norm(q.reshape(bsz, l, GDN_HEADS, GDN_DK)) * GDN_DK ** -0.5
    k = l2norm(k.reshape(bsz, l, GDN_HEADS, GDN_DK))
    v = v.reshape(bsz, l, GDN_HEADS, GDN_DV)
    beta = jax.nn.sigmoid(beta_pre.astype(f32))
    g = -jnp.exp(a_log.astype(f32)) * jax.nn.softplus(a_pre.astype(f32) + dt_bias.astype(f32))
    qc, kc, vc = to_chunks(q), to_chunks(k), to_chunks(v)
    bc = to_chunks_scalar(beta)
    gam = jnp.cumsum(to_chunks_scalar(g), axis=-1)
    idx = jnp.arange(CHUNK)
    incl = idx[:, None] >= idx[None, :]
    strict = idx[:, None] > idx[None, :]
    decay_mask = jnp.exp(jnp.where(incl, gam[..., :, None] - gam[..., None, :], -jnp.inf))
    k_beta = kc * bc[..., None]
    v_beta = vc * bc[..., None]
    lower = jnp.where(strict, jnp.einsum('bhnik,bhnjk->bhnij', k_beta, kc) * decay_mask, 0.0)
    eye = jnp.eye(CHUNK, dtype=f32)
    rhs = jnp.concatenate([v_beta, k_beta * jnp.exp(gam)[..., None]], axis=-1)
    sol = lax.linalg.triangular_solve(eye + lower, rhs, left_side=True, lower=True,
                                      unit_diagonal=True)
    u_c, w_c = jnp.split(sol, [GDN_DV], axis=-1)
    attn = jnp.einsum('bhnik,bhnjk->bhnij', qc, kc) * decay_mask
    q_dec = qc * jnp.exp(gam)[..., None]
    gam_last = gam[..., -1:]
    k_dec = kc * jnp.exp(gam_last - gam)[..., None]
    chunk_decay = jnp.exp(gam_last[..., 0])

    def step(s, inp):
        q_n, k_n, u_n, w_n, a_n, d_n = inp
        v_new = u_n - jnp.einsum('bhck,bhkv->bhcv', w_n, s)
        o_n = jnp.einsum('bhck,bhkv->bhcv', q_n, s) + jnp.einsum('bhij,bhjv->bhiv', a_n, v_new)
        s = d_n[..., None, None] * s + jnp.einsum('bhck,bhcv->bhkv', k_n, v_new)
        return s, o_n

    xs = (jnp.moveaxis(q_dec, 2, 0), jnp.moveaxis(k_dec, 2, 0), jnp.moveaxis(u_c, 2, 0),
          jnp.moveaxis(w_c, 2, 0), jnp.moveaxis(attn, 2, 0), jnp.moveaxis(chunk_decay, 2, 0))
    s0 = jnp.zeros((bsz, GDN_HEADS, GDN_DK, GDN_DV), f32)
    _, o = lax.scan(step, s0, xs)
    o = from_chunks(jnp.moveaxis(o, 0, 2))
    o = rmsnorm(o, norm_g) * jax.nn.silu(og.astype(f32).reshape(bsz, l, GDN_HEADS, GDN_DV))
    return o.reshape(bsz, l, GDN_WIDTH)


def setup_inputs(seed: int = 0) -> dict:
    key = jax.random.key(seed)
    ks = jax.random.split(key, 32)
    f32 = jnp.float32

    def nrm(k, shape, scale):
        return jax.random.normal(k, shape, f32) * scale

    nl, d = DEPTH, D_MODEL
    x = nrm(ks[0], (BATCH, SEQ, d), 1.0)
    c = nrm(ks[1], (BATCH, d), 1.0)
    w_ada = nrm(ks[2], (nl, d, 6 * d), 0.1 * d ** -0.5)
    b_ada = nrm(ks[3], (nl, 6 * d), 0.02)
    norm1_g = 1.0 + nrm(ks[4], (nl, d), 0.02)
    w_in = nrm(ks[5], (nl, d, D_IN), d ** -0.5)
    gla_w_lr = nrm(ks[6], (nl, GLA_LOWRANK, GLA_HEADS * GLA_DK), GLA_LOWRANK ** -0.5)
    gla_b_lr = nrm(ks[7], (nl, GLA_HEADS * GLA_DK), 0.1)
    gla_norm_g = 1.0 + nrm(ks[8], (nl, GLA_DV), 0.02)
    s5_lambda_re = -0.5 + nrm(ks[9], (nl, S5_GROUPS, S5_STATE), 0.01)
    s5_lambda_im = (math.pi * jnp.arange(S5_STATE, dtype=f32)
                    + nrm(ks[10], (nl, S5_GROUPS, S5_STATE), 0.01))
    s5_log_dt = jax.random.uniform(ks[11], (nl, S5_GROUPS), f32,
                                   minval=math.log(S5_DT_MIN), maxval=math.log(S5_DT_MAX))
    s5_b_re = nrm(ks[12], (nl, S5_GROUPS, S5_STATE, S5_GROUP_CH), (2 * S5_GROUP_CH) ** -0.5)
    s5_b_im = nrm(ks[13], (nl, S5_GROUPS, S5_STATE, S5_GROUP_CH), (2 * S5_GROUP_CH) ** -0.5)
    s5_c_re = nrm(ks[14], (nl, S5_GROUPS, S5_GROUP_CH, S5_STATE), (2 * S5_STATE) ** -0.5)
    s5_c_im = nrm(ks[15], (nl, S5_GROUPS, S5_GROUP_CH, S5_STATE), (2 * S5_STATE) ** -0.5)
    s5_d = nrm(ks[16], (nl, S5_GROUPS, S5_GROUP_CH), 0.5)
    s5_w_glu = nrm(ks[17], (nl, S5_WIDTH, 2 * S5_WIDTH), S5_WIDTH ** -0.5)
    gdn_conv_w = nrm(ks[18], (nl, GDN_CONV, GDN_QKV), GDN_CONV ** -0.5)
    gdn_a_log = jnp.log(jax.random.uniform(ks[19], (nl, GDN_HEADS), f32, minval=1.0, maxval=16.0))
    gdn_dt = jnp.exp(jax.random.uniform(ks[20], (nl, GDN_HEADS), f32,
                                        minval=math.log(0.001), maxval=math.log(0.1)))
    gdn_dt_bias = gdn_dt + jnp.log(-jnp.expm1(-gdn_dt))
    gdn_norm_g = 1.0 + nrm(ks[21], (nl, GDN_DV), 0.02)
    w_branch_gla = nrm(ks[22], (nl, GLA_WIDTH, d), GLA_WIDTH ** -0.5)
    w_branch_s5 = nrm(ks[23], (nl, S5_WIDTH, d), S5_WIDTH ** -0.5)
    w_branch_gdn = nrm(ks[24], (nl, GDN_WIDTH, d), GDN_WIDTH ** -0.5)
    w_out = nrm(ks[25], (nl, d, d), d ** -0.5)
    norm2_g = 1.0 + nrm(ks[26], (nl, d), 0.02)
    w_ffn_in = nrm(ks[27], (nl, d, 2 * D_FF), d ** -0.5)
    w_ffn_out = nrm(ks[28], (nl, D_FF, d), D_FF ** -0.5)
    final_g = 1.0 + nrm(ks[29], (d,), 0.02)
    return {'x': x, 'c': c, 'w_ada': w_ada, 'b_ada': b_ada, 'norm1_g': norm1_g, 'w_in': w_in,
            'gla_w_lr': gla_w_lr, 'gla_b_lr': gla_b_lr, 'gla_norm_g': gla_norm_g,
            's5_lambda_re': s5_lambda_re, 's5_lambda_im': s5_lambda_im, 's5_log_dt': s5_log_dt,
            's5_b_re': s5_b_re, 's5_b_im': s5_b_im, 's5_c_re': s5_c_re, 's5_c_im': s5_c_im,
            's5_d': s5_d, 's5_w_glu': s5_w_glu,
            'gdn_conv_w': gdn_conv_w, 'gdn_a_log': gdn_a_log, 'gdn_dt_bias': gdn_dt_bias,
            'gdn_norm_g': gdn_norm_g,
            'w_branch_gla': w_branch_gla, 'w_branch_s5': w_branch_s5, 'w_branch_gdn': w_branch_gdn,
            'w_out': w_out, 'norm2_g': norm2_g, 'w_ffn_in': w_ffn_in, 'w_ffn_out': w_ffn_out,
            'final_g': final_g}


def reference(x, c, w_ada, b_ada, norm1_g, w_in, gla_w_lr, gla_b_lr, gla_norm_g,
              s5_lambda_re, s5_lambda_im, s5_log_dt, s5_b_re, s5_b_im, s5_c_re, s5_c_im,
              s5_d, s5_w_glu, gdn_conv_w, gdn_a_log, gdn_dt_bias, gdn_norm_g,
              w_branch_gla, w_branch_s5, w_branch_gdn, w_out, norm2_g, w_ffn_in, w_ffn_out,
              final_g):
    bsz, l, d = x.shape
    dtype = x.dtype
    split_points = np.cumsum(IN_SIZES)[:-1].tolist()
    cond = jax.nn.silu(c)
    for i in range(DEPTH):
        mod = cond @ w_ada[i] + b_ada[i]
        sh1, sc1, gt1, sh2, sc2, gt2 = jnp.split(mod[:, None, :], 6, axis=-1)
        h = rmsnorm(x, norm1_g[i]) * (1.0 + sc1) + sh1
        z = h @ w_in[i]
        (gq, gk, gv, glr, gog, s5u, dqkv, dbeta, da, dog, zg) = jnp.split(z, split_points, axis=-1)
        y_gla = gla_mixer(gq, gk, gv, glr, gog, gla_w_lr[i], gla_b_lr[i], gla_norm_g[i]).astype(dtype)
        y_s5 = s5_mixer(s5u, s5_lambda_re[i], s5_lambda_im[i], s5_log_dt[i], s5_b_re[i], s5_b_im[i],
                        s5_c_re[i], s5_c_im[i], s5_d[i], s5_w_glu[i]).astype(dtype)
        y_gdn = gdn_mixer(dqkv, dbeta, da, dog, gdn_conv_w[i], gdn_a_log[i], gdn_dt_bias[i],
                          gdn_norm_g[i]).astype(dtype)
        gates = jax.nn.sigmoid(zg.reshape(bsz, l, N_BRANCH, d))
        merged = (gates[:, :, 0] * (y_gla @ w_branch_gla[i])
                  + gates[:, :, 1] * (y_s5 @ w_branch_s5[i])
                  + gates[:, :, 2] * (y_gdn @ w_branch_gdn[i]))
        x = x + gt1 * (merged @ w_out[i])
        h = rmsnorm(x, norm2_g[i]) * (1.0 + sc2) + sh2
        a, b = jnp.split(h @ w_ffn_in[i], 2, axis=-1)
        x = x + gt2 * ((jax.nn.silu(a) * b) @ w_ffn_out[i])
    return rmsnorm(x, final_g)
```

```python
import functools
import math

import jax
import jax.numpy as jnp
from jax import lax
from jax.experimental import pallas as pl
from jax.experimental.pallas import tpu as pltpu

F32 = jnp.float32
BF16 = jnp.bfloat16
HI = lax.Precision.HIGHEST

EPS = 1e-6
CHUNK = 64
LANE = 128
VMEM_LIMIT = 48 * 1024 * 1024

GLA_HEADS, GLA_DK, GLA_DV, GLA_LOWRANK = 4, 64, 128, 16
GLA_GATE_NORM = 16.0
S5_GROUPS, S5_GROUP_CH, S5_STATE = 32, 16, 64
GDN_HEADS, GDN_DK, GDN_DV, GDN_CONV = 4, 128, 128, 4
N_BRANCH = 3

ZM_GQ, ZM_GK, ZM_GV, ZM_GOG = 0, 256, 512, 1024
ZM_DQKV, ZM_DOG, ZM_GLR, ZM_DBA = 1536, 3072, 3584, 3712
ZM_WIDTH = 3840


def _mm(a, b):
    return jnp.dot(a.astype(BF16), b.astype(BF16), preferred_element_type=F32)


def _mm_nt(a, b):
    return lax.dot_general(a.astype(BF16), b.astype(BF16), (((1,), (1,)), ((), ())),
                           preferred_element_type=F32)


def _mm_tn(a, b):
    return lax.dot_general(a.astype(BF16), b.astype(BF16), (((0,), (0,)), ((), ())),
                           preferred_element_type=F32)


def _mm_hi(a, b):
    return jnp.dot(a, b, precision=HI, preferred_element_type=F32)


def _sigmoid(x):
    return 1.0 / (1.0 + jnp.exp(-x))


def _silu(x):
    return x * _sigmoid(x)


def _softplus(x):
    return jnp.maximum(x, 0.0) + jnp.log(1.0 + jnp.exp(-jnp.abs(x)))


def _log_sigmoid(x):
    return -_softplus(-x)


def _gelu_tanh(x):
    c = math.sqrt(2.0 / math.pi)
    return 0.5 * x * (1.0 + jnp.tanh(c * (x + 0.044715 * (x * x * x))))


def _norm_mod(x, g, sc, sh):
    ms = jnp.mean(x * x, axis=-1, keepdims=True)
    return (x * lax.rsqrt(ms + EPS) * g) * (1.0 + sc) + sh


def _params(*sem):
    return pltpu.CompilerParams(dimension_semantics=sem, vmem_limit_bytes=VMEM_LIMIT)


def _ada_kernel(c_ref, w_ref, b_ref, o_ref):
    c = c_ref[...]
    o_ref[0] = _mm_hi(_silu(c), w_ref[0]) + b_ref[0]


def ada_modulation(c, w_ada, b_ada):
    depth, d, d6 = w_ada.shape
    bsz = c.shape[0]
    rows = -(-bsz // 8) * 8
    c_pad = jnp.pad(c, ((0, rows - bsz), (0, 0)))
    out = pl.pallas_call(
        _ada_kernel,
        grid=(depth, d6 // d),
        in_specs=[pl.BlockSpec((rows, d), lambda i, j: (0, 0)),
                  pl.BlockSpec((1, d, d), lambda i, j: (i, 0, j)),
                  pl.BlockSpec((1, 1, d), lambda i, j: (i, 0, j))],
        out_specs=pl.BlockSpec((1, rows, d), lambda i, j: (i, 0, j)),
        out_shape=jax.ShapeDtypeStruct((depth, rows, d6), F32),
        compiler_params=_params("parallel", "parallel"),
        name="ada_modulation",
    )(c_pad, w_ada, b_ada.reshape(depth, 1, d6))
    return out[:, :bsz]


def _inproj_kernel(x_ref, g_ref, sc_ref, sh_ref, w_ref, o_ref, h_ref):
    @pl.when(pl.program_id(1) == 0)
    def _():
        h_ref[...] = _norm_mod(x_ref[...], g_ref[...], sc_ref[...], sh_ref[...]).astype(BF16)

    o_ref[...] = jnp.dot(h_ref[...], w_ref[...], preferred_element_type=F32)


def norm_mod_matmul(x2, g, sc, sh, w, seq, tm, tn):
    t, d = x2.shape
    c = w.shape[1]
    per_b = seq // tm
    return pl.pallas_call(
        _inproj_kernel,
        grid=(t // tm, c // tn),
        in_specs=[pl.BlockSpec((tm, d), lambda i, j: (i, 0)),
                  pl.BlockSpec((1, d), lambda i, j: (0, 0)),
                  pl.BlockSpec((None, 1, d), lambda i, j: (i // per_b, 0, 0)),
                  pl.BlockSpec((None, 1, d), lambda i, j: (i // per_b, 0, 0)),
                  pl.BlockSpec((d, tn), lambda i, j: (0, j))],
        out_specs=pl.BlockSpec((tm, tn), lambda i, j: (i, j)),
        out_shape=jax.ShapeDtypeStruct((t, c), F32),
        scratch_shapes=[pltpu.VMEM((tm, d), BF16)],
        compiler_params=_params("parallel", "arbitrary"),
        name="norm_mod_matmul",
    )(x2, g, sc, sh, w)


def _s5_inproj_kernel(x_ref, g_ref, sc_ref, sh_ref, wt_ref, o_ref, *, jt, d):
    for jj in range(jt):
        x = x_ref[:, jj * d:(jj + 1) * d]
        h = _norm_mod(x, g_ref[...], sc_ref[...], sh_ref[...]).astype(BF16)
        o_ref[jj] = lax.dot_general(wt_ref[...], h, (((1,), (1,)), ((), ())),
                                    preferred_element_type=F32).astype(BF16)


def s5_inproj(xc, g, sc_rows, sh_rows, wt, jt):
    r, cd = xc.shape
    d = cd // CHUNK
    s5w = wt.shape[0]
    return pl.pallas_call(
        functools.partial(_s5_inproj_kernel, jt=jt, d=d),
        grid=(r // LANE, CHUNK // jt),
        in_specs=[pl.BlockSpec((LANE, jt * d), lambda i, j: (i, j)),
                  pl.BlockSpec((1, d), lambda i, j: (0, 0)),
                  pl.BlockSpec((LANE, d), lambda i, j: (i, 0)),
                  pl.BlockSpec((LANE, d), lambda i, j: (i, 0)),
                  pl.BlockSpec((s5w, d), lambda i, j: (0, 0))],
        out_specs=pl.BlockSpec((jt, s5w, LANE), lambda i, j: (j, 0, i)),
        out_shape=jax.ShapeDtypeStruct((CHUNK, s5w, r), BF16),
        compiler_params=_params("parallel", "parallel"),
        name="s5_inproj",
    )(xc, g, sc_rows, sh_rows, wt)


S5_CW = CHUNK * S5_GROUP_CH


def _cpow(lr_dt, li_dt, e):
    mag = jnp.exp(lr_dt * e)
    ang = li_dt * e
    return mag * jnp.cos(ang), mag * jnp.sin(ang)


def _s5_param_kernel(ldt_ref, lrc_ref, lic_ref, lrr_ref, lir_ref, bre_ref, bim_ref,
                     c1_ref, c1t_ref, c2t_ref, dcol_ref, m_ref, et_ref, ft_ref, ac_ref):
    p, cw, h = S5_STATE, S5_CW, S5_GROUP_CH
    dt = jnp.exp(ldt_ref[0])
    lrc, lic = lrc_ref[0], lic_ref[0]
    ab_re, ab_im = _cpow(lrc * dt, lic * dt, 1.0)
    den = lrc * lrc + lic * lic
    nr, ni = ab_re - 1.0, ab_im
    w_re = (nr * lrc + ni * lic) / den
    w_im = (ni * lrc - nr * lic) / den
    bre, bim = bre_ref[0], bim_ref[0]
    bb_re = w_re * bre - w_im * bim
    bb_im = w_re * bim + w_im * bre
    lane = lax.broadcasted_iota(jnp.int32, (1, cw), 1)
    e_m = (CHUNK - 1 - lane // h).astype(F32)
    p_re, p_im = _cpow(lrc * dt, lic * dt, e_m)
    e_re = p_re * bb_re - p_im * bb_im
    e_im = p_re * bb_im + p_im * bb_re
    et = jnp.concatenate([e_re, e_im], axis=0)
    et_ref[0] = et.astype(BF16)
    a_re, a_im = _cpow(lrc * dt, lic * dt, float(CHUNK))
    ac_ref[0] = jnp.concatenate([a_re, a_im], axis=0)
    sgn = jnp.where(lax.broadcasted_iota(jnp.int32, (1, 2 * p), 1) < p, 1.0, -1.0)
    krev = _mm_hi(c1_ref[0] * sgn, et)
    row = lax.broadcasted_iota(jnp.int32, (h, cw), 0)
    col = lax.broadcasted_iota(jnp.int32, (h, cw), 1)
    krev = krev + jnp.where(col == (cw - h) + row, dcol_ref[0], 0.0)
    rrev = jnp.concatenate([krev, jnp.zeros_like(krev)], axis=1)
    per_tile = LANE // h
    rolled = [rrev if r == 0 else pltpu.roll(rrev, 2 * cw - r * h, axis=1) for r in range(per_tile)]
    for i in range(CHUNK):
        s = (CHUNK - 1 - i) * h
        a, r = s // LANE, (s % LANE) // h
        m_ref[0, i * h:(i + 1) * h, :] = rolled[r][:, a * LANE:a * LANE + cw].astype(BF16)
    rowi = lax.broadcasted_iota(jnp.int32, (cw, 1), 0)
    f_i = (rowi // h + 1).astype(F32)
    q_re, q_im = _cpow(lrr_ref[0] * dt, lir_ref[0] * dt, f_i)
    ft_ref[0] = (c1t_ref[0] * sgn * q_re - c2t_ref[0] * q_im).astype(BF16)


def s5_params(lam_re, lam_im, log_dt, b_re, b_im, c_re, c_im, dpar):
    ng = lam_re.shape[0] * lam_re.shape[1]
    p, h, cw = S5_STATE, S5_GROUP_CH, S5_CW
    lam_re = lam_re.reshape(ng, p)
    lam_im = lam_im.reshape(ng, p)
    c_re = c_re.reshape(ng, h, p)
    c_im = c_im.reshape(ng, h, p)
    c1 = jnp.concatenate([c_re, c_im], axis=-1)
    c2 = jnp.concatenate([c_im, c_re], axis=-1)
    args = (log_dt.reshape(ng, 1, 1),
            lam_re.reshape(ng, p, 1), lam_im.reshape(ng, p, 1),
            jnp.tile(lam_re.reshape(ng, 1, p), (1, 1, 2)), jnp.tile(lam_im.reshape(ng, 1, p), (1, 1, 2)),
            jnp.tile(b_re.reshape(ng, p, h), (1, 1, CHUNK)), jnp.tile(b_im.reshape(ng, p, h), (1, 1, CHUNK)),
            c1, jnp.tile(c1, (1, CHUNK, 1)), jnp.tile(c2, (1, CHUNK, 1)),
            dpar.reshape(ng, h, 1))

    def spec(a):
        return pl.BlockSpec((1,) + a.shape[1:], lambda i: (i, 0, 0))

    return pl.pallas_call(
        _s5_param_kernel,
        grid=(ng,),
        in_specs=[spec(a) for a in args],
        out_specs=[pl.BlockSpec((1, cw, cw), lambda i: (i, 0, 0)),
                   pl.BlockSpec((1, 2 * p, cw), lambda i: (i, 0, 0)),
                   pl.BlockSpec((1, cw, 2 * p), lambda i: (i, 0, 0)),
                   pl.BlockSpec((1, 2 * p, 1), lambda i: (i, 0, 0))],
        out_shape=[jax.ShapeDtypeStruct((ng, cw, cw), BF16),
                   jax.ShapeDtypeStruct((ng, 2 * p, cw), BF16),
                   jax.ShapeDtypeStruct((ng, cw, 2 * p), BF16),
                   jax.ShapeDtypeStruct((ng, 2 * p, 1), F32)],
        compiler_params=_params("parallel"),
        name="s5_params",
    )(*args)


def _s5_mix_kernel(u_ref, m_ref, et_ref, ft_ref, ac_ref, y_ref, *, nchunk):
    p = S5_STATE
    u = u_ref[...].reshape(S5_CW, u_ref.shape[-1])
    r = u.shape[1]
    s = jnp.dot(et_ref[0], u, preferred_element_type=F32)
    s_re, s_im = s[:p], s[p:]
    a = ac_ref[0]
    a_re, a_im = a[:p], a[p:]
    n_idx = lax.broadcasted_iota(jnp.int32, (1, r), 1) % nchunk
    shift = 1
    while shift < nchunk:
        keep = n_idx >= shift
        t_re = jnp.where(keep, pltpu.roll(s_re, shift, axis=1), 0.0)
        t_im = jnp.where(keep, pltpu.roll(s_im, shift, axis=1), 0.0)
        s_re, s_im = (s_re + a_re * t_re - a_im * t_im, s_im + a_re * t_im + a_im * t_re)
        a_re, a_im = a_re * a_re - a_im * a_im, 2.0 * a_re * a_im
        shift *= 2
    keep = n_idx >= 1
    h_prev = jnp.concatenate([jnp.where(keep, pltpu.roll(s_re, 1, axis=1), 0.0),
                              jnp.where(keep, pltpu.roll(s_im, 1, axis=1), 0.0)], axis=0)
    y = jnp.dot(m_ref[0], u, preferred_element_type=F32)
    y = y + jnp.dot(ft_ref[0], h_prev.astype(BF16), preferred_element_type=F32)
    y_ref[...] = y.reshape(y_ref.shape)


def s5_mix(ut, m, et, ft, ac, layer, nchunk):
    _, s5w, r = ut.shape
    g, h, p, cw = S5_GROUPS, S5_GROUP_CH, S5_STATE, S5_CW
    base = layer * g
    return pl.pallas_call(
        functools.partial(_s5_mix_kernel, nchunk=nchunk),
        grid=(g,),
        in_specs=[pl.BlockSpec((CHUNK, h, r), lambda i: (0, i, 0)),
                  pl.BlockSpec((1, cw, cw), lambda i: (base + i, 0, 0)),
                  pl.BlockSpec((1, 2 * p, cw), lambda i: (base + i, 0, 0)),
                  pl.BlockSpec((1, cw, 2 * p), lambda i: (base + i, 0, 0)),
                  pl.BlockSpec((1, 2 * p, 1), lambda i: (base + i, 0, 0))],
        out_specs=pl.BlockSpec((CHUNK, h, r), lambda i: (0, i, 0)),
        out_shape=jax.ShapeDtypeStruct((CHUNK, s5w, r), F32),
        compiler_params=_params("parallel"),
        name="s5_mix",
    )(ut, m, et, ft, ac)


def _gla_kernel(q_ref, k_ref, v_ref, og_ref, lr_ref, wlr_ref, blr_ref, ng_ref, o_ref, st_ref, *, nc):
    hd, dk, dv = GLA_HEADS, GLA_DK, GLA_DV

    @pl.when(pl.program_id(1) == 0)
    def _():
        st_ref[...] = jnp.zeros_like(st_ref)

    ri = lax.broadcasted_iota(jnp.int32, (CHUNK, CHUNK), 0)
    ci = lax.broadcasted_iota(jnp.int32, (CHUNK, CHUNK), 1)
    incl = ri >= ci
    ltri = incl.astype(F32)
    lane_k = lax.broadcasted_iota(jnp.int32, (1, hd * dk), 1)
    srow = lax.broadcasted_iota(jnp.int32, (hd * dv, hd * dk), 0)
    scol = lax.broadcasted_iota(jnp.int32, (hd * dv, hd * dk), 1)
    same_head = (srow // dv) == (scol // dk)
    wlr, blr, ng = wlr_ref[...], blr_ref[...], ng_ref[...]

    for c in range(nc):
        rows = pl.ds(c * CHUNK, CHUNK)
        q, k, v = q_ref[rows, :], k_ref[rows, :], v_ref[rows, :]
        g = _log_sigmoid(_mm_hi(lr_ref[rows, :], wlr) + blr) * (1.0 / GLA_GATE_NORM)
        bc = _mm_hi(ltri, g)
        bl = bc[CHUNK - 1:CHUNK, :]
        q_e = q * (dk ** -0.5) * jnp.exp(bc)
        k_e = k * jnp.exp(-bc)
        k_d = k * jnp.exp(bl - bc)
        st = st_ref[...]
        o_inter = _mm_nt(q_e, st)
        outs = []
        for h in range(hd):
            qh = jnp.where((lane_k // dk) == h, q_e, 0.0)
            sc = jnp.where(incl, _mm_nt(qh, k_e), 0.0)
            outs.append(_mm(sc, v[:, h * dv:(h + 1) * dv]))
        o = jnp.concatenate(outs, axis=1) + o_inter
        st_ref[...] = jnp.exp(bl) * st + jnp.where(same_head, _mm_tn(v, k_d), 0.0)
        og = og_ref[rows, :]
        ys = []
        for h in range(hd):
            oh = o[:, h * dv:(h + 1) * dv]
            ms = jnp.mean(oh * oh, axis=-1, keepdims=True)
            ys.append(oh * lax.rsqrt(ms + EPS) * ng)
        o_ref[rows, :] = jnp.concatenate(ys, axis=1) * _silu(og)


def gla_mix(zm, wlr_pad, blr, ng, bsz, seq, tc):
    t = zm.shape[0]
    per_b = seq // tc
    hd, dk, dv = GLA_HEADS, GLA_DK, GLA_DV

    def zspec(width, off):
        blk = off // width
        return pl.BlockSpec((tc, width), lambda b, i: (b * per_b + i, blk))

    def full(a):
        return pl.BlockSpec(a.shape, lambda b, i: (0,) * a.ndim)

    return pl.pallas_call(
        functools.partial(_gla_kernel, nc=tc // CHUNK),
        grid=(bsz, per_b),
        in_specs=[zspec(hd * dk, ZM_GQ), zspec(hd * dk, ZM_GK), zspec(hd * dv, ZM_GV),
                  zspec(hd * dv, ZM_GOG), zspec(LANE, ZM_GLR), full(wlr_pad), full(blr), full(ng)],
        out_specs=pl.BlockSpec((tc, hd * dv), lambda b, i: (b * per_b + i, 0)),
        out_shape=jax.ShapeDtypeStruct((t, hd * dv), F32),
        scratch_shapes=[pltpu.VMEM((hd * dv, hd * dk), F32)],
        compiler_params=_params("parallel", "arbitrary"),
        name="gla_mix",
    )(zm, zm, zm, zm, zm, wlr_pad, blr, ng)


def _gdn_kernel(qkv_ref, og_ref, ba_ref, cw_ref, alog_ref, dtb_ref, ng_ref, o_ref,
                xbuf_ref, s_ref, *, nc):
    hd, dk, dv, kc = GDN_HEADS, GDN_DK, GDN_DV, GDN_CONV
    tc = nc * CHUNK
    pad = 8

    @pl.when(pl.program_id(1) == 0)
    def _():
        s_ref[...] = jnp.zeros_like(s_ref)
        xbuf_ref[0:pad, :] = jnp.zeros((pad, xbuf_ref.shape[1]), F32)

    @pl.when(pl.program_id(1) != 0)
    def _():
        xbuf_ref[0:pad, :] = xbuf_ref[tc:tc + pad, :]

    xbuf_ref[pad:pad + tc, :] = qkv_ref[...]

    ri = lax.broadcasted_iota(jnp.int32, (CHUNK, CHUNK), 0)
    ci = lax.broadcasted_iota(jnp.int32, (CHUNK, CHUNK), 1)
    incl = ri >= ci
    strict = ri > ci
    ltri = incl.astype(F32)
    cw = cw_ref[...]
    ng = ng_ref[...]

    for c in range(nc):
        base = pad + c * CHUNK
        acc = cw[kc - 1:kc, :] * xbuf_ref[base:base + CHUNK, :]
        for i in range(kc - 1):
            off = base - (kc - 1) + i
            acc = acc + cw[i:i + 1, :] * xbuf_ref[off:off + CHUNK, :]
        x = _silu(acc)
        rows = pl.ds(c * CHUNK, CHUNK)
        ba = ba_ref[rows, :]
        beta_all = _sigmoid(ba)
        g_all = -jnp.exp(alog_ref[...]) * _softplus(ba + dtb_ref[...])
        gam_all = _mm_hi(ltri, g_all)
        gam_t = gam_all.T
        og = og_ref[rows, :]
        ys = []
        for h in range(hd):
            q = x[:, h * dk:(h + 1) * dk]
            k = x[:, (hd + h) * dk:(hd + h + 1) * dk]
            v = x[:, (2 * hd + h) * dk:(2 * hd + h + 1) * dk]
            q = q * lax.rsqrt(jnp.sum(q * q, axis=-1, keepdims=True) + EPS) * (dk ** -0.5)
            k = k * lax.rsqrt(jnp.sum(k * k, axis=-1, keepdims=True) + EPS)
            beta = beta_all[:, h:h + 1]
            gam = gam_all[:, hd + h:hd + h + 1]
            gam_row = gam_t[hd + h:hd + h + 1, :]
            gam_last = gam[CHUNK - 1:CHUNK, :]
            dmask = jnp.where(incl, jnp.exp(gam - gam_row), 0.0)
            k_beta = k * beta
            v_beta = v * beta
            egam = jnp.exp(gam)
            low = jnp.where(strict, _mm_nt(k_beta, k) * dmask, 0.0)
            sol = jnp.concatenate([v_beta, k_beta * egam], axis=1)
            sol = sol - _mm(low, sol)
            pw = low
            for _ in range(5):
                pw = _mm(pw, pw)
                sol = sol + _mm(pw, sol)
            u_c, w_c = sol[:, :dv], sol[:, dv:]
            attn = _mm_nt(q, k) * dmask
            q_dec = q * egam
            k_dec = k * jnp.exp(gam_last - gam)
            s = s_ref[h]
            v_new = u_c - _mm(w_c, s)
            o_h = _mm(q_dec, s) + _mm(attn, v_new)
            s_ref[h] = jnp.exp(gam_last) * s + _mm_tn(k_dec, v_new)
            ms = jnp.mean(o_h * o_h, axis=-1, keepdims=True)
            ys.append(o_h * lax.rsqrt(ms + EPS) * ng)
        o_ref[rows, :] = jnp.concatenate(ys, axis=1) * _silu(og)


def gdn_mix(zm, conv_w_pad, alog_row, dtb_row, ng, bsz, seq, tc):
    t = zm.shape[0]
    per_b = seq // tc
    hd, dk, dv = GDN_HEADS, GDN_DK, GDN_DV
    qkv_w = 3 * hd * dk

    def zspec(width, off):
        blk = off // width
        return pl.BlockSpec((tc, width), lambda b, i: (b * per_b + i, blk))

    def full(a):
        return pl.BlockSpec(a.shape, lambda b, i: (0,) * a.ndim)

    return pl.pallas_call(
        functools.partial(_gdn_kernel, nc=tc // CHUNK),
        grid=(bsz, per_b),
        in_specs=[zspec(qkv_w, ZM_DQKV), zspec(hd * dv, ZM_DOG), zspec(LANE, ZM_DBA),
                  full(conv_w_pad), full(alog_row), full(dtb_row), full(ng)],
        out_specs=pl.BlockSpec((tc, hd * dv), lambda b, i: (b * per_b + i, 0)),
        out_shape=jax.ShapeDtypeStruct((t, hd * dv), F32),
        scratch_shapes=[pltpu.VMEM((tc + 8, qkv_w), F32), pltpu.VMEM((hd, dk, dv), F32)],
        compiler_params=_params("parallel", "arbitrary"),
        name="gdn_mix",
    )(zm, zm, zm, conv_w_pad, alog_row, dtb_row, ng)


def _merge_kernel(x_ref, zg_ref, ygla_ref, ygdn_ref, yst_ref, gt_ref,
                  wglu_ref, wbg_ref, wbs_ref, wbd_ref, wout_ref, o_ref, *, jt, d, bw):
    for jj in range(jt):
        x = x_ref[:, jj * d:(jj + 1) * d]
        zg = zg_ref[:, jj * N_BRANCH * d:(jj + 1) * N_BRANCH * d]
        y_gla = ygla_ref[:, jj * bw:(jj + 1) * bw]
        y_gdn = ygdn_ref[:, jj * bw:(jj + 1) * bw]
        ys = _gelu_tanh(yst_ref[jj])
        glu = _mm_tn(ys, wglu_ref[...])
        y_s5 = glu[:, :bw] * _sigmoid(glu[:, bw:])
        merged = (_sigmoid(zg[:, 0:d]) * _mm(y_gla, wbg_ref[...])
                  + _sigmoid(zg[:, d:2 * d]) * _mm(y_s5, wbs_ref[...])
                  + _sigmoid(zg[:, 2 * d:3 * d]) * _mm(y_gdn, wbd_ref[...]))
        o_ref[:, jj * d:(jj + 1) * d] = x + gt_ref[...] * _mm(merged, wout_ref[...])


def merge_out(xc, zgc, yglac, ygdnc, yst, gt_rows, wglu, wbg, wbs, wbd, wout, jt):
    r, cd = xc.shape
    d = cd // CHUNK
    bw = yglac.shape[1] // CHUNK

    def full(a):
        return pl.BlockSpec(a.shape, lambda i, j: (0,) * a.ndim)

    return pl.pallas_call(
        functools.partial(_merge_kernel, jt=jt, d=d, bw=bw),
        grid=(r // LANE, CHUNK // jt),
        in_specs=[pl.BlockSpec((LANE, jt * d), lambda i, j: (i, j)),
                  pl.BlockSpec((LANE, jt * N_BRANCH * d), lambda i, j: (i, j)),
                  pl.BlockSpec((LANE, jt * bw), lambda i, j: (i, j)),
                  pl.BlockSpec((LANE, jt * bw), lambda i, j: (i, j)),
                  pl.BlockSpec((jt, yst.shape[1], LANE), lambda i, j: (j, 0, i)),
                  pl.BlockSpec((LANE, d), lambda i, j: (i, 0)),
                  full(wglu), full(wbg), full(wbs), full(wbd), full(wout)],
        out_specs=pl.BlockSpec((LANE, jt * d), lambda i, j: (i, j)),
        out_shape=jax.ShapeDtypeStruct((r, cd), F32),
        compiler_params=_params("parallel", "parallel"),
        name="merge_out",
    )(xc, zgc, yglac, ygdnc, yst, gt_rows, wglu, wbg, wbs, wbd, wout)


def _ffn_kernel(x_ref, g_ref, sc_ref, sh_ref, gt_ref, wa_ref, wb_ref, wo_ref, fg_ref, o_ref,
                h_ref, acc_ref, *, final_norm):
    j = pl.program_id(1)

    @pl.when(j == 0)
    def _():
        h_ref[...] = _norm_mod(x_ref[...], g_ref[...], sc_ref[...], sh_ref[...]).astype(BF16)
        acc_ref[...] = jnp.zeros_like(acc_ref)

    h = h_ref[...]
    a = jnp.dot(h, wa_ref[...], preferred_element_type=F32)
    b = jnp.dot(h, wb_ref[...], preferred_element_type=F32)
    acc_ref[...] += _mm(_silu(a) * b, wo_ref[...])

    @pl.when(j == pl.num_programs(1) - 1)
    def _():
        y = x_ref[...] + gt_ref[...] * acc_ref[...]
        if final_norm:
            ms = jnp.mean(y * y, axis=-1, keepdims=True)
            y = y * lax.rsqrt(ms + EPS) * fg_ref[...]
        o_ref[...] = y


def ffn_block(x2, g, sc, sh, gt, w_in, w_out, final_g, seq, tm, tf, final_norm):
    t, d = x2.shape
    dff = w_out.shape[0]
    nf = dff // tf
    per_b = seq // tm

    def bspec():
        return pl.BlockSpec((None, 1, d), lambda i, j: (i // per_b, 0, 0))

    return pl.pallas_call(
        functools.partial(_ffn_kernel, final_norm=final_norm),
        grid=(t // tm, nf),
        in_specs=[pl.BlockSpec((tm, d), lambda i, j: (i, 0)),
                  pl.BlockSpec((1, d), lambda i, j: (0, 0)),
                  bspec(), bspec(), bspec(),
                  pl.BlockSpec((d, tf), lambda i, j: (0, j)),
                  pl.BlockSpec((d, tf), lambda i, j: (0, nf + j)),
                  pl.BlockSpec((tf, d), lambda i, j: (j, 0)),
                  pl.BlockSpec((1, d), lambda i, j: (0, 0))],
        out_specs=pl.BlockSpec((tm, d), lambda i, j: (i, 0)),
        out_shape=jax.ShapeDtypeStruct((t, d), F32),
        scratch_shapes=[pltpu.VMEM((tm, d), BF16), pltpu.VMEM((tm, d), F32)],
        compiler_params=_params("parallel", "arbitrary"),
        name="ffn_block",
    )(x2, g, sc, sh, gt, w_in, w_in, w_out, final_g)


def _forward(x, c, w_ada, b_ada, norm1_g, w_in, gla_w_lr, gla_b_lr, gla_norm_g,
             s5_lambda_re, s5_lambda_im, s5_log_dt, s5_b_re, s5_b_im, s5_c_re, s5_c_im,
             s5_d, s5_w_glu, gdn_conv_w, gdn_a_log, gdn_dt_bias, gdn_norm_g,
             w_branch_gla, w_branch_s5, w_branch_gdn, w_out, norm2_g, w_ffn_in, w_ffn_out,
             final_g):
    bsz, seq, d = x.shape
    depth = w_ada.shape[0]
    t = bsz * seq
    nchunk = seq // CHUNK
    r = bsz * nchunk
    gqk, gw = GLA_HEADS * GLA_DK, GLA_HEADS * GLA_DV
    s5w = S5_GROUPS * S5_GROUP_CH
    dqkv, dw = 3 * GDN_HEADS * GDN_DK, GDN_HEADS * GDN_DV

    mod = ada_modulation(c, w_ada, b_ada).reshape(depth, bsz, 6, 1, d)
    s5m, s5et, s5ft, s5ac = s5_params(s5_lambda_re, s5_lambda_im, s5_log_dt, s5_b_re, s5_b_im,
                                      s5_c_re, s5_c_im, s5_d)

    o_gq, o_gk, o_gv = 0, gqk, 2 * gqk
    o_glr = o_gv + gw
    o_gog = o_glr + GLA_LOWRANK
    o_s5 = o_gog + gw
    o_dqkv = o_s5 + s5w
    o_dbeta = o_dqkv + dqkv
    o_da = o_dbeta + GDN_HEADS
    o_dog = o_da + GDN_HEADS
    o_zg = o_dog + dw

    x2 = x.reshape(t, d)
    for i in range(depth):
        wi = w_in[i]
        zpad = jnp.zeros((d, LANE - GLA_LOWRANK), F32)
        zpad2 = jnp.zeros((d, LANE - 2 * GDN_HEADS), F32)
        w_zm = jnp.concatenate([
            wi[:, o_gq:o_gq + gqk], wi[:, o_gk:o_gk + gqk], wi[:, o_gv:o_gv + gw],
            wi[:, o_gog:o_gog + gw], wi[:, o_dqkv:o_dqkv + dqkv], wi[:, o_dog:o_dog + dw],
            wi[:, o_glr:o_glr + GLA_LOWRANK], zpad,
            wi[:, o_dbeta:o_dbeta + 2 * GDN_HEADS], zpad2], axis=1).astype(BF16)
        w_zg = wi[:, o_zg:].astype(BF16)
        w_s5t = wi[:, o_s5:o_s5 + s5w].T.astype(BF16)

        sh1, sc1, gt1, sh2, sc2, gt2 = [mod[i, :, k] for k in range(6)]
        g1 = norm1_g[i].reshape(1, d)
        rows = lambda a: jnp.broadcast_to(a, (bsz, nchunk, d)).reshape(r, d)

        zm = norm_mod_matmul(x2, g1, sc1, sh1, w_zm, seq, tm=1024, tn=1280)
        zg = norm_mod_matmul(x2, g1, sc1, sh1, w_zg, seq, tm=1024, tn=1024)
        xc = x2.reshape(r, CHUNK * d)
        ut = s5_inproj(xc, g1, rows(sc1), rows(sh1), w_s5t, jt=8)

        wlr_pad = jnp.pad(gla_w_lr[i], ((0, LANE - GLA_LOWRANK), (0, 0)))
        y_gla = gla_mix(zm, wlr_pad, gla_b_lr[i].reshape(1, gqk), gla_norm_g[i].reshape(1, GLA_DV),
                        bsz, seq, tc=256)
        yst = s5_mix(ut, s5m, s5et, s5ft, s5ac, i, nchunk)
        conv_pad = jnp.pad(gdn_conv_w[i], ((0, 8 - GDN_CONV), (0, 0)))
        alog_row = jnp.pad(gdn_a_log[i], (GDN_HEADS, LANE - 2 * GDN_HEADS)).reshape(1, LANE)
        dtb_row = jnp.pad(gdn_dt_bias[i], (GDN_HEADS, LANE - 2 * GDN_HEADS)).reshape(1, LANE)
        y_gdn = gdn_mix(zm, conv_pad, alog_row, dtb_row, gdn_norm_g[i].reshape(1, GDN_DV),
                        bsz, seq, tc=256)

        xc = merge_out(xc, zg.reshape(r, CHUNK * N_BRANCH * d), y_gla.reshape(r, CHUNK * gw),
                       y_gdn.reshape(r, CHUNK * dw), yst, rows(gt1),
                       s5_w_glu[i].astype(BF16), w_branch_gla[i].astype(BF16),
                       w_branch_s5[i].astype(BF16), w_branch_gdn[i].astype(BF16),
                       w_out[i].astype(BF16), jt=2)
        x2 = xc.reshape(t, d)
        x2 = ffn_block(x2, norm2_g[i].reshape(1, d), sc2, sh2, gt2,
                       w_ffn_in[i].astype(BF16), w_ffn_out[i].astype(BF16),
                       final_g.reshape(1, d), seq, tm=1024, tf=256, final_norm=(i == depth - 1))
    return x2.reshape(bsz, seq, d)


def kernel(x, c, w_ada, b_ada, norm1_g, w_in, gla_w_lr, gla_b_lr, gla_norm_g, s5_lambda_re, s5_lambda_im, s5_log_dt, s5_b_re, s5_b_im, s5_c_re, s5_c_im, s5_d, s5_w_glu, gdn_conv_w, gdn_a_log, gdn_dt_bias, gdn_norm_g, w_branch_gla, w_branch_s5, w_branch_gdn, w_out, norm2_g, w_ffn_in, w_ffn_out, final_g):
    return _forward(x, c, w_ada, b_ada, norm1_g, w_in, gla_w_lr, gla_b_lr, gla_norm_g,
                    s5_lambda_re, s5_lambda_im, s5_log_dt, s5_b_re, s5_b_im, s5_c_re, s5_c_im,
                    s5_d, s5_w_glu, gdn_conv_w, gdn_a_log, gdn_dt_bias, gdn_norm_g,
                    w_branch_gla, w_branch_s5, w_branch_gdn, w_out, norm2_g, w_ffn_in, w_ffn_out,
                    final_g)
```

```python
import functools
import math

import jax
import jax.numpy as jnp
from jax import lax
from jax.experimental import pallas as pl
from jax.experimental.pallas import tpu as pltpu

F32 = jnp.float32
BF16 = jnp.bfloat16
HI = lax.Precision.HIGHEST

EPS = 1e-6
CHUNK = 64
LANE = 128
VMEM_LIMIT = 56 * 1024 * 1024

GLA_HEADS, GLA_DK, GLA_DV, GLA_LOWRANK = 4, 64, 128, 16
GLA_GATE_NORM = 16.0
S5_GROUPS, S5_GROUP_CH, S5_STATE = 32, 16, 64
GDN_HEADS, GDN_DK, GDN_DV, GDN_CONV = 4, 128, 128, 4
N_BRANCH = 3
MERGE_ROWS = 512

ZM_GQ, ZM_GK, ZM_GV, ZM_GOG = 0, 256, 512, 1024
ZM_DQKV, ZM_DOG, ZM_GLR, ZM_DBA = 1536, 3072, 3584, 3712
ZM_WIDTH = 3840


def _mm(a, b):
    return jnp.dot(a.astype(BF16), b.astype(BF16), preferred_element_type=F32)


def _mm_nt(a, b):
    return lax.dot_general(a.astype(BF16), b.astype(BF16), (((1,), (1,)), ((), ())),
                           preferred_element_type=F32)


def _mm_tn(a, b):
    return lax.dot_general(a.astype(BF16), b.astype(BF16), (((0,), (0,)), ((), ())),
                           preferred_element_type=F32)


def _mm_hi(a, b):
    return jnp.dot(a, b, precision=HI, preferred_element_type=F32)


def _sigmoid(x):
    return 1.0 / (1.0 + jnp.exp(-x))


def _silu(x):
    return x * _sigmoid(x)


def _softplus(x):
    return jnp.maximum(x, 0.0) + jnp.log(1.0 + jnp.exp(-jnp.abs(x)))


def _log_sigmoid(x):
    return -_softplus(-x)


def _gelu_tanh(x):
    c = math.sqrt(2.0 / math.pi)
    return 0.5 * x * (1.0 + jnp.tanh(c * (x + 0.044715 * (x * x * x))))


def _norm_mod(x, g, sc, sh):
    ms = jnp.mean(x * x, axis=-1, keepdims=True)
    return (x * lax.rsqrt(ms + EPS) * g) * (1.0 + sc) + sh


def _tok_major(ref):
    return pltpu.einshape("rjd->(jr)d", ref[...])


def _params(*sem):
    return pltpu.CompilerParams(dimension_semantics=sem, vmem_limit_bytes=VMEM_LIMIT)


def _ada_kernel(c_ref, w_ref, b_ref, o_ref):
    c = c_ref[...]
    o_ref[0] = _mm_hi(_silu(c), w_ref[0]) + b_ref[0]


def ada_modulation(c, w_ada, b_ada):
    depth, d, d6 = w_ada.shape
    bsz = c.shape[0]
    rows = -(-bsz // 8) * 8
    c_pad = jnp.pad(c, ((0, rows - bsz), (0, 0)))
    out = pl.pallas_call(
        _ada_kernel,
        grid=(depth, d6 // d),
        in_specs=[pl.BlockSpec((rows, d), lambda i, j: (0, 0)),
                  pl.BlockSpec((1, d, d), lambda i, j: (i, 0, j)),
                  pl.BlockSpec((1, 1, d), lambda i, j: (i, 0, j))],
        out_specs=pl.BlockSpec((1, rows, d), lambda i, j: (i, 0, j)),
        out_shape=jax.ShapeDtypeStruct((depth, rows, d6), F32),
        compiler_params=_params("parallel", "parallel"),
        name="ada_modulation",
    )(c_pad, w_ada, b_ada.reshape(depth, 1, d6))
    return out[:, :bsz]


def _inproj_kernel(x_ref, g_ref, sc_ref, sh_ref, w_ref, o_ref, h_ref):
    @pl.when(pl.program_id(1) == 0)
    def _():
        h_ref[...] = _norm_mod(x_ref[...], g_ref[...], sc_ref[...], sh_ref[...]).astype(BF16)

    o_ref[...] = jnp.dot(h_ref[...], w_ref[...], preferred_element_type=F32)


def norm_mod_matmul(x2, g, sc, sh, w, seq, tm, tn):
    t, d = x2.shape
    c = w.shape[1]
    per_b = seq // tm
    return pl.pallas_call(
        _inproj_kernel,
        grid=(t // tm, c // tn),
        in_specs=[pl.BlockSpec((tm, d), lambda i, j: (i, 0)),
                  pl.BlockSpec((1, d), lambda i, j: (0, 0)),
                  pl.BlockSpec((None, 1, d), lambda i, j: (i // per_b, 0, 0)),
                  pl.BlockSpec((None, 1, d), lambda i, j: (i // per_b, 0, 0)),
                  pl.BlockSpec((d, tn), lambda i, j: (0, j))],
        out_specs=pl.BlockSpec((tm, tn), lambda i, j: (i, j)),
        out_shape=jax.ShapeDtypeStruct((t, c), F32),
        scratch_shapes=[pltpu.VMEM((tm, d), BF16)],
        compiler_params=_params("parallel", "arbitrary"),
        name="norm_mod_matmul",
    )(x2, g, sc, sh, w)


def _s5_inproj_kernel(x_ref, g_ref, sc_ref, sh_ref, wt_ref, o_ref, *, jt):
    sc = jnp.tile(sc_ref[...], (jt, 1))
    sh = jnp.tile(sh_ref[...], (jt, 1))
    h = _norm_mod(_tok_major(x_ref), g_ref[...], sc, sh).astype(BF16)
    ut = lax.dot_general(wt_ref[...], h, (((1,), (1,)), ((), ())), preferred_element_type=F32)
    for jj in range(jt):
        o_ref[jj] = ut[:, jj * LANE:(jj + 1) * LANE].astype(BF16)


def s5_inproj(x3, g, sc_rows, sh_rows, wt, jt):
    r, _, d = x3.shape
    s5w = wt.shape[0]
    return pl.pallas_call(
        functools.partial(_s5_inproj_kernel, jt=jt),
        grid=(r // LANE, CHUNK // jt),
        in_specs=[pl.BlockSpec((LANE, jt, d), lambda i, j: (i, j, 0)),
                  pl.BlockSpec((1, d), lambda i, j: (0, 0)),
                  pl.BlockSpec((LANE, d), lambda i, j: (i, 0)),
                  pl.BlockSpec((LANE, d), lambda i, j: (i, 0)),
                  pl.BlockSpec((s5w, d), lambda i, j: (0, 0))],
        out_specs=pl.BlockSpec((jt, s5w, LANE), lambda i, j: (j, 0, i)),
        out_shape=jax.ShapeDtypeStruct((CHUNK, s5w, r), BF16),
        compiler_params=_params("parallel", "parallel"),
        name="s5_inproj",
    )(x3, g, sc_rows, sh_rows, wt)


S5_CW = CHUNK * S5_GROUP_CH


def _cpow(lr_dt, li_dt, e):
    mag = jnp.exp(lr_dt * e)
    ang = li_dt * e
    return mag * jnp.cos(ang), mag * jnp.sin(ang)


def _s5_param_kernel(ldt_ref, lrc_ref, lic_ref, lrr_ref, lir_ref, bre_ref, bim_ref,
                     c1_ref, c1t_ref, c2t_ref, dcol_ref, m_ref, et_ref, ft_ref, ac_ref):
    p, cw, h = S5_STATE, S5_CW, S5_GROUP_CH
    dt = jnp.exp(ldt_ref[0])
    lrc, lic = lrc_ref[0], lic_ref[0]
    ab_re, ab_im = _cpow(lrc * dt, lic * dt, 1.0)
    den = lrc * lrc + lic * lic
    nr, ni = ab_re - 1.0, ab_im
    w_re = (nr * lrc + ni * lic) / den
    w_im = (ni * lrc - nr * lic) / den
    bre, bim = bre_ref[0], bim_ref[0]
    bb_re = w_re * bre - w_im * bim
    bb_im = w_re * bim + w_im * bre
    lane = lax.broadcasted_iota(jnp.int32, (1, cw), 1)
    e_m = (CHUNK - 1 - lane // h).astype(F32)
    p_re, p_im = _cpow(lrc * dt, lic * dt, e_m)
    e_re = p_re * bb_re - p_im * bb_im
    e_im = p_re * bb_im + p_im * bb_re
    et = jnp.concatenate([e_re, e_im], axis=0)
    et_ref[0] = et.astype(BF16)
    a_re, a_im = _cpow(lrc * dt, lic * dt, float(CHUNK))
    ac_ref[0] = jnp.concatenate([a_re, a_im], axis=0)
    sgn = jnp.where(lax.broadcasted_iota(jnp.int32, (1, 2 * p), 1) < p, 1.0, -1.0)
    krev = _mm_hi(c1_ref[0] * sgn, et)
    row = lax.broadcasted_iota(jnp.int32, (h, cw), 0)
    col = lax.broadcasted_iota(jnp.int32, (h, cw), 1)
    krev = krev + jnp.where(col == (cw - h) + row, dcol_ref[0], 0.0)
    rrev = jnp.concatenate([krev, jnp.zeros_like(krev)], axis=1)
    per_tile = LANE // h
    rolled = [rrev if r == 0 else pltpu.roll(rrev, 2 * cw - r * h, axis=1) for r in range(per_tile)]
    for i in range(CHUNK):
        s = (CHUNK - 1 - i) * h
        a, r = s // LANE, (s % LANE) // h
        m_ref[0, i * h:(i + 1) * h, :] = rolled[r][:, a * LANE:a * LANE + cw].astype(BF16)
    rowi = lax.broadcasted_iota(jnp.int32, (cw, 1), 0)
    f_i = (rowi // h + 1).astype(F32)
    q_re, q_im = _cpow(lrr_ref[0] * dt, lir_ref[0] * dt, f_i)
    ft_ref[0] = (c1t_ref[0] * sgn * q_re - c2t_ref[0] * q_im).astype(BF16)


def s5_params(lam_re, lam_im, log_dt, b_re, b_im, c_re, c_im, dpar):
    ng = lam_re.shape[0] * lam_re.shape[1]
    p, h, cw = S5_STATE, S5_GROUP_CH, S5_CW
    lam_re = lam_re.reshape(ng, p)
    lam_im = lam_im.reshape(ng, p)
    c_re = c_re.reshape(ng, h, p)
    c_im = c_im.reshape(ng, h, p)
    c1 = jnp.concatenate([c_re, c_im], axis=-1)
    c2 = jnp.concatenate([c_im, c_re], axis=-1)
    args = (log_dt.reshape(ng, 1, 1),
            lam_re.reshape(ng, p, 1), lam_im.reshape(ng, p, 1),
            jnp.tile(lam_re.reshape(ng, 1, p), (1, 1, 2)), jnp.tile(lam_im.reshape(ng, 1, p), (1, 1, 2)),
            jnp.tile(b_re.reshape(ng, p, h), (1, 1, CHUNK)), jnp.tile(b_im.reshape(ng, p, h), (1, 1, CHUNK)),
            c1, jnp.tile(c1, (1, CHUNK, 1)), jnp.tile(c2, (1, CHUNK, 1)),
            dpar.reshape(ng, h, 1))

    def spec(a):
        return pl.BlockSpec((1,) + a.shape[1:], lambda i: (i, 0, 0))

    return pl.pallas_call(
        _s5_param_kernel,
        grid=(ng,),
        in_specs=[spec(a) for a in args],
        out_specs=[pl.BlockSpec((1, cw, cw), lambda i: (i, 0, 0)),
                   pl.BlockSpec((1, 2 * p, cw), lambda i: (i, 0, 0)),
                   pl.BlockSpec((1, cw, 2 * p), lambda i: (i, 0, 0)),
                   pl.BlockSpec((1, 2 * p, 1), lambda i: (i, 0, 0))],
        out_shape=[jax.ShapeDtypeStruct((ng, cw, cw), BF16),
                   jax.ShapeDtypeStruct((ng, 2 * p, cw), BF16),
                   jax.ShapeDtypeStruct((ng, cw, 2 * p), BF16),
                   jax.ShapeDtypeStruct((ng, 2 * p, 1), F32)],
        compiler_params=_params("parallel"),
        name="s5_params",
    )(*args)


def _s5_mix_kernel(u_ref, m_ref, et_ref, ft_ref, ac_ref, y_ref, *, nchunk):
    p = S5_STATE
    u = u_ref[...].reshape(S5_CW, u_ref.shape[-1])
    r = u.shape[1]
    s = jnp.dot(et_ref[0], u, preferred_element_type=F32)
    s_re, s_im = s[:p], s[p:]
    a = ac_ref[0]
    a_re, a_im = a[:p], a[p:]
    n_idx = lax.broadcasted_iota(jnp.int32, (1, r), 1) % nchunk
    shift = 1
    while shift < nchunk:
        keep = n_idx >= shift
        t_re = jnp.where(keep, pltpu.roll(s_re, shift, axis=1), 0.0)
        t_im = jnp.where(keep, pltpu.roll(s_im, shift, axis=1), 0.0)
        s_re, s_im = (s_re + a_re * t_re - a_im * t_im, s_im + a_re * t_im + a_im * t_re)
        a_re, a_im = a_re * a_re - a_im * a_im, 2.0 * a_re * a_im
        shift *= 2
    keep = n_idx >= 1
    h_prev = jnp.concatenate([jnp.where(keep, pltpu.roll(s_re, 1, axis=1), 0.0),
                              jnp.where(keep, pltpu.roll(s_im, 1, axis=1), 0.0)], axis=0)
    y = jnp.dot(m_ref[0], u, preferred_element_type=F32)
    y = y + jnp.dot(ft_ref[0], h_prev.astype(BF16), preferred_element_type=F32)
    y_ref[...] = y.reshape(y_ref.shape).astype(y_ref.dtype)


def s5_mix(ut, m, et, ft, ac, layer, nchunk):
    _, s5w, r = ut.shape
    g, h, p, cw = S5_GROUPS, S5_GROUP_CH, S5_STATE, S5_CW
    base = layer * g
    return pl.pallas_call(
        functools.partial(_s5_mix_kernel, nchunk=nchunk),
        grid=(g,),
        in_specs=[pl.BlockSpec((CHUNK, h, r), lambda i: (0, i, 0)),
                  pl.BlockSpec((1, cw, cw), lambda i: (base + i, 0, 0)),
                  pl.BlockSpec((1, 2 * p, cw), lambda i: (base + i, 0, 0)),
                  pl.BlockSpec((1, cw, 2 * p), lambda i: (base + i, 0, 0)),
                  pl.BlockSpec((1, 2 * p, 1), lambda i: (base + i, 0, 0))],
        out_specs=pl.BlockSpec((CHUNK, h, r), lambda i: (0, i, 0)),
        out_shape=jax.ShapeDtypeStruct((CHUNK, s5w, r), BF16),
        compiler_params=_params("parallel"),
        name="s5_mix",
    )(ut, m, et, ft, ac)


def _gla_kernel(q_ref, k_ref, v_ref, og_ref, lr_ref, wlr_ref, blr_ref, ng_ref, o_ref, st_ref, *, nc):
    hd, dk, dv = GLA_HEADS, GLA_DK, GLA_DV

    @pl.when(pl.program_id(1) == 0)
    def _():
        st_ref[...] = jnp.zeros_like(st_ref)

    ri = lax.broadcasted_iota(jnp.int32, (CHUNK, CHUNK), 0)
    ci = lax.broadcasted_iota(jnp.int32, (CHUNK, CHUNK), 1)
    incl = ri >= ci
    ltri = incl.astype(F32)
    lane_k = lax.broadcasted_iota(jnp.int32, (1, hd * dk), 1)
    srow = lax.broadcasted_iota(jnp.int32, (hd * dv, hd * dk), 0)
    scol = lax.broadcasted_iota(jnp.int32, (hd * dv, hd * dk), 1)
    same_head = (srow // dv) == (scol // dk)
    wlr, blr, ng = wlr_ref[...], blr_ref[...], ng_ref[...]

    cs = range(nc)
    cr = lambda c: slice(c * CHUNK, (c + 1) * CHUNK)
    g = [_log_sigmoid(_mm_hi(lr_ref[cr(c), :], wlr) + blr) * (1.0 / GLA_GATE_NORM) for c in cs]
    bc = [_mm_hi(ltri, g[c]) for c in cs]
    bl = [bc[c][CHUNK - 1:CHUNK, :] for c in cs]
    q_e = [q_ref[cr(c), :] * (dk ** -0.5) * jnp.exp(bc[c]) for c in cs]
    k_e = [k_ref[cr(c), :] * jnp.exp(-bc[c]) for c in cs]
    k_d = [k_ref[cr(c), :] * jnp.exp(bl[c] - bc[c]) for c in cs]
    kv = [jnp.where(same_head, _mm_tn(v_ref[cr(c), :], k_d[c]), 0.0) for c in cs]
    sts = []
    st = st_ref[...]
    for c in cs:
        sts.append(st)
        st = jnp.exp(bl[c]) * st + kv[c]
    st_ref[...] = st
    o_inter = [_mm_nt(q_e[c], sts[c]) for c in cs]
    sc = [[jnp.where(incl, _mm_nt(jnp.where((lane_k // dk) == h, q_e[c], 0.0), k_e[c]), 0.0)
           for h in range(hd)] for c in cs]
    for c in cs:
        for h in range(hd):
            cols = slice(h * dv, (h + 1) * dv)
            oh = _mm(sc[c][h], v_ref[cr(c), cols]) + o_inter[c][:, cols]
            ms = jnp.mean(oh * oh, axis=-1, keepdims=True)
            o_ref[cr(c), cols] = (oh * lax.rsqrt(ms + EPS) * ng * _silu(og_ref[cr(c), cols])).astype(o_ref.dtype)


def gla_mix(zm, wlr_pad, blr, ng, bsz, seq, tc):
    t = zm.shape[0]
    per_b = seq // tc
    hd, dk, dv = GLA_HEADS, GLA_DK, GLA_DV

    def zspec(width, off):
        blk = off // width
        return pl.BlockSpec((tc, width), lambda b, i: (b * per_b + i, blk))

    def full(a):
        return pl.BlockSpec(a.shape, lambda b, i: (0,) * a.ndim)

    return pl.pallas_call(
        functools.partial(_gla_kernel, nc=tc // CHUNK),
        grid=(bsz, per_b),
        in_specs=[zspec(hd * dk, ZM_GQ), zspec(hd * dk, ZM_GK), zspec(hd * dv, ZM_GV),
                  zspec(hd * dv, ZM_GOG), zspec(LANE, ZM_GLR), full(wlr_pad), full(blr), full(ng)],
        out_specs=pl.BlockSpec((tc, hd * dv), lambda b, i: (b * per_b + i, 0)),
        out_shape=jax.ShapeDtypeStruct((t, hd * dv), F32),
        scratch_shapes=[pltpu.VMEM((hd * dv, hd * dk), F32)],
        compiler_params=_params("parallel", "arbitrary"),
        name="gla_mix",
    )(zm, zm, zm, zm, zm, wlr_pad, blr, ng)


def _gdn_kernel(qkv_ref, og_ref, ba_ref, cw_ref, alog_ref, dtb_ref, ng_ref, o_ref,
                xbuf_ref, s_ref, *, nc):
    hd, dk, dv, kc = GDN_HEADS, GDN_DK, GDN_DV, GDN_CONV
    tc = nc * CHUNK
    pad = 8

    @pl.when(pl.program_id(1) == 0)
    def _():
        s_ref[...] = jnp.zeros_like(s_ref)
        xbuf_ref[0:pad, :] = jnp.zeros((pad, xbuf_ref.shape[1]), F32)

    @pl.when(pl.program_id(1) != 0)
    def _():
        xbuf_ref[0:pad, :] = xbuf_ref[tc:tc + pad, :]

    xbuf_ref[pad:pad + tc, :] = qkv_ref[...]

    ri = lax.broadcasted_iota(jnp.int32, (CHUNK, CHUNK), 0)
    ci = lax.broadcasted_iota(jnp.int32, (CHUNK, CHUNK), 1)
    incl = ri >= ci
    strict = ri > ci
    ltri = incl.astype(F32)
    cw = cw_ref[...]
    ng = ng_ref[...]

    acc = cw[kc - 1:kc, :] * xbuf_ref[pad:pad + tc, :]
    for i in range(kc - 1):
        off = pad - (kc - 1) + i
        acc = acc + cw[i:i + 1, :] * xbuf_ref[off:off + tc, :]
    x = _silu(acc)
    ba = ba_ref[...]
    beta_all = _sigmoid(ba)
    g_all = -jnp.exp(alog_ref[...]) * _softplus(ba + dtb_ref[...])

    pairs = [(c, h) for c in range(nc) for h in range(hd)]
    cr = lambda c: slice(c * CHUNK, (c + 1) * CHUNK)
    gam_c = [_mm_hi(ltri, g_all[cr(c), :]) for c in range(nc)]
    gam_tc = [g.T for g in gam_c]

    def l2n(a):
        return a * lax.rsqrt(jnp.sum(a * a, axis=-1, keepdims=True) + EPS)

    q = [l2n(x[cr(c), h * dk:(h + 1) * dk]) * (dk ** -0.5) for c, h in pairs]
    k = [l2n(x[cr(c), (hd + h) * dk:(hd + h + 1) * dk]) for c, h in pairs]
    v = [x[cr(c), (2 * hd + h) * dk:(2 * hd + h + 1) * dk] for c, h in pairs]
    beta = [beta_all[cr(c), h:h + 1] for c, h in pairs]
    gam = [gam_c[c][:, hd + h:hd + h + 1] for c, h in pairs]
    gam_row = [gam_tc[c][hd + h:hd + h + 1, :] for c, h in pairs]
    n = len(pairs)
    dmask = [jnp.where(incl, jnp.exp(gam[i] - gam_row[i]), 0.0) for i in range(n)]
    k_beta = [k[i] * beta[i] for i in range(n)]
    egam = [jnp.exp(gam[i]) for i in range(n)]
    low = [jnp.where(strict, _mm_nt(k_beta[i], k[i]) * dmask[i], 0.0) for i in range(n)]
    attn = [_mm_nt(q[i], k[i]) * dmask[i] for i in range(n)]
    sol = [jnp.concatenate([v[i] * beta[i], k_beta[i] * egam[i]], axis=1) for i in range(n)]
    sol = [sol[i] - _mm(low[i], sol[i]) for i in range(n)]
    pw = low
    for _ in range(5):
        pw = [_mm(pw[i], pw[i]) for i in range(n)]
        sol = [sol[i] + _mm(pw[i], sol[i]) for i in range(n)]
    q_dec = [q[i] * egam[i] for i in range(n)]
    gam_last = [gam[i][CHUNK - 1:CHUNK, :] for i in range(n)]
    k_dec = [k[i] * jnp.exp(gam_last[i] - gam[i]) for i in range(n)]
    decay = [jnp.exp(gam_last[i]) for i in range(n)]

    st = [s_ref[h] for h in range(hd)]
    for c in range(nc):
        ids = [c * hd + h for h in range(hd)]
        v_new = [sol[i][:, :dv] - _mm(sol[i][:, dv:], st[h]) for h, i in enumerate(ids)]
        o = [_mm(q_dec[i], st[h]) + _mm(attn[i], v_new[h]) for h, i in enumerate(ids)]
        st = [decay[i] * st[h] + _mm_tn(k_dec[i], v_new[h]) for h, i in enumerate(ids)]
        for h in range(hd):
            ms = jnp.mean(o[h] * o[h], axis=-1, keepdims=True)
            cols = slice(h * dv, (h + 1) * dv)
            o_ref[cr(c), cols] = (o[h] * lax.rsqrt(ms + EPS) * ng * _silu(og_ref[cr(c), cols])).astype(o_ref.dtype)
    for h in range(hd):
        s_ref[h] = st[h]


def gdn_mix(zm, conv_w_pad, alog_row, dtb_row, ng, bsz, seq, tc):
    t = zm.shape[0]
    per_b = seq // tc
    hd, dk, dv = GDN_HEADS, GDN_DK, GDN_DV
    qkv_w = 3 * hd * dk

    def zspec(width, off):
        blk = off // width
        return pl.BlockSpec((tc, width), lambda b, i: (b * per_b + i, blk))

    def full(a):
        return pl.BlockSpec(a.shape, lambda b, i: (0,) * a.ndim)

    return pl.pallas_call(
        functools.partial(_gdn_kernel, nc=tc // CHUNK),
        grid=(bsz, per_b),
        in_specs=[zspec(qkv_w, ZM_DQKV), zspec(hd * dv, ZM_DOG), zspec(LANE, ZM_DBA),
                  full(conv_w_pad), full(alog_row), full(dtb_row), full(ng)],
        out_specs=pl.BlockSpec((tc, hd * dv), lambda b, i: (b * per_b + i, 0)),
        out_shape=jax.ShapeDtypeStruct((t, hd * dv), F32),
        scratch_shapes=[pltpu.VMEM((tc + 8, qkv_w), F32), pltpu.VMEM((hd, dk, dv), F32)],
        compiler_params=_params("parallel", "arbitrary"),
        name="gdn_mix",
    )(zm, zm, zm, conv_w_pad, alog_row, dtb_row, ng)


def _merge_kernel(x_ref, g_ref, sc_ref, sh_ref, ygla_ref, ygdn_ref, yst_ref, wglu_ref,
                  wz0_ref, wz1_ref, wz2_ref, wbg_ref, wbs_ref, wbd_ref, o_ref,
                  h_ref, agla_ref, as5_ref, agdn_ref, m_ref, *, jt, bw):
    @pl.when(pl.program_id(2) == 0)
    def _():
        sc = jnp.tile(sc_ref[...], (jt, 1))
        sh = jnp.tile(sh_ref[...], (jt, 1))
        h_ref[...] = _norm_mod(_tok_major(x_ref), g_ref[...], sc, sh).astype(BF16)
        agla_ref[...] = _tok_major(ygla_ref).astype(BF16)
        agdn_ref[...] = _tok_major(ygdn_ref).astype(BF16)
        for jj in range(jt):
            rows = slice(jj * LANE, (jj + 1) * LANE)
            ys = _gelu_tanh(yst_ref[jj].astype(F32))
            glu = _mm_tn(ys, wglu_ref[...])
            as5_ref[rows, :] = (glu[:, :bw] * _sigmoid(glu[:, bw:])).astype(BF16)

    def branch(rows, wz_ref, a_ref, wb_ref):
        gate = _sigmoid(jnp.dot(h_ref[rows, :], wz_ref[...], preferred_element_type=F32))
        return gate * jnp.dot(a_ref[rows, :], wb_ref[...], preferred_element_type=F32)

    for rb in range(jt * LANE // MERGE_ROWS):
        rows = slice(rb * MERGE_ROWS, (rb + 1) * MERGE_ROWS)
        m_ref[rows, :] = (branch(rows, wz0_ref, agla_ref, wbg_ref) + branch(rows, wz1_ref, as5_ref, wbs_ref)
                          + branch(rows, wz2_ref, agdn_ref, wbd_ref))
    o_ref[...] = pltpu.einshape("(jr)d->rjd", m_ref[...], j=jt)


def merge_gate(x3, g, sc_rows, sh_rows, ygla3, ygdn3, yst, wglu, wzg, wbg, wbs, wbd, jt, tn):
    r, _, d = x3.shape
    bw = ygla3.shape[2]
    ncol = d // tn
    rows = jt * LANE

    def full(a):
        return pl.BlockSpec(a.shape, lambda i, j, n: (0,) * a.ndim)

    def view(width):
        return pl.BlockSpec((LANE, jt, width), lambda i, j, n: (i, j, 0))

    def wz(b):
        return pl.BlockSpec((d, tn), lambda i, j, n: (0, b * ncol + n))

    wb = pl.BlockSpec((bw, tn), lambda i, j, n: (0, n))
    rowspec = pl.BlockSpec((LANE, d), lambda i, j, n: (i, 0))
    return pl.pallas_call(
        functools.partial(_merge_kernel, jt=jt, bw=bw),
        grid=(r // LANE, CHUNK // jt, ncol),
        in_specs=[view(d), full(g), rowspec, rowspec, view(bw), view(bw),
                  pl.BlockSpec((jt, yst.shape[1], LANE), lambda i, j, n: (j, 0, i)),
                  full(wglu), wz(0), wz(1), wz(2), wb, wb, wb],
        out_specs=pl.BlockSpec((LANE, jt, tn), lambda i, j, n: (i, j, n)),
        out_shape=jax.ShapeDtypeStruct((r, CHUNK, d), F32),
        scratch_shapes=[pltpu.VMEM((rows, d), BF16), pltpu.VMEM((rows, bw), BF16),
                        pltpu.VMEM((rows, bw), BF16), pltpu.VMEM((rows, bw), BF16),
                        pltpu.VMEM((rows, tn), F32)],
        compiler_params=_params("parallel", "parallel", "arbitrary"),
        name="merge_gate",
    )(x3, g, sc_rows, sh_rows, ygla3, ygdn3, yst, wglu, wzg, wzg, wzg, wbg, wbs, wbd)


def _ffn_kernel(x_ref, m_ref, wout_ref, gt1_ref, g_ref, sc_ref, sh_ref, gt2_ref, wa_ref, wb_ref, wo_ref,
                fg_ref, o_ref, x1_ref, h_ref, acc_ref, *, final_norm):
    j = pl.program_id(1)

    @pl.when(j == 0)
    def _():
        x1 = x_ref[...] + gt1_ref[...] * _mm(m_ref[...], wout_ref[...])
        x1_ref[...] = x1
        h_ref[...] = _norm_mod(x1, g_ref[...], sc_ref[...], sh_ref[...]).astype(BF16)
        acc_ref[...] = jnp.zeros_like(acc_ref)

    h = h_ref[...]
    a = jnp.dot(h, wa_ref[...], preferred_element_type=F32)
    b = jnp.dot(h, wb_ref[...], preferred_element_type=F32)
    acc_ref[...] += _mm(_silu(a) * b, wo_ref[...])

    @pl.when(j == pl.num_programs(1) - 1)
    def _():
        y = x1_ref[...] + gt2_ref[...] * acc_ref[...]
        if final_norm:
            ms = jnp.mean(y * y, axis=-1, keepdims=True)
            y = y * lax.rsqrt(ms + EPS) * fg_ref[...]
        o_ref[...] = y


def ffn_block(x2, merged, w_o, gt1, g, sc, sh, gt2, w_in, w_out, final_g, seq, tm, tf, final_norm):
    t, d = x2.shape
    dff = w_out.shape[0]
    nf = dff // tf
    per_b = seq // tm

    def bspec():
        return pl.BlockSpec((None, 1, d), lambda i, j: (i // per_b, 0, 0))

    def const(a):
        return pl.BlockSpec(a.shape, lambda i, j: (0,) * a.ndim)

    row = pl.BlockSpec((tm, d), lambda i, j: (i, 0))
    return pl.pallas_call(
        functools.partial(_ffn_kernel, final_norm=final_norm),
        grid=(t // tm, nf),
        in_specs=[row, row, const(w_o), bspec(), const(g), bspec(), bspec(), bspec(),
                  pl.BlockSpec((d, tf), lambda i, j: (0, j)),
                  pl.BlockSpec((d, tf), lambda i, j: (0, nf + j)),
                  pl.BlockSpec((tf, d), lambda i, j: (j, 0)),
                  const(final_g)],
        out_specs=row,
        out_shape=jax.ShapeDtypeStruct((t, d), F32),
        scratch_shapes=[pltpu.VMEM((tm, d), F32), pltpu.VMEM((tm, d), BF16), pltpu.VMEM((tm, d), F32)],
        compiler_params=_params("parallel", "arbitrary"),
        name="ffn_block",
    )(x2, merged, w_o, gt1, g, sc, sh, gt2, w_in, w_in, w_out, final_g)


def _forward(x, c, w_ada, b_ada, norm1_g, w_in, gla_w_lr, gla_b_lr, gla_norm_g,
             s5_lambda_re, s5_lambda_im, s5_log_dt, s5_b_re, s5_b_im, s5_c_re, s5_c_im,
             s5_d, s5_w_glu, gdn_conv_w, gdn_a_log, gdn_dt_bias, gdn_norm_g,
             w_branch_gla, w_branch_s5, w_branch_gdn, w_out, norm2_g, w_ffn_in, w_ffn_out,
             final_g):
    bsz, seq, d = x.shape
    depth = w_ada.shape[0]
    t = bsz * seq
    nchunk = seq // CHUNK
    r = bsz * nchunk
    gqk, gw = GLA_HEADS * GLA_DK, GLA_HEADS * GLA_DV
    s5w = S5_GROUPS * S5_GROUP_CH
    dqkv, dw = 3 * GDN_HEADS * GDN_DK, GDN_HEADS * GDN_DV

    mod = ada_modulation(c, w_ada, b_ada).reshape(depth, bsz, 6, 1, d)
    s5m, s5et, s5ft, s5ac = s5_params(s5_lambda_re, s5_lambda_im, s5_log_dt, s5_b_re, s5_b_im,
                                      s5_c_re, s5_c_im, s5_d)

    o_gq, o_gk, o_gv = 0, gqk, 2 * gqk
    o_glr = o_gv + gw
    o_gog = o_glr + GLA_LOWRANK
    o_s5 = o_gog + gw
    o_dqkv = o_s5 + s5w
    o_dbeta = o_dqkv + dqkv
    o_da = o_dbeta + GDN_HEADS
    o_dog = o_da + GDN_HEADS
    o_zg = o_dog + dw

    x2 = x.reshape(t, d)
    for i in range(depth):
        wi = w_in[i]
        zpad = jnp.zeros((d, LANE - GLA_LOWRANK), F32)
        zpad2 = jnp.zeros((d, LANE - 2 * GDN_HEADS), F32)
        w_zm = jnp.concatenate([
            wi[:, o_gq:o_gq + gqk], wi[:, o_gk:o_gk + gqk], wi[:, o_gv:o_gv + gw],
            wi[:, o_gog:o_gog + gw], wi[:, o_dqkv:o_dqkv + dqkv], wi[:, o_dog:o_dog + dw],
            wi[:, o_glr:o_glr + GLA_LOWRANK], zpad,
            wi[:, o_dbeta:o_dbeta + 2 * GDN_HEADS], zpad2], axis=1).astype(BF16)
        w_zg = wi[:, o_zg:].astype(BF16)
        w_s5t = wi[:, o_s5:o_s5 + s5w].T.astype(BF16)

        sh1, sc1, gt1, sh2, sc2, gt2 = [mod[i, :, k] for k in range(6)]
        g1 = norm1_g[i].reshape(1, d)
        rows = lambda a: jnp.broadcast_to(a, (bsz, nchunk, d)).reshape(r, d)

        zm = norm_mod_matmul(x2, g1, sc1, sh1, w_zm, seq, tm=1024, tn=1280)
        x3 = x2.reshape(r, CHUNK, d)
        sc_rows, sh_rows = rows(sc1), rows(sh1)
        ut = s5_inproj(x3, g1, sc_rows, sh_rows, w_s5t, jt=8)

        wlr_pad = jnp.pad(gla_w_lr[i], ((0, LANE - GLA_LOWRANK), (0, 0)))
        y_gla = gla_mix(zm, wlr_pad, gla_b_lr[i].reshape(1, gqk), gla_norm_g[i].reshape(1, GLA_DV),
                        bsz, seq, tc=256)
        yst = s5_mix(ut, s5m, s5et, s5ft, s5ac, i, nchunk)
        conv_pad = jnp.pad(gdn_conv_w[i], ((0, 8 - GDN_CONV), (0, 0)))
        alog_row = jnp.pad(gdn_a_log[i], (GDN_HEADS, LANE - 2 * GDN_HEADS)).reshape(1, LANE)
        dtb_row = jnp.pad(gdn_dt_bias[i], (GDN_HEADS, LANE - 2 * GDN_HEADS)).reshape(1, LANE)
        y_gdn = gdn_mix(zm, conv_pad, alog_row, dtb_row, gdn_norm_g[i].reshape(1, GDN_DV),
                        bsz, seq, tc=256)

        merged = merge_gate(x3, g1, sc_rows, sh_rows, y_gla.reshape(r, CHUNK, gw), y_gdn.reshape(r, CHUNK, dw),
                            yst, s5_w_glu[i].astype(BF16), w_zg, w_branch_gla[i].astype(BF16),
                            w_branch_s5[i].astype(BF16), w_branch_gdn[i].astype(BF16), jt=8, tn=256)
        x2 = ffn_block(x2, merged.reshape(t, d), w_out[i].astype(BF16), gt1,
                       norm2_g[i].reshape(1, d), sc2, sh2, gt2,
                       w_ffn_in[i].astype(BF16), w_ffn_out[i].astype(BF16),
                       final_g.reshape(1, d), seq, tm=1024, tf=256, final_norm=(i == depth - 1))
    return x2.reshape(bsz, seq, d)


def kernel(x, c, w_ada, b_ada, norm1_g, w_in, gla_w_lr, gla_b_lr, gla_norm_g, s5_lambda_re, s5_lambda_im, s5_log_dt, s5_b_re, s5_b_im, s5_c_re, s5_c_im, s5_d, s5_w_glu, gdn_conv_w, gdn_a_log, gdn_dt_bias, gdn_norm_g, w_branch_gla, w_branch_s5, w_branch_gdn, w_out, norm2_g, w_ffn_in, w_ffn_out, final_g):
    return _forward(x, c, w_ada, b_ada, norm1_g, w_in, gla_w_lr, gla_b_lr, gla_norm_g,
                    s5_lambda_re, s5_lambda_im, s5_log_dt, s5_b_re, s5_b_im, s5_c_re, s5_c_im,
                    s5_d, s5_w_glu, gdn_conv_w, gdn_a_log, gdn_dt_bias, gdn_norm_g,
                    w_branch_gla, w_branch_s5, w_branch_gdn, w_out, norm2_g, w_ffn_in, w_ffn_out,
                    final_g)
```

```python
import functools
import math

import jax
import jax.numpy as jnp
from jax import lax
from jax.experimental import pallas as pl
from jax.experimental.pallas import tpu as pltpu

F32 = jnp.float32
BF16 = jnp.bfloat16
HI = lax.Precision.HIGHEST

EPS = 1e-6
CHUNK = 64
LANE = 128
VMEM_LIMIT = 56 * 1024 * 1024

GLA_HEADS, GLA_DK, GLA_DV, GLA_LOWRANK = 4, 64, 128, 16
GLA_GATE_NORM = 16.0
S5_GROUPS, S5_GROUP_CH, S5_STATE = 32, 16, 64
GDN_HEADS, GDN_DK, GDN_DV, GDN_CONV = 4, 128, 128, 4
N_BRANCH = 3
MERGE_ROWS = 512
GDN_GROUP_CHUNKS = 4

ZM_GQ, ZM_GK, ZM_GV, ZM_GOG = 0, 256, 512, 1024
ZM_DQKV, ZM_DOG, ZM_GLR, ZM_DBA = 1536, 3072, 3584, 3712
ZM_WIDTH = 3840


def _mm(a, b):
    return jnp.dot(a.astype(BF16), b.astype(BF16), preferred_element_type=F32)


def _mm_nt(a, b):
    return lax.dot_general(a.astype(BF16), b.astype(BF16), (((1,), (1,)), ((), ())),
                           preferred_element_type=F32)


def _mm_tn(a, b):
    return lax.dot_general(a.astype(BF16), b.astype(BF16), (((0,), (0,)), ((), ())),
                           preferred_element_type=F32)


def _mm_hi(a, b):
    return jnp.dot(a, b, precision=HI, preferred_element_type=F32)


def _split3(x):
    hi = x.astype(BF16)
    r1 = x - hi.astype(F32)
    mid = r1.astype(BF16)
    return hi, mid, (r1 - mid.astype(F32)).astype(BF16)


def _select_rows(m01, x):
    return sum(jnp.dot(m01, t, preferred_element_type=F32) for t in _split3(x))


def _select_cols(x, m01):
    return sum(jnp.dot(t, m01, preferred_element_type=F32) for t in _split3(x))


def _sigmoid(x):
    return 0.5 + 0.5 * jnp.tanh(0.5 * x)


def _silu(x):
    h = 0.5 * x
    return h + h * jnp.tanh(h)


def _softplus(x):
    return jnp.maximum(x, 0.0) + jnp.log(1.0 + jnp.exp(-jnp.abs(x)))


def _log_sigmoid(x):
    return -_softplus(-x)


def _gelu_tanh(x):
    c = math.sqrt(2.0 / math.pi)
    return 0.5 * x * (1.0 + jnp.tanh(c * (x + 0.044715 * (x * x * x))))


def _norm_mod(x, g, sc, sh):
    ms = jnp.mean(x * x, axis=-1, keepdims=True)
    return (x * lax.rsqrt(ms + EPS) * g) * (1.0 + sc) + sh


def _tok_major(ref):
    return pltpu.einshape("rjd->(jr)d", ref[...])


def _params(*sem):
    return pltpu.CompilerParams(dimension_semantics=sem, vmem_limit_bytes=VMEM_LIMIT)


def _ada_kernel(c_ref, w_ref, b_ref, o_ref):
    c = c_ref[...]
    o_ref[0] = _mm_hi(_silu(c), w_ref[0]) + b_ref[0]


def ada_modulation(c, w_ada, b_ada):
    depth, d, d6 = w_ada.shape
    bsz = c.shape[0]
    rows = -(-bsz // 8) * 8
    c_pad = jnp.pad(c, ((0, rows - bsz), (0, 0)))
    out = pl.pallas_call(
        _ada_kernel,
        grid=(depth, d6 // d),
        in_specs=[pl.BlockSpec((rows, d), lambda i, j: (0, 0)),
                  pl.BlockSpec((1, d, d), lambda i, j: (i, 0, j)),
                  pl.BlockSpec((1, 1, d), lambda i, j: (i, 0, j))],
        out_specs=pl.BlockSpec((1, rows, d), lambda i, j: (i, 0, j)),
        out_shape=jax.ShapeDtypeStruct((depth, rows, d6), F32),
        compiler_params=_params("parallel", "parallel"),
        name="ada_modulation",
    )(c_pad, w_ada, b_ada.reshape(depth, 1, d6))
    return out[:, :bsz]


def _inproj_kernel(x_ref, g_ref, sc_ref, sh_ref, w_ref, o_ref, h_ref):
    @pl.when(pl.program_id(1) == 0)
    def _():
        h_ref[...] = _norm_mod(x_ref[...], g_ref[...], sc_ref[...], sh_ref[...]).astype(BF16)

    o_ref[...] = jnp.dot(h_ref[...], w_ref[...], preferred_element_type=F32).astype(o_ref.dtype)


def norm_mod_matmul(x2, g, sc, sh, w, seq, tm, tn):
    t, d = x2.shape
    c = w.shape[1]
    per_b = seq // tm
    return pl.pallas_call(
        _inproj_kernel,
        grid=(t // tm, c // tn),
        in_specs=[pl.BlockSpec((tm, d), lambda i, j: (i, 0)),
                  pl.BlockSpec((1, d), lambda i, j: (0, 0)),
                  pl.BlockSpec((None, 1, d), lambda i, j: (i // per_b, 0, 0)),
                  pl.BlockSpec((None, 1, d), lambda i, j: (i // per_b, 0, 0)),
                  pl.BlockSpec((d, tn), lambda i, j: (0, j))],
        out_specs=pl.BlockSpec((tm, tn), lambda i, j: (i, j)),
        out_shape=jax.ShapeDtypeStruct((t, c), BF16),
        scratch_shapes=[pltpu.VMEM((tm, d), BF16)],
        compiler_params=_params("parallel", "arbitrary"),
        name="norm_mod_matmul",
    )(x2, g, sc, sh, w)


def _s5_inproj_kernel(x_ref, g_ref, sc_ref, sh_ref, wt_ref, o_ref, *, jt):
    sc = jnp.tile(sc_ref[...], (jt, 1))
    sh = jnp.tile(sh_ref[...], (jt, 1))
    h = _norm_mod(_tok_major(x_ref), g_ref[...], sc, sh).astype(BF16)
    ut = lax.dot_general(wt_ref[...], h, (((1,), (1,)), ((), ())), preferred_element_type=F32)
    for jj in range(jt):
        o_ref[jj] = ut[:, jj * LANE:(jj + 1) * LANE].astype(BF16)


def s5_inproj(x3, g, sc_rows, sh_rows, wt, jt):
    r, _, d = x3.shape
    s5w = wt.shape[0]
    return pl.pallas_call(
        functools.partial(_s5_inproj_kernel, jt=jt),
        grid=(r // LANE, CHUNK // jt),
        in_specs=[pl.BlockSpec((LANE, jt, d), lambda i, j: (i, j, 0)),
                  pl.BlockSpec((1, d), lambda i, j: (0, 0)),
                  pl.BlockSpec((LANE, d), lambda i, j: (i, 0)),
                  pl.BlockSpec((LANE, d), lambda i, j: (i, 0)),
                  pl.BlockSpec((s5w, d), lambda i, j: (0, 0))],
        out_specs=pl.BlockSpec((jt, s5w, LANE), lambda i, j: (j, 0, i)),
        out_shape=jax.ShapeDtypeStruct((CHUNK, s5w, r), BF16),
        compiler_params=_params("parallel", "parallel"),
        name="s5_inproj",
    )(x3, g, sc_rows, sh_rows, wt)


S5_CW = CHUNK * S5_GROUP_CH


def _cpow(lr_dt, li_dt, e):
    mag = jnp.exp(lr_dt * e)
    ang = li_dt * e
    return mag * jnp.cos(ang), mag * jnp.sin(ang)


def _s5_param_kernel(ldt_ref, lrc_ref, lic_ref, lrr_ref, lir_ref, bre_ref, bim_ref,
                     c1_ref, c1t_ref, c2t_ref, dcol_ref, m_ref, et_ref, ft_ref, ac_ref):
    p, cw, h = S5_STATE, S5_CW, S5_GROUP_CH
    dt = jnp.exp(ldt_ref[0])
    lrc, lic = lrc_ref[0], lic_ref[0]
    ab_re, ab_im = _cpow(lrc * dt, lic * dt, 1.0)
    den = lrc * lrc + lic * lic
    nr, ni = ab_re - 1.0, ab_im
    w_re = (nr * lrc + ni * lic) / den
    w_im = (ni * lrc - nr * lic) / den
    bre, bim = bre_ref[0], bim_ref[0]
    bb_re = w_re * bre - w_im * bim
    bb_im = w_re * bim + w_im * bre
    lane = lax.broadcasted_iota(jnp.int32, (1, LANE), 1)
    e_m = jnp.where(lane < CHUNK, CHUNK - 1 - lane, 0).astype(F32)
    pd_re, pd_im = _cpow(lrc * dt, lic * dt, e_m)
    xrow = lax.broadcasted_iota(jnp.int32, (LANE, cw), 0)
    xcol = lax.broadcasted_iota(jnp.int32, (LANE, cw), 1)
    expand_l = (xrow == xcol // h).astype(BF16)
    p_re, p_im = _select_cols(pd_re, expand_l), _select_cols(pd_im, expand_l)
    e_re = p_re * bb_re - p_im * bb_im
    e_im = p_re * bb_im + p_im * bb_re
    et = jnp.concatenate([e_re, e_im], axis=0)
    et_ref[0] = et.astype(BF16)
    a_re, a_im = _cpow(lrc * dt, lic * dt, float(CHUNK))
    ac_ref[0] = jnp.concatenate([a_re, a_im], axis=0)
    sgn = jnp.where(lax.broadcasted_iota(jnp.int32, (1, 2 * p), 1) < p, 1.0, -1.0)
    krev = _mm_hi(c1_ref[0] * sgn, et)
    row = lax.broadcasted_iota(jnp.int32, (h, cw), 0)
    col = lax.broadcasted_iota(jnp.int32, (h, cw), 1)
    krev = krev + jnp.where(col == (cw - h) + row, dcol_ref[0], 0.0)
    rrev = jnp.concatenate([krev, jnp.zeros_like(krev)], axis=1)
    per_tile = LANE // h
    rolled = [rrev if r == 0 else pltpu.roll(rrev, 2 * cw - r * h, axis=1) for r in range(per_tile)]
    for i in range(CHUNK):
        s = (CHUNK - 1 - i) * h
        a, r = s // LANE, (s % LANE) // h
        m_ref[0, i * h:(i + 1) * h, :] = rolled[r][:, a * LANE:a * LANE + cw].astype(BF16)
    f_i = (lax.broadcasted_iota(jnp.int32, (CHUNK, 1), 0) + 1).astype(F32)
    qd_re, qd_im = _cpow(lrr_ref[0] * dt, lir_ref[0] * dt, f_i)
    yrow = lax.broadcasted_iota(jnp.int32, (cw, CHUNK), 0)
    ycol = lax.broadcasted_iota(jnp.int32, (cw, CHUNK), 1)
    expand_r = (yrow // h == ycol).astype(BF16)
    q_re, q_im = _select_rows(expand_r, qd_re), _select_rows(expand_r, qd_im)
    ft_ref[0] = (c1t_ref[0] * sgn * q_re - c2t_ref[0] * q_im).astype(BF16)


def s5_params(lam_re, lam_im, log_dt, b_re, b_im, c_re, c_im, dpar):
    ng = lam_re.shape[0] * lam_re.shape[1]
    p, h, cw = S5_STATE, S5_GROUP_CH, S5_CW
    lam_re = lam_re.reshape(ng, p)
    lam_im = lam_im.reshape(ng, p)
    c_re = c_re.reshape(ng, h, p)
    c_im = c_im.reshape(ng, h, p)
    c1 = jnp.concatenate([c_re, c_im], axis=-1)
    c2 = jnp.concatenate([c_im, c_re], axis=-1)
    args = (log_dt.reshape(ng, 1, 1),
            lam_re.reshape(ng, p, 1), lam_im.reshape(ng, p, 1),
            jnp.tile(lam_re.reshape(ng, 1, p), (1, 1, 2)), jnp.tile(lam_im.reshape(ng, 1, p), (1, 1, 2)),
            jnp.tile(b_re.reshape(ng, p, h), (1, 1, CHUNK)), jnp.tile(b_im.reshape(ng, p, h), (1, 1, CHUNK)),
            c1, jnp.tile(c1, (1, CHUNK, 1)), jnp.tile(c2, (1, CHUNK, 1)),
            dpar.reshape(ng, h, 1))

    def spec(a):
        return pl.BlockSpec((1,) + a.shape[1:], lambda i: (i, 0, 0))

    return pl.pallas_call(
        _s5_param_kernel,
        grid=(ng,),
        in_specs=[spec(a) for a in args],
        out_specs=[pl.BlockSpec((1, cw, cw), lambda i: (i, 0, 0)),
                   pl.BlockSpec((1, 2 * p, cw), lambda i: (i, 0, 0)),
                   pl.BlockSpec((1, cw, 2 * p), lambda i: (i, 0, 0)),
                   pl.BlockSpec((1, 2 * p, 1), lambda i: (i, 0, 0))],
        out_shape=[jax.ShapeDtypeStruct((ng, cw, cw), BF16),
                   jax.ShapeDtypeStruct((ng, 2 * p, cw), BF16),
                   jax.ShapeDtypeStruct((ng, cw, 2 * p), BF16),
                   jax.ShapeDtypeStruct((ng, 2 * p, 1), F32)],
        compiler_params=_params("parallel"),
        name="s5_params",
    )(*args)


def _s5_mix_kernel(u_ref, m_ref, et_ref, ft_ref, ac_ref, y_ref, *, nchunk):
    p = S5_STATE
    u = u_ref[...].reshape(S5_CW, u_ref.shape[-1])
    r = u.shape[1]
    s = jnp.dot(et_ref[0], u, preferred_element_type=F32)
    s_re, s_im = s[:p], s[p:]
    a = ac_ref[0]
    a_re, a_im = a[:p], a[p:]
    n_idx = lax.broadcasted_iota(jnp.int32, (1, r), 1) % nchunk
    shift = 1
    while shift < nchunk:
        keep = n_idx >= shift
        t_re = jnp.where(keep, pltpu.roll(s_re, shift, axis=1), 0.0)
        t_im = jnp.where(keep, pltpu.roll(s_im, shift, axis=1), 0.0)
        s_re, s_im = (s_re + a_re * t_re - a_im * t_im, s_im + a_re * t_im + a_im * t_re)
        a_re, a_im = a_re * a_re - a_im * a_im, 2.0 * a_re * a_im
        shift *= 2
    keep = n_idx >= 1
    h_prev = jnp.concatenate([jnp.where(keep, pltpu.roll(s_re, 1, axis=1), 0.0),
                              jnp.where(keep, pltpu.roll(s_im, 1, axis=1), 0.0)], axis=0)
    y = jnp.dot(m_ref[0], u, preferred_element_type=F32)
    y = y + jnp.dot(ft_ref[0], h_prev.astype(BF16), preferred_element_type=F32)
    y_ref[...] = y.reshape(y_ref.shape).astype(y_ref.dtype)


def s5_mix(ut, m, et, ft, ac, layer, nchunk):
    _, s5w, r = ut.shape
    g, h, p, cw = S5_GROUPS, S5_GROUP_CH, S5_STATE, S5_CW
    base = layer * g
    return pl.pallas_call(
        functools.partial(_s5_mix_kernel, nchunk=nchunk),
        grid=(g,),
        in_specs=[pl.BlockSpec((CHUNK, h, r), lambda i: (0, i, 0)),
                  pl.BlockSpec((1, cw, cw), lambda i: (base + i, 0, 0)),
                  pl.BlockSpec((1, 2 * p, cw), lambda i: (base + i, 0, 0)),
                  pl.BlockSpec((1, cw, 2 * p), lambda i: (base + i, 0, 0)),
                  pl.BlockSpec((1, 2 * p, 1), lambda i: (base + i, 0, 0))],
        out_specs=pl.BlockSpec((CHUNK, h, r), lambda i: (0, i, 0)),
        out_shape=jax.ShapeDtypeStruct((CHUNK, s5w, r), BF16),
        compiler_params=_params("parallel"),
        name="s5_mix",
    )(ut, m, et, ft, ac)


def _gla_kernel(q_ref, k_ref, v_ref, og_ref, lr_ref, wlr_ref, blr_ref, ng_ref, o_ref, st_ref, *, nc):
    hd, dk, dv = GLA_HEADS, GLA_DK, GLA_DV

    @pl.when(pl.program_id(1) == 0)
    def _():
        st_ref[...] = jnp.zeros_like(st_ref)

    ri = lax.broadcasted_iota(jnp.int32, (CHUNK, CHUNK), 0)
    ci = lax.broadcasted_iota(jnp.int32, (CHUNK, CHUNK), 1)
    incl = ri >= ci
    ltri = incl.astype(BF16)
    lane_k = lax.broadcasted_iota(jnp.int32, (1, hd * dk), 1)
    srow = lax.broadcasted_iota(jnp.int32, (hd * dv, hd * dk), 0)
    scol = lax.broadcasted_iota(jnp.int32, (hd * dv, hd * dk), 1)
    same_head = (srow // dv) == (scol // dk)
    wlr, blr, ng = wlr_ref[...], blr_ref[...], ng_ref[...]

    cs = range(nc)
    cr = lambda c: slice(c * CHUNK, (c + 1) * CHUNK)
    g = [_log_sigmoid(_mm(lr_ref[cr(c), :], wlr) + blr) * (1.0 / GLA_GATE_NORM) for c in cs]
    bc = [_select_rows(ltri, g[c]) for c in cs]
    bl = [bc[c][CHUNK - 1:CHUNK, :] for c in cs]
    q_e = [q_ref[cr(c), :].astype(F32) * (dk ** -0.5) * jnp.exp(bc[c]) for c in cs]
    k_e = [k_ref[cr(c), :].astype(F32) * jnp.exp(-bc[c]) for c in cs]
    k_d = [k_ref[cr(c), :].astype(F32) * jnp.exp(bl[c] - bc[c]) for c in cs]
    kv = [jnp.where(same_head, _mm_tn(v_ref[cr(c), :], k_d[c]), 0.0) for c in cs]
    sts = []
    st = st_ref[...]
    for c in cs:
        sts.append(st)
        st = jnp.exp(bl[c]) * st + kv[c]
    st_ref[...] = st
    o_inter = [_mm_nt(q_e[c], sts[c]) for c in cs]
    q_heads = [jnp.concatenate([jnp.where((lane_k // dk) == h, q_e[c], 0.0) for h in range(hd)], axis=0)
               for c in cs]
    sc_all = [_mm_nt(q_heads[c], k_e[c]) for c in cs]
    sc = [[jnp.where(incl, sc_all[c][h * CHUNK:(h + 1) * CHUNK], 0.0) for h in range(hd)] for c in cs]
    for c in cs:
        for h in range(hd):
            cols = slice(h * dv, (h + 1) * dv)
            oh = _mm(sc[c][h], v_ref[cr(c), cols]) + o_inter[c][:, cols]
            ms = jnp.mean(oh * oh, axis=-1, keepdims=True)
            o_ref[cr(c), cols] = (oh * lax.rsqrt(ms + EPS) * ng * _silu(og_ref[cr(c), cols].astype(F32))).astype(o_ref.dtype)


def gla_mix(zm, wlr_pad, blr, ng, bsz, seq, tc):
    t = zm.shape[0]
    per_b = seq // tc
    hd, dk, dv = GLA_HEADS, GLA_DK, GLA_DV

    def zspec(width, off):
        blk = off // width
        return pl.BlockSpec((tc, width), lambda b, i: (b * per_b + i, blk))

    def full(a):
        return pl.BlockSpec(a.shape, lambda b, i: (0,) * a.ndim)

    return pl.pallas_call(
        functools.partial(_gla_kernel, nc=tc // CHUNK),
        grid=(bsz, per_b),
        in_specs=[zspec(hd * dk, ZM_GQ), zspec(hd * dk, ZM_GK), zspec(hd * dv, ZM_GV),
                  zspec(hd * dv, ZM_GOG), zspec(LANE, ZM_GLR), full(wlr_pad), full(blr), full(ng)],
        out_specs=pl.BlockSpec((tc, hd * dv), lambda b, i: (b * per_b + i, 0)),
        out_shape=jax.ShapeDtypeStruct((t, hd * dv), F32),
        scratch_shapes=[pltpu.VMEM((hd * dv, hd * dk), F32)],
        compiler_params=_params("parallel", "arbitrary"),
        name="gla_mix",
    )(zm, zm, zm, zm, zm, wlr_pad, blr, ng)


def _gdn_kernel(qkv_ref, og_ref, ba_ref, cw_ref, alog_ref, dtb_ref, ng_ref, o_ref,
                xbuf_ref, s_ref, *, nc):
    hd, dk, dv, kc = GDN_HEADS, GDN_DK, GDN_DV, GDN_CONV
    tc = nc * CHUNK
    pad = 8

    @pl.when(pl.program_id(1) == 0)
    def _():
        s_ref[...] = jnp.zeros_like(s_ref)
        xbuf_ref[0:pad, :] = jnp.zeros((pad, xbuf_ref.shape[1]), F32)

    @pl.when(pl.program_id(1) != 0)
    def _():
        xbuf_ref[0:pad, :] = xbuf_ref[tc:tc + pad, :]

    xbuf_ref[pad:pad + tc, :] = qkv_ref[...].astype(F32)

    ri = lax.broadcasted_iota(jnp.int32, (CHUNK, CHUNK), 0)
    ci = lax.broadcasted_iota(jnp.int32, (CHUNK, CHUNK), 1)
    incl = ri >= ci
    strict = ri > ci
    ltri = incl.astype(BF16)
    eye = (ri == ci).astype(F32)
    cw = cw_ref[...]
    ng = ng_ref[...]

    ba = ba_ref[...].astype(F32)
    beta_all = _sigmoid(ba)
    g_all = -jnp.exp(alog_ref[...]) * _softplus(ba + dtb_ref[...])

    cr = lambda c: slice(c * CHUNK, (c + 1) * CHUNK)
    gam_c = [_select_rows(ltri, g_all[cr(c), :]) for c in range(nc)]
    gam_tc = [g.T for g in gam_c]
    groups = [list(range(g0, min(g0 + GDN_GROUP_CHUNKS, nc))) for g0 in range(0, nc, GDN_GROUP_CHUNKS)]

    def conv_piece(chunks, col):
        lanes = slice(col * dk, (col + 1) * dk)
        r0, nr = pad + chunks[0] * CHUNK, len(chunks) * CHUNK
        acc = cw[kc - 1:kc, lanes] * xbuf_ref[r0:r0 + nr, lanes]
        for i in range(kc - 1):
            off = r0 - (kc - 1) + i
            acc = acc + cw[i:i + 1, lanes] * xbuf_ref[off:off + nr, lanes]
        y = _silu(acc)
        if col < 2 * hd:
            y = y * lax.rsqrt(jnp.sum(y * y, axis=-1, keepdims=True) + EPS)
        if col < hd:
            y = y * (dk ** -0.5)
        return [y[i * CHUNK:(i + 1) * CHUNK] for i in range(len(chunks))]

    def chain_stages(chunks, cols, res):
        pairs = [(ci, c, h) for ci, c in enumerate(chunks) for h in range(hd)]
        n = len(pairs)
        q = [cols[h][ci] for ci, c, h in pairs]
        k = [cols[hd + h][ci] for ci, c, h in pairs]
        v = [cols[2 * hd + h][ci] for ci, c, h in pairs]
        beta = [beta_all[cr(c), h:h + 1] for ci, c, h in pairs]
        gam = [gam_c[c][:, hd + h:hd + h + 1] for ci, c, h in pairs]
        gam_row = [gam_tc[c][hd + h:hd + h + 1, :] for ci, c, h in pairs]
        dmask = [jnp.where(incl, jnp.exp(gam[i] - gam_row[i]), 0.0) for i in range(n)]
        k_beta = [k[i] * beta[i] for i in range(n)]
        egam = [jnp.exp(gam[i]) for i in range(n)]
        kk = [_mm_nt(jnp.concatenate([k_beta[i], q[i]], axis=0), k[i]) for i in range(n)]
        low = [jnp.where(strict, kk[i][:CHUNK] * dmask[i], 0.0) for i in range(n)]
        res["attn"] = [kk[i][CHUNK:] * dmask[i] for i in range(n)]
        rhs = [jnp.concatenate([v[i] * beta[i], k_beta[i] * egam[i]], axis=1).astype(BF16) for i in range(n)]
        yield
        lowb = [low[i].astype(BF16) for i in range(n)]
        pw = [_mm(lowb[i], lowb[i]).astype(BF16) for i in range(n)]
        tinv = [eye - low[i] for i in range(n)]
        for s in range(5):
            yield
            if s < 4:
                stk = [_mm(jnp.concatenate([tinv[i].astype(BF16), pw[i]], axis=0), pw[i]) for i in range(n)]
                tinv = [tinv[i] + stk[i][:CHUNK] for i in range(n)]
                pw = [stk[i][CHUNK:].astype(BF16) for i in range(n)]
            else:
                tinv = [tinv[i] + _mm(tinv[i], pw[i]) for i in range(n)]
        yield
        res["sol"] = [_mm(tinv[i], rhs[i]) for i in range(n)]
        yield
        res["q_dec"] = [q[i] * egam[i] for i in range(n)]
        gam_last = [gam[i][CHUNK - 1:CHUNK, :] for i in range(n)]
        res["k_dec"] = [k[i] * jnp.exp(gam_last[i] - gam[i]) for i in range(n)]
        res["decay"] = [jnp.exp(gam_last[i]) for i in range(n)]

    ncols = 3 * hd
    cols = [conv_piece(groups[0], col) for col in range(ncols)]
    st = [s_ref[h] for h in range(hd)]
    for gi, chunks in enumerate(groups):
        res = {}
        nxt = groups[gi + 1] if gi + 1 < len(groups) else None
        nxt_cols = []
        for _ in chain_stages(chunks, cols, res):
            if nxt is not None and len(nxt_cols) < ncols:
                nxt_cols.append(conv_piece(nxt, len(nxt_cols)))
        while nxt is not None and len(nxt_cols) < ncols:
            nxt_cols.append(conv_piece(nxt, len(nxt_cols)))
        cols = nxt_cols
        sol, attn, q_dec, k_dec, decay = res["sol"], res["attn"], res["q_dec"], res["k_dec"], res["decay"]
        for ci, c in enumerate(chunks):
            ids = [ci * hd + h for h in range(hd)]
            ws = [_mm(jnp.concatenate([sol[i][:, dv:], q_dec[i]], axis=0), st[h]) for h, i in enumerate(ids)]
            v_new = [sol[i][:, :dv] - ws[h][:CHUNK] for h, i in enumerate(ids)]
            o = [ws[h][CHUNK:] + _mm(attn[i], v_new[h]) for h, i in enumerate(ids)]
            st = [decay[i] * st[h] + _mm_tn(k_dec[i], v_new[h]) for h, i in enumerate(ids)]
            for h in range(hd):
                ms = jnp.mean(o[h] * o[h], axis=-1, keepdims=True)
                cl = slice(h * dv, (h + 1) * dv)
                o_ref[cr(c), cl] = (o[h] * lax.rsqrt(ms + EPS) * ng * _silu(og_ref[cr(c), cl].astype(F32))).astype(o_ref.dtype)
    for h in range(hd):
        s_ref[h] = st[h]


def gdn_mix(zm, conv_w_pad, alog_row, dtb_row, ng, bsz, seq, tc):
    t = zm.shape[0]
    per_b = seq // tc
    hd, dk, dv = GDN_HEADS, GDN_DK, GDN_DV
    qkv_w = 3 * hd * dk

    def zspec(width, off):
        blk = off // width
        return pl.BlockSpec((tc, width), lambda b, i: (b * per_b + i, blk))

    def full(a):
        return pl.BlockSpec(a.shape, lambda b, i: (0,) * a.ndim)

    return pl.pallas_call(
        functools.partial(_gdn_kernel, nc=tc // CHUNK),
        grid=(bsz, per_b),
        in_specs=[zspec(qkv_w, ZM_DQKV), zspec(hd * dv, ZM_DOG), zspec(LANE, ZM_DBA),
                  full(conv_w_pad), full(alog_row), full(dtb_row), full(ng)],
        out_specs=pl.BlockSpec((tc, hd * dv), lambda b, i: (b * per_b + i, 0)),
        out_shape=jax.ShapeDtypeStruct((t, hd * dv), F32),
        scratch_shapes=[pltpu.VMEM((tc + 8, qkv_w), F32), pltpu.VMEM((hd, dk, dv), F32)],
        compiler_params=_params("parallel", "arbitrary"),
        name="gdn_mix",
    )(zm, zm, zm, conv_w_pad, alog_row, dtb_row, ng)


def _merge_kernel(x_ref, g_ref, sc_ref, sh_ref, ygla_ref, ygdn_ref, yst_ref, wglu_ref,
                  wz0_ref, wz1_ref, wz2_ref, wbg_ref, wbs_ref, wbd_ref, o_ref,
                  h_ref, agla_ref, as5_ref, agdn_ref, s5t_ref, *, jt, bw):
    n_rows = jt * LANE

    @pl.when(pl.program_id(2) == 0)
    def _():
        h3 = _norm_mod(x_ref[...], g_ref[...], sc_ref[...], sh_ref[...])
        h_ref[...] = h3.reshape(n_rows, h3.shape[-1]).astype(BF16)
        agla_ref[...] = ygla_ref[...].reshape(n_rows, bw).astype(BF16)
        agdn_ref[...] = ygdn_ref[...].reshape(n_rows, bw).astype(BF16)
        for jj in range(jt):
            ys = _gelu_tanh(yst_ref[jj].astype(F32))
            glu = _mm_tn(ys, wglu_ref[...])
            s5t_ref[jj * LANE:(jj + 1) * LANE, :] = glu[:, :bw] * _sigmoid(glu[:, bw:])
        as5_ref[...] = pltpu.einshape("(jr)d->rjd", s5t_ref[...], j=jt).reshape(n_rows, bw).astype(BF16)

    def branch(rows, wz_ref, a_ref, wb_ref):
        gate = _sigmoid(jnp.dot(h_ref[rows, :], wz_ref[...], preferred_element_type=F32))
        return gate * jnp.dot(a_ref[rows, :], wb_ref[...], preferred_element_type=F32)

    per = MERGE_ROWS // jt
    for rb in range(n_rows // MERGE_ROWS):
        rows = slice(rb * MERGE_ROWS, (rb + 1) * MERGE_ROWS)
        m = (branch(rows, wz0_ref, agla_ref, wbg_ref) + branch(rows, wz1_ref, as5_ref, wbs_ref)
             + branch(rows, wz2_ref, agdn_ref, wbd_ref))
        o_ref[rb * per:(rb + 1) * per] = m.reshape(per, jt, m.shape[-1])


def merge_gate(x3, g, sc_rows, sh_rows, ygla3, ygdn3, yst, wglu, wzg, wbg, wbs, wbd, jt, tn):
    r, _, d = x3.shape
    bw = ygla3.shape[2]
    ncol = d // tn
    rows = jt * LANE

    def full(a):
        return pl.BlockSpec(a.shape, lambda i, j, n: (0,) * a.ndim)

    def view(width):
        return pl.BlockSpec((LANE, jt, width), lambda i, j, n: (i, j, 0))

    def wz(b):
        return pl.BlockSpec((d, tn), lambda i, j, n: (0, b * ncol + n))

    wb = pl.BlockSpec((bw, tn), lambda i, j, n: (0, n))
    rowspec = pl.BlockSpec((LANE, 1, d), lambda i, j, n: (i, 0, 0))
    return pl.pallas_call(
        functools.partial(_merge_kernel, jt=jt, bw=bw),
        grid=(r // LANE, CHUNK // jt, ncol),
        in_specs=[view(d), full(g), rowspec, rowspec, view(bw), view(bw),
                  pl.BlockSpec((jt, yst.shape[1], LANE), lambda i, j, n: (j, 0, i)),
                  full(wglu), wz(0), wz(1), wz(2), wb, wb, wb],
        out_specs=pl.BlockSpec((LANE, jt, tn), lambda i, j, n: (i, j, n)),
        out_shape=jax.ShapeDtypeStruct((r, CHUNK, d), F32),
        scratch_shapes=[pltpu.VMEM((rows, d), BF16), pltpu.VMEM((rows, bw), BF16),
                        pltpu.VMEM((rows, bw), BF16), pltpu.VMEM((rows, bw), BF16),
                        pltpu.VMEM((rows, bw), F32)],
        compiler_params=_params("parallel", "parallel", "arbitrary"),
        name="merge_gate",
    )(x3, g, sc_rows[:, None, :], sh_rows[:, None, :], ygla3, ygdn3, yst, wglu, wzg, wzg, wzg, wbg, wbs, wbd)


def _ffn_kernel(x_ref, m_ref, wout_ref, gt1_ref, g_ref, sc_ref, sh_ref, gt2_ref, wa_ref, wb_ref, wo_ref,
                fg_ref, o_ref, x1_ref, h_ref, acc_ref, *, final_norm):
    j = pl.program_id(1)

    @pl.when(j == 0)
    def _():
        x1 = x_ref[...] + gt1_ref[...] * _mm(m_ref[...], wout_ref[...])
        x1_ref[...] = x1
        h_ref[...] = _norm_mod(x1, g_ref[...], sc_ref[...], sh_ref[...]).astype(BF16)
        acc_ref[...] = jnp.zeros_like(acc_ref)

    h = h_ref[...]
    a = jnp.dot(h, wa_ref[...], preferred_element_type=F32)
    b = jnp.dot(h, wb_ref[...], preferred_element_type=F32)
    acc_ref[...] += _mm(_silu(a) * b, wo_ref[...])

    @pl.when(j == pl.num_programs(1) - 1)
    def _():
        y = x1_ref[...] + gt2_ref[...] * acc_ref[...]
        if final_norm:
            ms = jnp.mean(y * y, axis=-1, keepdims=True)
            y = y * lax.rsqrt(ms + EPS) * fg_ref[...]
        o_ref[...] = y


def ffn_block(x2, merged, w_o, gt1, g, sc, sh, gt2, w_in, w_out, final_g, seq, tm, tf, final_norm):
    t, d = x2.shape
    dff = w_out.shape[0]
    nf = dff // tf
    per_b = seq // tm

    def bspec():
        return pl.BlockSpec((None, 1, d), lambda i, j: (i // per_b, 0, 0))

    def const(a):
        return pl.BlockSpec(a.shape, lambda i, j: (0,) * a.ndim)

    row = pl.BlockSpec((tm, d), lambda i, j: (i, 0))
    return pl.pallas_call(
        functools.partial(_ffn_kernel, final_norm=final_norm),
        grid=(t // tm, nf),
        in_specs=[row, row, const(w_o), bspec(), const(g), bspec(), bspec(), bspec(),
                  pl.BlockSpec((d, tf), lambda i, j: (0, j)),
                  pl.BlockSpec((d, tf), lambda i, j: (0, nf + j)),
                  pl.BlockSpec((tf, d), lambda i, j: (j, 0)),
                  const(final_g)],
        out_specs=row,
        out_shape=jax.ShapeDtypeStruct((t, d), F32),
        scratch_shapes=[pltpu.VMEM((tm, d), F32), pltpu.VMEM((tm, d), BF16), pltpu.VMEM((tm, d), F32)],
        compiler_params=_params("parallel", "arbitrary"),
        name="ffn_block",
    )(x2, merged, w_o, gt1, g, sc, sh, gt2, w_in, w_in, w_out, final_g)


def _forward(x, c, w_ada, b_ada, norm1_g, w_in, gla_w_lr, gla_b_lr, gla_norm_g,
             s5_lambda_re, s5_lambda_im, s5_log_dt, s5_b_re, s5_b_im, s5_c_re, s5_c_im,
             s5_d, s5_w_glu, gdn_conv_w, gdn_a_log, gdn_dt_bias, gdn_norm_g,
             w_branch_gla, w_branch_s5, w_branch_gdn, w_out, norm2_g, w_ffn_in, w_ffn_out,
             final_g):
    bsz, seq, d = x.shape
    depth = w_ada.shape[0]
    t = bsz * seq
    nchunk = seq // CHUNK
    r = bsz * nchunk
    gqk, gw = GLA_HEADS * GLA_DK, GLA_HEADS * GLA_DV
    s5w = S5_GROUPS * S5_GROUP_CH
    dqkv, dw = 3 * GDN_HEADS * GDN_DK, GDN_HEADS * GDN_DV

    mod = ada_modulation(c, w_ada, b_ada).reshape(depth, bsz, 6, 1, d)
    s5m, s5et, s5ft, s5ac = s5_params(s5_lambda_re, s5_lambda_im, s5_log_dt, s5_b_re, s5_b_im,
                                      s5_c_re, s5_c_im, s5_d)

    o_gq, o_gk, o_gv = 0, gqk, 2 * gqk
    o_glr = o_gv + gw
    o_gog = o_glr + GLA_LOWRANK
    o_s5 = o_gog + gw
    o_dqkv = o_s5 + s5w
    o_dbeta = o_dqkv + dqkv
    o_da = o_dbeta + GDN_HEADS
    o_dog = o_da + GDN_HEADS
    o_zg = o_dog + dw

    w_in, s5_w_glu, w_branch_gla, w_branch_s5, w_branch_gdn, w_out, w_ffn_in, w_ffn_out = [
        a.astype(BF16) for a in (w_in, s5_w_glu, w_branch_gla, w_branch_s5, w_branch_gdn, w_out,
                                 w_ffn_in, w_ffn_out)]
    zpad = jnp.zeros((d, LANE - GLA_LOWRANK), BF16)
    zpad2 = jnp.zeros((d, LANE - 2 * GDN_HEADS), BF16)

    x2 = x.reshape(t, d)
    for i in range(depth):
        wi = w_in[i]
        w_zm = jnp.concatenate([
            wi[:, o_gq:o_gq + gqk], wi[:, o_gk:o_gk + gqk], wi[:, o_gv:o_gv + gw],
            wi[:, o_gog:o_gog + gw], wi[:, o_dqkv:o_dqkv + dqkv], wi[:, o_dog:o_dog + dw],
            wi[:, o_glr:o_glr + GLA_LOWRANK], zpad,
            wi[:, o_dbeta:o_dbeta + 2 * GDN_HEADS], zpad2], axis=1)
        w_zg = wi[:, o_zg:]
        w_s5t = wi[:, o_s5:o_s5 + s5w].T

        sh1, sc1, gt1, sh2, sc2, gt2 = [mod[i, :, k] for k in range(6)]
        g1 = norm1_g[i].reshape(1, d)
        rows = lambda a: jnp.broadcast_to(a, (bsz, nchunk, d)).reshape(r, d)

        zm = norm_mod_matmul(x2, g1, sc1, sh1, w_zm, seq, tm=1024, tn=1280)
        x3 = x2.reshape(r, CHUNK, d)
        sc_rows, sh_rows = rows(sc1), rows(sh1)
        ut = s5_inproj(x3, g1, sc_rows, sh_rows, w_s5t, jt=8)

        wlr_pad = jnp.pad(gla_w_lr[i], ((0, LANE - GLA_LOWRANK), (0, 0)))
        y_gla = gla_mix(zm, wlr_pad, gla_b_lr[i].reshape(1, gqk), gla_norm_g[i].reshape(1, GLA_DV),
                        bsz, seq, tc=256)
        yst = s5_mix(ut, s5m, s5et, s5ft, s5ac, i, nchunk)
        conv_pad = jnp.pad(gdn_conv_w[i], ((0, 8 - GDN_CONV), (0, 0)))
        alog_row = jnp.pad(gdn_a_log[i], (GDN_HEADS, LANE - 2 * GDN_HEADS)).reshape(1, LANE)
        dtb_row = jnp.pad(gdn_dt_bias[i], (GDN_HEADS, LANE - 2 * GDN_HEADS)).reshape(1, LANE)
        y_gdn = gdn_mix(zm, conv_pad, alog_row, dtb_row, gdn_norm_g[i].reshape(1, GDN_DV),
                        bsz, seq, tc=256)

        merged = merge_gate(x3, g1, sc_rows, sh_rows, y_gla.reshape(r, CHUNK, gw), y_gdn.reshape(r, CHUNK, dw),
                            yst, s5_w_glu[i], w_zg, w_branch_gla[i], w_branch_s5[i], w_branch_gdn[i],
                            jt=8, tn=256)
        x2 = ffn_block(x2, merged.reshape(t, d), w_out[i], gt1,
                       norm2_g[i].reshape(1, d), sc2, sh2, gt2,
                       w_ffn_in[i], w_ffn_out[i],
                       final_g.reshape(1, d), seq, tm=1024, tf=256, final_norm=(i == depth - 1))
    return x2.reshape(bsz, seq, d)


def kernel(x, c, w_ada, b_ada, norm1_g, w_in, gla_w_lr, gla_b_lr, gla_norm_g, s5_lambda_re, s5_lambda_im, s5_log_dt, s5_b_re, s5_b_im, s5_c_re, s5_c_im, s5_d, s5_w_glu, gdn_conv_w, gdn_a_log, gdn_dt_bias, gdn_norm_g, w_branch_gla, w_branch_s5, w_branch_gdn, w_out, norm2_g, w_ffn_in, w_ffn_out, final_g):
    return _forward(x, c, w_ada, b_ada, norm1_g, w_in, gla_w_lr, gla_b_lr, gla_norm_g,
                    s5_lambda_re, s5_lambda_im, s5_log_dt, s5_b_re, s5_b_im, s5_c_re, s5_c_im,
                    s5_d, s5_w_glu, gdn_conv_w, gdn_a_log, gdn_dt_bias, gdn_norm_g,
                    w_branch_gla, w_branch_s5, w_branch_gdn, w_out, norm2_g, w_ffn_in, w_ffn_out,
                    final_g)
```

```python
import functools
import math

import jax
import jax.numpy as jnp
from jax import lax
from jax.experimental import pallas as pl
from jax.experimental.pallas import tpu as pltpu

F32 = jnp.float32
BF16 = jnp.bfloat16
HI = lax.Precision.HIGHEST

EPS = 1e-6
CHUNK = 64
LANE = 128
VMEM_LIMIT = 56 * 1024 * 1024

GLA_HEADS, GLA_DK, GLA_DV, GLA_LOWRANK = 4, 64, 128, 16
GLA_GATE_NORM = 16.0
S5_GROUPS, S5_GROUP_CH, S5_STATE = 32, 16, 64
GDN_HEADS, GDN_DK, GDN_DV, GDN_CONV = 4, 128, 128, 4
N_BRANCH = 3
MERGE_ROWS = 512
GDN_GROUP_CHUNKS = 4

ZM_GQ, ZM_GK, ZM_GV, ZM_GOG = 0, 256, 512, 1024
ZM_DQKV, ZM_DOG, ZM_GLR, ZM_DBA = 1536, 3072, 3584, 3712
ZM_WIDTH = 3840


def _mm(a, b):
    return jnp.dot(a.astype(BF16), b.astype(BF16), preferred_element_type=F32)


def _mm_nt(a, b):
    return lax.dot_general(a.astype(BF16), b.astype(BF16), (((1,), (1,)), ((), ())),
                           preferred_element_type=F32)


def _mm_tn(a, b):
    return lax.dot_general(a.astype(BF16), b.astype(BF16), (((0,), (0,)), ((), ())),
                           preferred_element_type=F32)


def _mm_hi(a, b):
    return jnp.dot(a, b, precision=HI, preferred_element_type=F32)


def _split3(x):
    hi = x.astype(BF16)
    r1 = x - hi.astype(F32)
    mid = r1.astype(BF16)
    return hi, mid, (r1 - mid.astype(F32)).astype(BF16)


def _select_rows(m01, x):
    return sum(jnp.dot(m01, t, preferred_element_type=F32) for t in _split3(x))


def _select_cols(x, m01):
    return sum(jnp.dot(t, m01, preferred_element_type=F32) for t in _split3(x))


def _sigmoid(x):
    return 0.5 + 0.5 * jnp.tanh(0.5 * x)


def _silu(x):
    h = 0.5 * x
    return h + h * jnp.tanh(h)


def _softplus(x):
    return jnp.maximum(x, 0.0) + jnp.log(1.0 + jnp.exp(-jnp.abs(x)))


def _log_sigmoid(x):
    return -_softplus(-x)


def _gelu_tanh(x):
    c = math.sqrt(2.0 / math.pi)
    return 0.5 * x * (1.0 + jnp.tanh(c * (x + 0.044715 * (x * x * x))))


def _norm_mod(x, g, sc, sh):
    ms = jnp.mean(x * x, axis=-1, keepdims=True)
    return (x * lax.rsqrt(ms + EPS) * g) * (1.0 + sc) + sh


def _tok_major(ref):
    return pltpu.einshape("rjd->(jr)d", ref[...])


def _params(*sem):
    return pltpu.CompilerParams(dimension_semantics=sem, vmem_limit_bytes=VMEM_LIMIT)


def _ada_kernel(c_ref, w_ref, b_ref, o_ref):
    c = c_ref[...]
    o_ref[0] = _mm_hi(_silu(c), w_ref[0]) + b_ref[0]


def ada_modulation(c, w_ada, b_ada):
    depth, d, d6 = w_ada.shape
    bsz = c.shape[0]
    rows = -(-bsz // 8) * 8
    c_pad = jnp.pad(c, ((0, rows - bsz), (0, 0)))
    out = pl.pallas_call(
        _ada_kernel,
        grid=(depth, d6 // d),
        in_specs=[pl.BlockSpec((rows, d), lambda i, j: (0, 0)),
                  pl.BlockSpec((1, d, d), lambda i, j: (i, 0, j)),
                  pl.BlockSpec((1, 1, d), lambda i, j: (i, 0, j))],
        out_specs=pl.BlockSpec((1, rows, d), lambda i, j: (i, 0, j)),
        out_shape=jax.ShapeDtypeStruct((depth, rows, d6), F32),
        compiler_params=_params("parallel", "parallel"),
        name="ada_modulation",
    )(c_pad, w_ada, b_ada.reshape(depth, 1, d6))
    return out[:, :bsz]


def _inproj_kernel(x_ref, g_ref, sc_ref, sh_ref, w_ref, o_ref, hf_ref, h_ref):
    @pl.when(pl.program_id(1) == 0)
    def _():
        h = _norm_mod(x_ref[...], g_ref[...], sc_ref[...], sh_ref[...])
        hf_ref[...] = h
        h_ref[...] = h.astype(BF16)

    o_ref[...] = jnp.dot(h_ref[...], w_ref[...], preferred_element_type=F32).astype(o_ref.dtype)


def norm_mod_matmul(x2, g, sc, sh, w, seq, tm, tn):
    t, d = x2.shape
    c = w.shape[1]
    per_b = seq // tm
    return pl.pallas_call(
        _inproj_kernel,
        grid=(t // tm, c // tn),
        in_specs=[pl.BlockSpec((tm, d), lambda i, j: (i, 0)),
                  pl.BlockSpec((1, d), lambda i, j: (0, 0)),
                  pl.BlockSpec((None, 1, d), lambda i, j: (i // per_b, 0, 0)),
                  pl.BlockSpec((None, 1, d), lambda i, j: (i // per_b, 0, 0)),
                  pl.BlockSpec((d, tn), lambda i, j: (0, j))],
        out_specs=[pl.BlockSpec((tm, tn), lambda i, j: (i, j)), pl.BlockSpec((tm, d), lambda i, j: (i, 0))],
        out_shape=[jax.ShapeDtypeStruct((t, c), BF16), jax.ShapeDtypeStruct((t, d), F32)],
        scratch_shapes=[pltpu.VMEM((tm, d), BF16)],
        compiler_params=_params("parallel", "arbitrary"),
        name="norm_mod_matmul",
    )(x2, g, sc, sh, w)


def _s5_inproj_kernel(h_ref, wt_ref, o_ref, *, jt):
    h = _tok_major(h_ref).astype(BF16)
    ut = lax.dot_general(wt_ref[...], h, (((1,), (1,)), ((), ())), preferred_element_type=F32)
    for jj in range(jt):
        o_ref[jj] = ut[:, jj * LANE:(jj + 1) * LANE].astype(BF16)


def s5_inproj(h3, wt, jt):
    r, _, d = h3.shape
    s5w = wt.shape[0]
    return pl.pallas_call(
        functools.partial(_s5_inproj_kernel, jt=jt),
        grid=(r // LANE, CHUNK // jt),
        in_specs=[pl.BlockSpec((LANE, jt, d), lambda i, j: (i, j, 0)),
                  pl.BlockSpec((s5w, d), lambda i, j: (0, 0))],
        out_specs=pl.BlockSpec((jt, s5w, LANE), lambda i, j: (j, 0, i)),
        out_shape=jax.ShapeDtypeStruct((CHUNK, s5w, r), BF16),
        compiler_params=_params("parallel", "parallel"),
        name="s5_inproj",
    )(h3, wt)


S5_CW = CHUNK * S5_GROUP_CH


def _cpow(lr_dt, li_dt, e):
    mag = jnp.exp(lr_dt * e)
    ang = li_dt * e
    return mag * jnp.cos(ang), mag * jnp.sin(ang)


def _s5_param_kernel(ldt_ref, lrc_ref, lic_ref, lrr_ref, lir_ref, bre_ref, bim_ref,
                     c1_ref, c2_ref, dcol_ref, m_ref, et_ref, ft_ref, ac_ref):
    p, cw, h = S5_STATE, S5_CW, S5_GROUP_CH
    dt = jnp.exp(ldt_ref[0])
    lrc, lic = lrc_ref[0], lic_ref[0]
    ab_re, ab_im = _cpow(lrc * dt, lic * dt, 1.0)
    den = lrc * lrc + lic * lic
    nr, ni = ab_re - 1.0, ab_im
    w_re = (nr * lrc + ni * lic) / den
    w_im = (ni * lrc - nr * lic) / den
    trow = lax.broadcasted_iota(jnp.int32, (h, cw), 0)
    tcol = lax.broadcasted_iota(jnp.int32, (h, cw), 1)
    tile_l = (trow == tcol % h).astype(BF16)
    bre, bim = _select_cols(bre_ref[0], tile_l), _select_cols(bim_ref[0], tile_l)
    bb_re = w_re * bre - w_im * bim
    bb_im = w_re * bim + w_im * bre
    lane = lax.broadcasted_iota(jnp.int32, (1, LANE), 1)
    e_m = jnp.where(lane < CHUNK, CHUNK - 1 - lane, 0).astype(F32)
    pd_re, pd_im = _cpow(lrc * dt, lic * dt, e_m)
    xrow = lax.broadcasted_iota(jnp.int32, (LANE, cw), 0)
    xcol = lax.broadcasted_iota(jnp.int32, (LANE, cw), 1)
    expand_l = (xrow == xcol // h).astype(BF16)
    p_re, p_im = _select_cols(pd_re, expand_l), _select_cols(pd_im, expand_l)
    e_re = p_re * bb_re - p_im * bb_im
    e_im = p_re * bb_im + p_im * bb_re
    et = jnp.concatenate([e_re, e_im], axis=0)
    et_ref[0] = et.astype(BF16)
    a_re, a_im = _cpow(lrc * dt, lic * dt, float(CHUNK))
    ac_ref[0] = jnp.concatenate([a_re, a_im], axis=0)
    sgn = jnp.where(lax.broadcasted_iota(jnp.int32, (1, 2 * p), 1) < p, 1.0, -1.0)
    krev = _mm_hi(c1_ref[0] * sgn, et)
    row = lax.broadcasted_iota(jnp.int32, (h, cw), 0)
    col = lax.broadcasted_iota(jnp.int32, (h, cw), 1)
    krev = krev + jnp.where(col == (cw - h) + row, dcol_ref[0], 0.0)
    rrev = jnp.concatenate([krev, jnp.zeros_like(krev)], axis=1)
    per_tile = LANE // h
    rolled = [rrev if r == 0 else pltpu.roll(rrev, 2 * cw - r * h, axis=1) for r in range(per_tile)]
    for i in range(CHUNK):
        s = (CHUNK - 1 - i) * h
        a, r = s // LANE, (s % LANE) // h
        m_ref[0, i * h:(i + 1) * h, :] = rolled[r][:, a * LANE:a * LANE + cw].astype(BF16)
    f_i = (lax.broadcasted_iota(jnp.int32, (CHUNK, 1), 0) + 1).astype(F32)
    qd_re, qd_im = _cpow(lrr_ref[0] * dt, lir_ref[0] * dt, f_i)
    yrow = lax.broadcasted_iota(jnp.int32, (cw, CHUNK), 0)
    ycol = lax.broadcasted_iota(jnp.int32, (cw, CHUNK), 1)
    expand_r = (yrow // h == ycol).astype(BF16)
    q_re, q_im = _select_rows(expand_r, qd_re), _select_rows(expand_r, qd_im)
    over_tokens = lambda c: jnp.broadcast_to(c[None], (CHUNK, h, 2 * p)).reshape(cw, 2 * p)
    ft_ref[0] = (over_tokens(c1_ref[0] * sgn) * q_re - over_tokens(c2_ref[0]) * q_im).astype(BF16)


def s5_params(lam_re, lam_im, log_dt, b_re, b_im, c_re, c_im, dpar):
    ng = lam_re.shape[0] * lam_re.shape[1]
    p, h, cw = S5_STATE, S5_GROUP_CH, S5_CW
    lam_re = lam_re.reshape(ng, p)
    lam_im = lam_im.reshape(ng, p)
    c_re = c_re.reshape(ng, h, p)
    c_im = c_im.reshape(ng, h, p)
    c1 = jnp.concatenate([c_re, c_im], axis=-1)
    c2 = jnp.concatenate([c_im, c_re], axis=-1)
    args = (log_dt.reshape(ng, 1, 1),
            lam_re.reshape(ng, p, 1), lam_im.reshape(ng, p, 1),
            jnp.tile(lam_re.reshape(ng, 1, p), (1, 1, 2)), jnp.tile(lam_im.reshape(ng, 1, p), (1, 1, 2)),
            b_re.reshape(ng, p, h), b_im.reshape(ng, p, h), c1, c2,
            dpar.reshape(ng, h, 1))

    def spec(a):
        return pl.BlockSpec((1,) + a.shape[1:], lambda i: (i, 0, 0))

    return pl.pallas_call(
        _s5_param_kernel,
        grid=(ng,),
        in_specs=[spec(a) for a in args],
        out_specs=[pl.BlockSpec((1, cw, cw), lambda i: (i, 0, 0)),
                   pl.BlockSpec((1, 2 * p, cw), lambda i: (i, 0, 0)),
                   pl.BlockSpec((1, cw, 2 * p), lambda i: (i, 0, 0)),
                   pl.BlockSpec((1, 2 * p, 1), lambda i: (i, 0, 0))],
        out_shape=[jax.ShapeDtypeStruct((ng, cw, cw), BF16),
                   jax.ShapeDtypeStruct((ng, 2 * p, cw), BF16),
                   jax.ShapeDtypeStruct((ng, cw, 2 * p), BF16),
                   jax.ShapeDtypeStruct((ng, 2 * p, 1), F32)],
        compiler_params=_params("parallel"),
        name="s5_params",
    )(*args)


def _s5_mix_kernel(u_ref, m_ref, et_ref, ft_ref, ac_ref, y_ref, *, nchunk, gsub):
    p, h = S5_STATE, S5_GROUP_CH
    r = u_ref.shape[-1]
    ks = range(gsub)
    u = [u_ref[:, k * h:(k + 1) * h, :].reshape(S5_CW, r) for k in ks]
    s = [jnp.dot(et_ref[k], u[k], preferred_element_type=F32) for k in ks]
    s_re, s_im = [s[k][:p] for k in ks], [s[k][p:] for k in ks]
    a_re, a_im = [ac_ref[k][:p] for k in ks], [ac_ref[k][p:] for k in ks]
    n_idx = lax.broadcasted_iota(jnp.int32, (1, r), 1) % nchunk
    shift = 1
    while shift < nchunk:
        keep = n_idx >= shift
        t_re = [jnp.where(keep, pltpu.roll(s_re[k], shift, axis=1), 0.0) for k in ks]
        t_im = [jnp.where(keep, pltpu.roll(s_im[k], shift, axis=1), 0.0) for k in ks]
        s_re, s_im = ([s_re[k] + a_re[k] * t_re[k] - a_im[k] * t_im[k] for k in ks],
                      [s_im[k] + a_re[k] * t_im[k] + a_im[k] * t_re[k] for k in ks])
        a_re, a_im = ([a_re[k] * a_re[k] - a_im[k] * a_im[k] for k in ks], [2.0 * a_re[k] * a_im[k] for k in ks])
        shift *= 2
    keep = n_idx >= 1
    for k in ks:
        h_prev = jnp.concatenate([jnp.where(keep, pltpu.roll(s_re[k], 1, axis=1), 0.0),
                                  jnp.where(keep, pltpu.roll(s_im[k], 1, axis=1), 0.0)], axis=0)
        y = jnp.dot(m_ref[k], u[k], preferred_element_type=F32)
        y = y + jnp.dot(ft_ref[k], h_prev.astype(BF16), preferred_element_type=F32)
        y_ref[:, k * h:(k + 1) * h, :] = y.reshape(CHUNK, h, r).astype(y_ref.dtype)


def s5_mix(ut, m, et, ft, ac, layer, nchunk, gsub):
    _, s5w, r = ut.shape
    g, h, p, cw = S5_GROUPS, S5_GROUP_CH, S5_STATE, S5_CW
    base = layer * g // gsub
    return pl.pallas_call(
        functools.partial(_s5_mix_kernel, nchunk=nchunk, gsub=gsub),
        grid=(g // gsub,),
        in_specs=[pl.BlockSpec((CHUNK, gsub * h, r), lambda i: (0, i, 0)),
                  pl.BlockSpec((gsub, cw, cw), lambda i: (base + i, 0, 0)),
                  pl.BlockSpec((gsub, 2 * p, cw), lambda i: (base + i, 0, 0)),
                  pl.BlockSpec((gsub, cw, 2 * p), lambda i: (base + i, 0, 0)),
                  pl.BlockSpec((gsub, 2 * p, 1), lambda i: (base + i, 0, 0))],
        out_specs=pl.BlockSpec((CHUNK, gsub * h, r), lambda i: (0, i, 0)),
        out_shape=jax.ShapeDtypeStruct((CHUNK, s5w, r), BF16),
        compiler_params=_params("parallel"),
        name="s5_mix",
    )(ut, m, et, ft, ac)


def _gla_kernel(q_ref, k_ref, v_ref, og_ref, lr_ref, wlr_ref, blr_ref, ng_ref, o_ref, st_ref, *, nc):
    hd, dk, dv = GLA_HEADS, GLA_DK, GLA_DV

    @pl.when(pl.program_id(1) == 0)
    def _():
        st_ref[...] = jnp.zeros_like(st_ref)

    ri = lax.broadcasted_iota(jnp.int32, (CHUNK, CHUNK), 0)
    ci = lax.broadcasted_iota(jnp.int32, (CHUNK, CHUNK), 1)
    incl = ri >= ci
    ltri = incl.astype(BF16)
    lane_k = lax.broadcasted_iota(jnp.int32, (1, hd * dk), 1)
    srow = lax.broadcasted_iota(jnp.int32, (hd * dv, hd * dk), 0)
    scol = lax.broadcasted_iota(jnp.int32, (hd * dv, hd * dk), 1)
    same_head = (srow // dv) == (scol // dk)
    wlr, blr, ng = wlr_ref[...], blr_ref[...], ng_ref[...]

    cs = range(nc)
    cr = lambda c: slice(c * CHUNK, (c + 1) * CHUNK)
    g = [_log_sigmoid(_mm(lr_ref[cr(c), :], wlr) + blr) * (1.0 / GLA_GATE_NORM) for c in cs]
    bc = [_select_rows(ltri, g[c]) for c in cs]
    bl = [bc[c][CHUNK - 1:CHUNK, :] for c in cs]
    q_e = [q_ref[cr(c), :].astype(F32) * (dk ** -0.5) * jnp.exp(bc[c]) for c in cs]
    k_e = [k_ref[cr(c), :].astype(F32) * jnp.exp(-bc[c]) for c in cs]
    k_d = [k_ref[cr(c), :].astype(F32) * jnp.exp(bl[c] - bc[c]) for c in cs]
    kv = [jnp.where(same_head, _mm_tn(v_ref[cr(c), :], k_d[c]), 0.0) for c in cs]
    sts = []
    st = st_ref[...]
    for c in cs:
        sts.append(st)
        st = jnp.exp(bl[c]) * st + kv[c]
    st_ref[...] = st
    o_inter = [_mm_nt(q_e[c], sts[c]) for c in cs]
    q_heads = [jnp.concatenate([jnp.where((lane_k // dk) == h, q_e[c], 0.0) for h in range(hd)], axis=0)
               for c in cs]
    sc_all = [_mm_nt(q_heads[c], k_e[c]) for c in cs]
    sc = [[jnp.where(incl, sc_all[c][h * CHUNK:(h + 1) * CHUNK], 0.0) for h in range(hd)] for c in cs]
    for c in cs:
        for h in range(hd):
            cols = slice(h * dv, (h + 1) * dv)
            oh = _mm(sc[c][h], v_ref[cr(c), cols]) + o_inter[c][:, cols]
            ms = jnp.mean(oh * oh, axis=-1, keepdims=True)
            o_ref[cr(c), cols] = (oh * lax.rsqrt(ms + EPS) * ng * _silu(og_ref[cr(c), cols].astype(F32))).astype(o_ref.dtype)


def gla_mix(zm, wlr_pad, blr, ng, bsz, seq, tc):
    t = zm.shape[0]
    per_b = seq // tc
    hd, dk, dv = GLA_HEADS, GLA_DK, GLA_DV

    def zspec(width, off):
        blk = off // width
        return pl.BlockSpec((tc, width), lambda b, i: (b * per_b + i, blk))

    def full(a):
        return pl.BlockSpec(a.shape, lambda b, i: (0,) * a.ndim)

    return pl.pallas_call(
        functools.partial(_gla_kernel, nc=tc // CHUNK),
        grid=(bsz, per_b),
        in_specs=[zspec(hd * dk, ZM_GQ), zspec(hd * dk, ZM_GK), zspec(hd * dv, ZM_GV),
                  zspec(hd * dv, ZM_GOG), zspec(LANE, ZM_GLR), full(wlr_pad), full(blr), full(ng)],
        out_specs=pl.BlockSpec((tc, hd * dv), lambda b, i: (b * per_b + i, 0)),
        out_shape=jax.ShapeDtypeStruct((t, hd * dv), F32),
        scratch_shapes=[pltpu.VMEM((hd * dv, hd * dk), F32)],
        compiler_params=_params("parallel", "arbitrary"),
        name="gla_mix",
    )(zm, zm, zm, zm, zm, wlr_pad, blr, ng)


def _gdn_kernel(qkv_ref, og_ref, ba_ref, cw_ref, alog_ref, dtb_ref, ng_ref, o_ref,
                xbuf_ref, s_ref, *, nc):
    hd, dk, dv, kc = GDN_HEADS, GDN_DK, GDN_DV, GDN_CONV
    tc = nc * CHUNK
    pad = 8

    @pl.when(pl.program_id(1) == 0)
    def _():
        s_ref[...] = jnp.zeros_like(s_ref)
        xbuf_ref[0:pad, :] = jnp.zeros((pad, xbuf_ref.shape[1]), F32)

    @pl.when(pl.program_id(1) != 0)
    def _():
        xbuf_ref[0:pad, :] = xbuf_ref[tc:tc + pad, :]

    xbuf_ref[pad:pad + tc, :] = qkv_ref[...].astype(F32)

    ri = lax.broadcasted_iota(jnp.int32, (CHUNK, CHUNK), 0)
    ci = lax.broadcasted_iota(jnp.int32, (CHUNK, CHUNK), 1)
    incl = ri >= ci
    strict = ri > ci
    ltri = incl.astype(BF16)
    eye = (ri == ci).astype(F32)
    cw = [cw_ref[i] for i in range(kc)]
    ng = ng_ref[...]

    ba = ba_ref[...].astype(F32)
    beta_all = _sigmoid(ba)
    g_all = -jnp.exp(alog_ref[...]) * _softplus(ba + dtb_ref[...])

    cr = lambda c: slice(c * CHUNK, (c + 1) * CHUNK)
    gam_c = [_select_rows(ltri, g_all[cr(c), :]) for c in range(nc)]
    gam_tc = [g.T for g in gam_c]
    groups = [list(range(g0, min(g0 + GDN_GROUP_CHUNKS, nc))) for g0 in range(0, nc, GDN_GROUP_CHUNKS)]

    def conv_piece(chunks, col):
        lanes = slice(col * dk, (col + 1) * dk)
        r0, nr = pad + chunks[0] * CHUNK, len(chunks) * CHUNK
        win = lambda off: xbuf_ref[off:off + nr, lanes].reshape(nr // 8, 8, dk)
        acc = cw[kc - 1][:, lanes] * win(r0)
        for i in range(kc - 1):
            acc = acc + cw[i][:, lanes] * win(r0 - (kc - 1) + i)
        y = _silu(acc.reshape(nr, dk))
        if col < 2 * hd:
            y = y * lax.rsqrt(jnp.sum(y * y, axis=-1, keepdims=True) + EPS)
        if col < hd:
            y = y * (dk ** -0.5)
        return [y[i * CHUNK:(i + 1) * CHUNK] for i in range(len(chunks))]

    def chain_stages(chunks, cols, res):
        pairs = [(ci, c, h) for ci, c in enumerate(chunks) for h in range(hd)]
        n = len(pairs)
        q = [cols[h][ci] for ci, c, h in pairs]
        k = [cols[hd + h][ci] for ci, c, h in pairs]
        v = [cols[2 * hd + h][ci] for ci, c, h in pairs]
        beta = [beta_all[cr(c), h:h + 1] for ci, c, h in pairs]
        gam = [gam_c[c][:, hd + h:hd + h + 1] for ci, c, h in pairs]
        gam_row = [gam_tc[c][hd + h:hd + h + 1, :] for ci, c, h in pairs]
        dmask = [jnp.where(incl, jnp.exp(gam[i] - gam_row[i]), 0.0) for i in range(n)]
        k_beta = [k[i] * beta[i] for i in range(n)]
        egam = [jnp.exp(gam[i]) for i in range(n)]
        kk = [_mm_nt(jnp.concatenate([k_beta[i], q[i]], axis=0), k[i]) for i in range(n)]
        low = [jnp.where(strict, kk[i][:CHUNK] * dmask[i], 0.0) for i in range(n)]
        res["attn"] = [kk[i][CHUNK:] * dmask[i] for i in range(n)]
        rhs = [jnp.concatenate([v[i] * beta[i], k_beta[i] * egam[i]], axis=1).astype(BF16) for i in range(n)]
        yield
        lowb = [low[i].astype(BF16) for i in range(n)]
        pw = [_mm(lowb[i], lowb[i]).astype(BF16) for i in range(n)]
        tinv = [eye - low[i] for i in range(n)]
        for s in range(5):
            yield
            if s < 4:
                stk = [_mm(jnp.concatenate([tinv[i].astype(BF16), pw[i]], axis=0), pw[i]) for i in range(n)]
                tinv = [tinv[i] + stk[i][:CHUNK] for i in range(n)]
                pw = [stk[i][CHUNK:].astype(BF16) for i in range(n)]
            else:
                tinv = [tinv[i] + _mm(tinv[i], pw[i]) for i in range(n)]
        yield
        res["sol"] = [_mm(tinv[i], rhs[i]) for i in range(n)]
        yield
        res["q_dec"] = [q[i] * egam[i] for i in range(n)]
        gam_last = [gam[i][CHUNK - 1:CHUNK, :] for i in range(n)]
        res["k_dec"] = [k[i] * jnp.exp(gam_last[i] - gam[i]) for i in range(n)]
        res["decay"] = [jnp.exp(gam_last[i]) for i in range(n)]

    ncols = 3 * hd
    cols = [conv_piece(groups[0], col) for col in range(ncols)]
    st = [s_ref[h] for h in range(hd)]
    for gi, chunks in enumerate(groups):
        res = {}
        nxt = groups[gi + 1] if gi + 1 < len(groups) else None
        nxt_cols = []
        for _ in chain_stages(chunks, cols, res):
            if nxt is not None and len(nxt_cols) < ncols:
                nxt_cols.append(conv_piece(nxt, len(nxt_cols)))
        while nxt is not None and len(nxt_cols) < ncols:
            nxt_cols.append(conv_piece(nxt, len(nxt_cols)))
        cols = nxt_cols
        sol, attn, q_dec, k_dec, decay = res["sol"], res["attn"], res["q_dec"], res["k_dec"], res["decay"]
        for ci, c in enumerate(chunks):
            ids = [ci * hd + h for h in range(hd)]
            ws = [_mm(jnp.concatenate([sol[i][:, dv:], q_dec[i]], axis=0), st[h]) for h, i in enumerate(ids)]
            v_new = [sol[i][:, :dv] - ws[h][:CHUNK] for h, i in enumerate(ids)]
            o = [ws[h][CHUNK:] + _mm(attn[i], v_new[h]) for h, i in enumerate(ids)]
            st = [decay[i] * st[h] + _mm_tn(k_dec[i], v_new[h]) for h, i in enumerate(ids)]
            for h in range(hd):
                ms = jnp.mean(o[h] * o[h], axis=-1, keepdims=True)
                cl = slice(h * dv, (h + 1) * dv)
                o_ref[cr(c), cl] = (o[h] * lax.rsqrt(ms + EPS) * ng * _silu(og_ref[cr(c), cl].astype(F32))).astype(o_ref.dtype)
    for h in range(hd):
        s_ref[h] = st[h]


def gdn_mix(zm, conv_w_pad, alog_row, dtb_row, ng, bsz, seq, tc):
    t = zm.shape[0]
    per_b = seq // tc
    hd, dk, dv = GDN_HEADS, GDN_DK, GDN_DV
    qkv_w = 3 * hd * dk

    def zspec(width, off):
        blk = off // width
        return pl.BlockSpec((tc, width), lambda b, i: (b * per_b + i, blk))

    def full(a):
        return pl.BlockSpec(a.shape, lambda b, i: (0,) * a.ndim)

    return pl.pallas_call(
        functools.partial(_gdn_kernel, nc=tc // CHUNK),
        grid=(bsz, per_b),
        in_specs=[zspec(qkv_w, ZM_DQKV), zspec(hd * dv, ZM_DOG), zspec(LANE, ZM_DBA),
                  full(conv_w_pad), full(alog_row), full(dtb_row), full(ng)],
        out_specs=pl.BlockSpec((tc, hd * dv), lambda b, i: (b * per_b + i, 0)),
        out_shape=jax.ShapeDtypeStruct((t, hd * dv), F32),
        scratch_shapes=[pltpu.VMEM((tc + 8, qkv_w), F32), pltpu.VMEM((hd, dk, dv), F32)],
        compiler_params=_params("parallel", "arbitrary"),
        name="gdn_mix",
    )(zm, zm, zm, conv_w_pad, alog_row, dtb_row, ng)


def _merge_kernel(h3_ref, ygla_ref, ygdn_ref, yst_ref, wglu_ref,
                  wz0_ref, wz1_ref, wz2_ref, wbg_ref, wbs_ref, wbd_ref, o_ref,
                  h_ref, agla_ref, as5_ref, agdn_ref, s5t_ref, *, jt, bw):
    n_rows = jt * LANE

    @pl.when(pl.program_id(2) == 0)
    def _():
        h_ref[...] = h3_ref[...].reshape(n_rows, h3_ref.shape[-1]).astype(BF16)
        agla_ref[...] = ygla_ref[...].reshape(n_rows, bw).astype(BF16)
        agdn_ref[...] = ygdn_ref[...].reshape(n_rows, bw).astype(BF16)
        for jj in range(jt):
            ys = _gelu_tanh(yst_ref[jj].astype(F32))
            glu = _mm_tn(ys, wglu_ref[...])
            s5t_ref[jj * LANE:(jj + 1) * LANE, :] = glu[:, :bw] * _sigmoid(glu[:, bw:])
        as5_ref[...] = pltpu.einshape("(jr)d->rjd", s5t_ref[...], j=jt).reshape(n_rows, bw).astype(BF16)

    def branch(rows, wz_ref, a_ref, wb_ref):
        gate = _sigmoid(jnp.dot(h_ref[rows, :], wz_ref[...], preferred_element_type=F32))
        return gate * jnp.dot(a_ref[rows, :], wb_ref[...], preferred_element_type=F32)

    per = MERGE_ROWS // jt
    for rb in range(n_rows // MERGE_ROWS):
        rows = slice(rb * MERGE_ROWS, (rb + 1) * MERGE_ROWS)
        m = (branch(rows, wz0_ref, agla_ref, wbg_ref) + branch(rows, wz1_ref, as5_ref, wbs_ref)
             + branch(rows, wz2_ref, agdn_ref, wbd_ref))
        o_ref[rb * per:(rb + 1) * per] = m.reshape(per, jt, m.shape[-1])


def merge_gate(h3, ygla3, ygdn3, yst, wglu, wzg, wbg, wbs, wbd, jt, tn):
    r, _, d = h3.shape
    bw = ygla3.shape[2]
    ncol = d // tn
    rows = jt * LANE

    def full(a):
        return pl.BlockSpec(a.shape, lambda i, j, n: (0,) * a.ndim)

    def view(width):
        return pl.BlockSpec((LANE, jt, width), lambda i, j, n: (i, j, 0))

    def wz(b):
        return pl.BlockSpec((d, tn), lambda i, j, n: (0, b * ncol + n))

    wb = pl.BlockSpec((bw, tn), lambda i, j, n: (0, n))
    return pl.pallas_call(
        functools.partial(_merge_kernel, jt=jt, bw=bw),
        grid=(r // LANE, CHUNK // jt, ncol),
        in_specs=[view(d), view(bw), view(bw),
                  pl.BlockSpec((jt, yst.shape[1], LANE), lambda i, j, n: (j, 0, i)),
                  full(wglu), wz(0), wz(1), wz(2), wb, wb, wb],
        out_specs=pl.BlockSpec((LANE, jt, tn), lambda i, j, n: (i, j, n)),
        out_shape=jax.ShapeDtypeStruct((r, CHUNK, d), F32),
        scratch_shapes=[pltpu.VMEM((rows, d), BF16), pltpu.VMEM((rows, bw), BF16),
                        pltpu.VMEM((rows, bw), BF16), pltpu.VMEM((rows, bw), BF16),
                        pltpu.VMEM((rows, bw), F32)],
        compiler_params=_params("parallel", "parallel", "arbitrary"),
        name="merge_gate",
    )(h3, ygla3, ygdn3, yst, wglu, wzg, wzg, wzg, wbg, wbs, wbd)


def _ffn_kernel(x_ref, m_ref, wout_ref, gt1_ref, g_ref, sc_ref, sh_ref, gt2_ref, wa_ref, wb_ref, wo_ref,
                fg_ref, o_ref, x1_ref, h_ref, acc_ref, *, final_norm):
    j = pl.program_id(1)

    @pl.when(j == 0)
    def _():
        x1 = x_ref[...] + gt1_ref[...] * _mm(m_ref[...], wout_ref[...])
        x1_ref[...] = x1
        h_ref[...] = _norm_mod(x1, g_ref[...], sc_ref[...], sh_ref[...]).astype(BF16)
        acc_ref[...] = jnp.zeros_like(acc_ref)

    h = h_ref[...]
    a = jnp.dot(h, wa_ref[...], preferred_element_type=F32)
    b = jnp.dot(h, wb_ref[...], preferred_element_type=F32)
    acc_ref[...] += _mm(_silu(a) * b, wo_ref[...])

    @pl.when(j == pl.num_programs(1) - 1)
    def _():
        y = x1_ref[...] + gt2_ref[...] * acc_ref[...]
        if final_norm:
            ms = jnp.mean(y * y, axis=-1, keepdims=True)
            y = y * lax.rsqrt(ms + EPS) * fg_ref[...]
        o_ref[...] = y


def ffn_block(x2, merged, w_o, gt1, g, sc, sh, gt2, w_in, w_out, final_g, seq, tm, tf, final_norm):
    t, d = x2.shape
    dff = w_out.shape[0]
    nf = dff // tf
    per_b = seq // tm

    def bspec():
        return pl.BlockSpec((None, 1, d), lambda i, j: (i // per_b, 0, 0))

    def const(a):
        return pl.BlockSpec(a.shape, lambda i, j: (0,) * a.ndim)

    row = pl.BlockSpec((tm, d), lambda i, j: (i, 0))
    return pl.pallas_call(
        functools.partial(_ffn_kernel, final_norm=final_norm),
        grid=(t // tm, nf),
        in_specs=[row, row, const(w_o), bspec(), const(g), bspec(), bspec(), bspec(),
                  pl.BlockSpec((d, tf), lambda i, j: (0, j)),
                  pl.BlockSpec((d, tf), lambda i, j: (0, nf + j)),
                  pl.BlockSpec((tf, d), lambda i, j: (j, 0)),
                  const(final_g)],
        out_specs=row,
        out_shape=jax.ShapeDtypeStruct((t, d), F32),
        scratch_shapes=[pltpu.VMEM((tm, d), F32), pltpu.VMEM((tm, d), BF16), pltpu.VMEM((tm, d), F32)],
        compiler_params=_params("parallel", "arbitrary"),
        name="ffn_block",
    )(x2, merged, w_o, gt1, g, sc, sh, gt2, w_in, w_in, w_out, final_g)


def _wprep_kernel(w_ref, zm_ref, zg_ref, s5t_ref, *, segs, pads, zg_off, s5_off):
    for dst, width in pads:
        zm_ref[0, :, dst:dst + width] = jnp.zeros((zm_ref.shape[1], width), BF16)
    for src, width, dst in segs:
        zm_ref[0, :, dst:dst + width] = w_ref[0, :, src:src + width].astype(BF16)
    zg_ref[0] = w_ref[0, :, zg_off:zg_off + zg_ref.shape[2]].astype(BF16)
    s5t_ref[0] = w_ref[0, :, s5_off:s5_off + s5t_ref.shape[1]].T.astype(BF16)


def prep_in_weights(w_in, segs, pads, zg_off, zg_w, s5_off, s5_w, tr):
    depth, d, d_in = w_in.shape
    return pl.pallas_call(
        functools.partial(_wprep_kernel, segs=segs, pads=pads, zg_off=zg_off, s5_off=s5_off),
        grid=(depth, d // tr),
        in_specs=[pl.BlockSpec((1, tr, d_in), lambda i, j: (i, j, 0))],
        out_specs=[pl.BlockSpec((1, tr, ZM_WIDTH), lambda i, j: (i, j, 0)),
                   pl.BlockSpec((1, tr, zg_w), lambda i, j: (i, j, 0)),
                   pl.BlockSpec((1, s5_w, tr), lambda i, j: (i, 0, j))],
        out_shape=[jax.ShapeDtypeStruct((depth, d, ZM_WIDTH), BF16),
                   jax.ShapeDtypeStruct((depth, d, zg_w), BF16),
                   jax.ShapeDtypeStruct((depth, s5_w, d), BF16)],
        compiler_params=_params("parallel", "parallel"),
        name="prep_in_weights",
    )(w_in)


def _forward(x, c, w_ada, b_ada, norm1_g, w_in, gla_w_lr, gla_b_lr, gla_norm_g,
             s5_lambda_re, s5_lambda_im, s5_log_dt, s5_b_re, s5_b_im, s5_c_re, s5_c_im,
             s5_d, s5_w_glu, gdn_conv_w, gdn_a_log, gdn_dt_bias, gdn_norm_g,
             w_branch_gla, w_branch_s5, w_branch_gdn, w_out, norm2_g, w_ffn_in, w_ffn_out,
             final_g):
    bsz, seq, d = x.shape
    depth = w_ada.shape[0]
    t = bsz * seq
    nchunk = seq // CHUNK
    r = bsz * nchunk
    gqk, gw = GLA_HEADS * GLA_DK, GLA_HEADS * GLA_DV
    s5w = S5_GROUPS * S5_GROUP_CH
    dqkv, dw = 3 * GDN_HEADS * GDN_DK, GDN_HEADS * GDN_DV

    mod = ada_modulation(c, w_ada, b_ada).reshape(depth, bsz, 6, 1, d)
    s5m, s5et, s5ft, s5ac = s5_params(s5_lambda_re, s5_lambda_im, s5_log_dt, s5_b_re, s5_b_im,
                                      s5_c_re, s5_c_im, s5_d)

    o_gq, o_gk, o_gv = 0, gqk, 2 * gqk
    o_glr = o_gv + gw
    o_gog = o_glr + GLA_LOWRANK
    o_s5 = o_gog + gw
    o_dqkv = o_s5 + s5w
    o_dbeta = o_dqkv + dqkv
    o_da = o_dbeta + GDN_HEADS
    o_dog = o_da + GDN_HEADS
    o_zg = o_dog + dw

    segs = ((o_gq, gqk, ZM_GQ), (o_gk, gqk, ZM_GK), (o_gv, gw, ZM_GV), (o_gog, gw, ZM_GOG),
            (o_dqkv, dqkv, ZM_DQKV), (o_dog, dw, ZM_DOG), (o_glr, GLA_LOWRANK, ZM_GLR),
            (o_dbeta, 2 * GDN_HEADS, ZM_DBA))
    pads = ((ZM_GLR, LANE), (ZM_DBA, LANE))
    w_zm_all, w_zg_all, w_s5t_all = prep_in_weights(w_in, segs, pads, o_zg, N_BRANCH * d, o_s5, s5w, tr=256)
    s5_w_glu, w_branch_gla, w_branch_s5, w_branch_gdn, w_out, w_ffn_in, w_ffn_out = [
        a.astype(BF16) for a in (s5_w_glu, w_branch_gla, w_branch_s5, w_branch_gdn, w_out, w_ffn_in, w_ffn_out)]

    x2 = x.reshape(t, d)
    for i in range(depth):
        w_zm, w_zg, w_s5t = w_zm_all[i], w_zg_all[i], w_s5t_all[i]

        sh1, sc1, gt1, sh2, sc2, gt2 = [mod[i, :, k] for k in range(6)]
        g1 = norm1_g[i].reshape(1, d)

        zm, hf = norm_mod_matmul(x2, g1, sc1, sh1, w_zm, seq, tm=1024, tn=1280)
        h3 = hf.reshape(r, CHUNK, d)
        ut = s5_inproj(h3, w_s5t, jt=8)

        wlr_pad = jnp.pad(gla_w_lr[i], ((0, LANE - GLA_LOWRANK), (0, 0)))
        y_gla = gla_mix(zm, wlr_pad, gla_b_lr[i].reshape(1, gqk), gla_norm_g[i].reshape(1, GLA_DV),
                        bsz, seq, tc=256)
        yst = s5_mix(ut, s5m, s5et, s5ft, s5ac, i, nchunk, gsub=4)
        conv_pad = jnp.broadcast_to(gdn_conv_w[i][:, None, :], (GDN_CONV, 8, dqkv))
        alog_row = jnp.pad(gdn_a_log[i], (GDN_HEADS, LANE - 2 * GDN_HEADS)).reshape(1, LANE)
        dtb_row = jnp.pad(gdn_dt_bias[i], (GDN_HEADS, LANE - 2 * GDN_HEADS)).reshape(1, LANE)
        y_gdn = gdn_mix(zm, conv_pad, alog_row, dtb_row, gdn_norm_g[i].reshape(1, GDN_DV),
                        bsz, seq, tc=256)

        merged = merge_gate(h3, y_gla.reshape(r, CHUNK, gw), y_gdn.reshape(r, CHUNK, dw),
                            yst, s5_w_glu[i], w_zg, w_branch_gla[i], w_branch_s5[i], w_branch_gdn[i],
                            jt=8, tn=256)
        x2 = ffn_block(x2, merged.reshape(t, d), w_out[i], gt1,
                       norm2_g[i].reshape(1, d), sc2, sh2, gt2,
                       w_ffn_in[i], w_ffn_out[i],
                       final_g.reshape(1, d), seq, tm=1024, tf=256, final_norm=(i == depth - 1))
    return x2.reshape(bsz, seq, d)


def kernel(x, c, w_ada, b_ada, norm1_g, w_in, gla_w_lr, gla_b_lr, gla_norm_g, s5_lambda_re, s5_lambda_im, s5_log_dt, s5_b_re, s5_b_im, s5_c_re, s5_c_im, s5_d, s5_w_glu, gdn_conv_w, gdn_a_log, gdn_dt_bias, gdn_norm_g, w_branch_gla, w_branch_s5, w_branch_gdn, w_out, norm2_g, w_ffn_in, w_ffn_out, final_g):
    return _forward(x, c, w_ada, b_ada, norm1_g, w_in, gla_w_lr, gla_b_lr, gla_norm_g,
                    s5_lambda_re, s5_lambda_im, s5_log_dt, s5_b_re, s5_b_im, s5_c_re, s5_c_im,
                    s5_d, s5_w_glu, gdn_conv_w, gdn_a_log, gdn_dt_bias, gdn_norm_g,
                    w_branch_gla, w_branch_s5, w_branch_gdn, w_out, norm2_g, w_ffn_in, w_ffn_out,
                    final_g)
```

```python
import functools
import math

import jax
import jax.numpy as jnp
from jax import lax
from jax.experimental import pallas as pl
from jax.experimental.pallas import tpu as pltpu

F32 = jnp.float32
BF16 = jnp.bfloat16
HI = lax.Precision.HIGHEST

EPS = 1e-6
CHUNK = 64
LANE = 128
VMEM_LIMIT = 56 * 1024 * 1024

GLA_HEADS, GLA_DK, GLA_DV, GLA_LOWRANK = 4, 64, 128, 16
GLA_GATE_NORM = 16.0
S5_GROUPS, S5_GROUP_CH, S5_STATE = 32, 16, 64
GDN_HEADS, GDN_DK, GDN_DV, GDN_CONV = 4, 128, 128, 4
N_BRANCH = 3
MERGE_ROWS = 512
GDN_GROUP_CHUNKS = 4

ZM_GQ, ZM_GK, ZM_GV, ZM_GOG = 0, 256, 512, 1024
ZM_DQKV, ZM_DOG, ZM_GLR, ZM_DBA = 1536, 3072, 3584, 3712
ZM_WIDTH = 3840


def _mm(a, b):
    return jnp.dot(a.astype(BF16), b.astype(BF16), preferred_element_type=F32)


def _mm_nt(a, b):
    return lax.dot_general(a.astype(BF16), b.astype(BF16), (((1,), (1,)), ((), ())),
                           preferred_element_type=F32)


def _mm_tn(a, b):
    return lax.dot_general(a.astype(BF16), b.astype(BF16), (((0,), (0,)), ((), ())),
                           preferred_element_type=F32)


def _mm_hi(a, b):
    return jnp.dot(a, b, precision=HI, preferred_element_type=F32)


def _split3(x):
    hi = x.astype(BF16)
    r1 = x - hi.astype(F32)
    mid = r1.astype(BF16)
    return hi, mid, (r1 - mid.astype(F32)).astype(BF16)


def _select_rows(m01, x):
    return sum(jnp.dot(m01, t, preferred_element_type=F32) for t in _split3(x))


def _select_cols(x, m01):
    return sum(jnp.dot(t, m01, preferred_element_type=F32) for t in _split3(x))


def _sigmoid(x):
    return 0.5 + 0.5 * jnp.tanh(0.5 * x)


def _silu(x):
    h = 0.5 * x
    return h + h * jnp.tanh(h)


def _softplus(x):
    return jnp.maximum(x, 0.0) + jnp.log(1.0 + jnp.exp(-jnp.abs(x)))


def _log_sigmoid(x):
    return -_softplus(-x)


def _gelu_tanh(x):
    c = math.sqrt(2.0 / math.pi)
    return 0.5 * x * (1.0 + jnp.tanh(c * (x + 0.044715 * (x * x * x))))


def _norm_mod(x, g, sc, sh):
    ms = jnp.mean(x * x, axis=-1, keepdims=True)
    return (x * lax.rsqrt(ms + EPS) * g) * (1.0 + sc) + sh


def _tok_major(ref):
    return pltpu.einshape("rjd->(jr)d", ref[...])


def _params(*sem):
    return pltpu.CompilerParams(dimension_semantics=sem, vmem_limit_bytes=VMEM_LIMIT)


def _ada_kernel(c_ref, w_ref, b_ref, o_ref):
    c = c_ref[...]
    o_ref[0] = _mm_hi(_silu(c), w_ref[0]) + b_ref[0]


def ada_modulation(c, w_ada, b_ada):
    depth, d, d6 = w_ada.shape
    bsz = c.shape[0]
    rows = -(-bsz // 8) * 8
    c_pad = jnp.pad(c, ((0, rows - bsz), (0, 0)))
    out = pl.pallas_call(
        _ada_kernel,
        grid=(depth, d6 // d),
        in_specs=[pl.BlockSpec((rows, d), lambda i, j: (0, 0)),
                  pl.BlockSpec((1, d, d), lambda i, j: (i, 0, j)),
                  pl.BlockSpec((1, 1, d), lambda i, j: (i, 0, j))],
        out_specs=pl.BlockSpec((1, rows, d), lambda i, j: (i, 0, j)),
        out_shape=jax.ShapeDtypeStruct((depth, rows, d6), F32),
        compiler_params=_params("parallel", "parallel"),
        name="ada_modulation",
    )(c_pad, w_ada, b_ada.reshape(depth, 1, d6))
    return out[:, :bsz]


def _inproj_kernel(x_ref, g_ref, sc_ref, sh_ref, w_ref, o_ref, hf_ref, h_ref):
    @pl.when(pl.program_id(1) == 0)
    def _():
        h = _norm_mod(x_ref[...], g_ref[...], sc_ref[...], sh_ref[...])
        hf_ref[...] = h
        h_ref[...] = h.astype(BF16)

    o_ref[...] = jnp.dot(h_ref[...], w_ref[...], preferred_element_type=F32).astype(o_ref.dtype)


def norm_mod_matmul(x2, g, sc, sh, w, layer, seq, tm, tn):
    t, d = x2.shape
    c = w.shape[2]
    per_b = seq // tm
    return pl.pallas_call(
        _inproj_kernel,
        grid=(t // tm, c // tn),
        in_specs=[pl.BlockSpec((tm, d), lambda i, j: (i, 0)),
                  pl.BlockSpec((1, d), lambda i, j: (0, 0)),
                  pl.BlockSpec((None, 1, d), lambda i, j: (i // per_b, 0, 0)),
                  pl.BlockSpec((None, 1, d), lambda i, j: (i // per_b, 0, 0)),
                  pl.BlockSpec((None, d, tn), lambda i, j: (layer, 0, j))],
        out_specs=[pl.BlockSpec((tm, tn), lambda i, j: (i, j)), pl.BlockSpec((tm, d), lambda i, j: (i, 0))],
        out_shape=[jax.ShapeDtypeStruct((t, c), BF16), jax.ShapeDtypeStruct((t, d), F32)],
        scratch_shapes=[pltpu.VMEM((tm, d), BF16)],
        compiler_params=_params("parallel", "arbitrary"),
        name="norm_mod_matmul",
    )(x2, g, sc, sh, w)


def _s5_inproj_kernel(h_ref, wt_ref, o_ref, *, jt):
    h = _tok_major(h_ref).astype(BF16)
    ut = lax.dot_general(wt_ref[...], h, (((1,), (1,)), ((), ())), preferred_element_type=F32)
    for jj in range(jt):
        o_ref[jj] = ut[:, jj * LANE:(jj + 1) * LANE].astype(BF16)


def s5_inproj(h3, wt, layer, jt):
    r, _, d = h3.shape
    s5w = wt.shape[1]
    return pl.pallas_call(
        functools.partial(_s5_inproj_kernel, jt=jt),
        grid=(r // LANE, CHUNK // jt),
        in_specs=[pl.BlockSpec((LANE, jt, d), lambda i, j: (i, j, 0)),
                  pl.BlockSpec((None, s5w, d), lambda i, j: (layer, 0, 0))],
        out_specs=pl.BlockSpec((jt, s5w, LANE), lambda i, j: (j, 0, i)),
        out_shape=jax.ShapeDtypeStruct((CHUNK, s5w, r), BF16),
        compiler_params=_params("parallel", "parallel"),
        name="s5_inproj",
    )(h3, wt)


S5_CW = CHUNK * S5_GROUP_CH


def _cpow(lr_dt, li_dt, e):
    mag = jnp.exp(lr_dt * e)
    ang = li_dt * e
    return mag * jnp.cos(ang), mag * jnp.sin(ang)


def _s5_param_kernel(ldt_ref, lrc_ref, lic_ref, lrr_ref, lir_ref, bre_ref, bim_ref,
                     c1_ref, c2_ref, dcol_ref, m_ref, et_ref, ft_ref, ac_ref):
    p, cw, h = S5_STATE, S5_CW, S5_GROUP_CH
    dt = jnp.exp(ldt_ref[0])
    lrc, lic = lrc_ref[0], lic_ref[0]
    ab_re, ab_im = _cpow(lrc * dt, lic * dt, 1.0)
    den = lrc * lrc + lic * lic
    nr, ni = ab_re - 1.0, ab_im
    w_re = (nr * lrc + ni * lic) / den
    w_im = (ni * lrc - nr * lic) / den
    trow = lax.broadcasted_iota(jnp.int32, (h, cw), 0)
    tcol = lax.broadcasted_iota(jnp.int32, (h, cw), 1)
    tile_l = (trow == tcol % h).astype(BF16)
    bre, bim = _select_cols(bre_ref[0], tile_l), _select_cols(bim_ref[0], tile_l)
    bb_re = w_re * bre - w_im * bim
    bb_im = w_re * bim + w_im * bre
    lane = lax.broadcasted_iota(jnp.int32, (1, LANE), 1)
    e_m = jnp.where(lane < CHUNK, CHUNK - 1 - lane, 0).astype(F32)
    pd_re, pd_im = _cpow(lrc * dt, lic * dt, e_m)
    xrow = lax.broadcasted_iota(jnp.int32, (LANE, cw), 0)
    xcol = lax.broadcasted_iota(jnp.int32, (LANE, cw), 1)
    expand_l = (xrow == xcol // h).astype(BF16)
    p_re, p_im = _select_cols(pd_re, expand_l), _select_cols(pd_im, expand_l)
    e_re = p_re * bb_re - p_im * bb_im
    e_im = p_re * bb_im + p_im * bb_re
    et = jnp.concatenate([e_re, e_im], axis=0)
    et_ref[0] = et.astype(BF16)
    a_re, a_im = _cpow(lrc * dt, lic * dt, float(CHUNK))
    ac_ref[0] = jnp.concatenate([a_re, a_im], axis=0)
    sgn = jnp.where(lax.broadcasted_iota(jnp.int32, (1, 2 * p), 1) < p, 1.0, -1.0)
    krev = _mm_hi(c1_ref[0] * sgn, et)
    row = lax.broadcasted_iota(jnp.int32, (h, cw), 0)
    col = lax.broadcasted_iota(jnp.int32, (h, cw), 1)
    krev = krev + jnp.where(col == (cw - h) + row, dcol_ref[0], 0.0)
    rrev = jnp.concatenate([krev, jnp.zeros_like(krev)], axis=1)
    per_tile = LANE // h
    rolled = [rrev if r == 0 else pltpu.roll(rrev, 2 * cw - r * h, axis=1) for r in range(per_tile)]
    for i in range(CHUNK):
        s = (CHUNK - 1 - i) * h
        a, r = s // LANE, (s % LANE) // h
        m_ref[0, i * h:(i + 1) * h, :] = rolled[r][:, a * LANE:a * LANE + cw].astype(BF16)
    f_i = (lax.broadcasted_iota(jnp.int32, (CHUNK, 1), 0) + 1).astype(F32)
    qd_re, qd_im = _cpow(lrr_ref[0] * dt, lir_ref[0] * dt, f_i)
    yrow = lax.broadcasted_iota(jnp.int32, (cw, CHUNK), 0)
    ycol = lax.broadcasted_iota(jnp.int32, (cw, CHUNK), 1)
    expand_r = (yrow // h == ycol).astype(BF16)
    q_re, q_im = _select_rows(expand_r, qd_re), _select_rows(expand_r, qd_im)
    over_tokens = lambda c: jnp.broadcast_to(c[None], (CHUNK, h, 2 * p)).reshape(cw, 2 * p)
    ft_ref[0] = (over_tokens(c1_ref[0] * sgn) * q_re - over_tokens(c2_ref[0]) * q_im).astype(BF16)


def s5_params(lam_re, lam_im, log_dt, b_re, b_im, c_re, c_im, dpar):
    ng = lam_re.shape[0] * lam_re.shape[1]
    p, h, cw = S5_STATE, S5_GROUP_CH, S5_CW
    lam_re = lam_re.reshape(ng, p)
    lam_im = lam_im.reshape(ng, p)
    c_re = c_re.reshape(ng, h, p)
    c_im = c_im.reshape(ng, h, p)
    c1 = jnp.concatenate([c_re, c_im], axis=-1)
    c2 = jnp.concatenate([c_im, c_re], axis=-1)
    args = (log_dt.reshape(ng, 1, 1),
            lam_re.reshape(ng, p, 1), lam_im.reshape(ng, p, 1),
            jnp.tile(lam_re.reshape(ng, 1, p), (1, 1, 2)), jnp.tile(lam_im.reshape(ng, 1, p), (1, 1, 2)),
            b_re.reshape(ng, p, h), b_im.reshape(ng, p, h), c1, c2,
            dpar.reshape(ng, h, 1))

    def spec(a):
        return pl.BlockSpec((1,) + a.shape[1:], lambda i: (i, 0, 0))

    return pl.pallas_call(
        _s5_param_kernel,
        grid=(ng,),
        in_specs=[spec(a) for a in args],
        out_specs=[pl.BlockSpec((1, cw, cw), lambda i: (i, 0, 0)),
                   pl.BlockSpec((1, 2 * p, cw), lambda i: (i, 0, 0)),
                   pl.BlockSpec((1, cw, 2 * p), lambda i: (i, 0, 0)),
                   pl.BlockSpec((1, 2 * p, 1), lambda i: (i, 0, 0))],
        out_shape=[jax.ShapeDtypeStruct((ng, cw, cw), BF16),
                   jax.ShapeDtypeStruct((ng, 2 * p, cw), BF16),
                   jax.ShapeDtypeStruct((ng, cw, 2 * p), BF16),
                   jax.ShapeDtypeStruct((ng, 2 * p, 1), F32)],
        compiler_params=_params("parallel"),
        name="s5_params",
    )(*args)


def _s5_mix_kernel(u_ref, m_ref, et_ref, ft_ref, ac_ref, y_ref, *, nchunk, gsub):
    p, h = S5_STATE, S5_GROUP_CH
    r = u_ref.shape[-1]
    ks = range(gsub)
    u = [u_ref[:, k * h:(k + 1) * h, :].reshape(S5_CW, r) for k in ks]
    s = [jnp.dot(et_ref[k], u[k], preferred_element_type=F32) for k in ks]
    s_re, s_im = [s[k][:p] for k in ks], [s[k][p:] for k in ks]
    a_re, a_im = [ac_ref[k][:p] for k in ks], [ac_ref[k][p:] for k in ks]
    n_idx = lax.broadcasted_iota(jnp.int32, (1, r), 1) % nchunk
    shift = 1
    while shift < nchunk:
        keep = n_idx >= shift
        t_re = [jnp.where(keep, pltpu.roll(s_re[k], shift, axis=1), 0.0) for k in ks]
        t_im = [jnp.where(keep, pltpu.roll(s_im[k], shift, axis=1), 0.0) for k in ks]
        s_re, s_im = ([s_re[k] + a_re[k] * t_re[k] - a_im[k] * t_im[k] for k in ks],
                      [s_im[k] + a_re[k] * t_im[k] + a_im[k] * t_re[k] for k in ks])
        a_re, a_im = ([a_re[k] * a_re[k] - a_im[k] * a_im[k] for k in ks], [2.0 * a_re[k] * a_im[k] for k in ks])
        shift *= 2
    keep = n_idx >= 1
    for k in ks:
        h_prev = jnp.concatenate([jnp.where(keep, pltpu.roll(s_re[k], 1, axis=1), 0.0),
                                  jnp.where(keep, pltpu.roll(s_im[k], 1, axis=1), 0.0)], axis=0)
        y = jnp.dot(m_ref[k], u[k], preferred_element_type=F32)
        y = y + jnp.dot(ft_ref[k], h_prev.astype(BF16), preferred_element_type=F32)
        y_ref[:, k * h:(k + 1) * h, :] = y.reshape(CHUNK, h, r).astype(y_ref.dtype)


def s5_mix(ut, m, et, ft, ac, layer, nchunk, gsub):
    _, s5w, r = ut.shape
    g, h, p, cw = S5_GROUPS, S5_GROUP_CH, S5_STATE, S5_CW
    base = layer * g // gsub
    return pl.pallas_call(
        functools.partial(_s5_mix_kernel, nchunk=nchunk, gsub=gsub),
        grid=(g // gsub,),
        in_specs=[pl.BlockSpec((CHUNK, gsub * h, r), lambda i: (0, i, 0)),
                  pl.BlockSpec((gsub, cw, cw), lambda i: (base + i, 0, 0)),
                  pl.BlockSpec((gsub, 2 * p, cw), lambda i: (base + i, 0, 0)),
                  pl.BlockSpec((gsub, cw, 2 * p), lambda i: (base + i, 0, 0)),
                  pl.BlockSpec((gsub, 2 * p, 1), lambda i: (base + i, 0, 0))],
        out_specs=pl.BlockSpec((CHUNK, gsub * h, r), lambda i: (0, i, 0)),
        out_shape=jax.ShapeDtypeStruct((CHUNK, s5w, r), BF16),
        compiler_params=_params("parallel"),
        name="s5_mix",
    )(ut, m, et, ft, ac)


def _gla_kernel(q_ref, k_ref, v_ref, og_ref, lr_ref, wlr_ref, blr_ref, ng_ref, o_ref, st_ref, *, nc):
    hd, dk, dv = GLA_HEADS, GLA_DK, GLA_DV

    @pl.when(pl.program_id(1) == 0)
    def _():
        st_ref[...] = jnp.zeros_like(st_ref)

    ri = lax.broadcasted_iota(jnp.int32, (CHUNK, CHUNK), 0)
    ci = lax.broadcasted_iota(jnp.int32, (CHUNK, CHUNK), 1)
    incl = ri >= ci
    ltri = incl.astype(BF16)
    lane_k = lax.broadcasted_iota(jnp.int32, (1, hd * dk), 1)
    srow = lax.broadcasted_iota(jnp.int32, (hd * dv, hd * dk), 0)
    scol = lax.broadcasted_iota(jnp.int32, (hd * dv, hd * dk), 1)
    same_head = (srow // dv) == (scol // dk)
    wlr, blr, ng = wlr_ref[...], blr_ref[...], ng_ref[...]

    cs = range(nc)
    cr = lambda c: slice(c * CHUNK, (c + 1) * CHUNK)
    g = [_log_sigmoid(_mm(lr_ref[cr(c), :], wlr) + blr) * (1.0 / GLA_GATE_NORM) for c in cs]
    bc = [_select_rows(ltri, g[c]) for c in cs]
    bl = [bc[c][CHUNK - 1:CHUNK, :] for c in cs]
    q_e = [q_ref[cr(c), :].astype(F32) * (dk ** -0.5) * jnp.exp(bc[c]) for c in cs]
    k_e = [k_ref[cr(c), :].astype(F32) * jnp.exp(-bc[c]) for c in cs]
    k_d = [k_ref[cr(c), :].astype(F32) * jnp.exp(bl[c] - bc[c]) for c in cs]
    kv = [jnp.where(same_head, _mm_tn(v_ref[cr(c), :], k_d[c]), 0.0) for c in cs]
    sts = []
    st = st_ref[...]
    for c in cs:
        sts.append(st)
        st = jnp.exp(bl[c]) * st + kv[c]
    st_ref[...] = st
    o_inter = [_mm_nt(q_e[c], sts[c]) for c in cs]
    q_heads = [jnp.concatenate([jnp.where((lane_k // dk) == h, q_e[c], 0.0) for h in range(hd)], axis=0)
               for c in cs]
    sc_all = [_mm_nt(q_heads[c], k_e[c]) for c in cs]
    sc = [[jnp.where(incl, sc_all[c][h * CHUNK:(h + 1) * CHUNK], 0.0) for h in range(hd)] for c in cs]
    for c in cs:
        for h in range(hd):
            cols = slice(h * dv, (h + 1) * dv)
            oh = _mm(sc[c][h], v_ref[cr(c), cols]) + o_inter[c][:, cols]
            ms = jnp.mean(oh * oh, axis=-1, keepdims=True)
            o_ref[cr(c), cols] = (oh * lax.rsqrt(ms + EPS) * ng * _silu(og_ref[cr(c), cols].astype(F32))).astype(o_ref.dtype)


def gla_mix(zm, wlr_pad, blr, ng, bsz, seq, tc):
    t = zm.shape[0]
    per_b = seq // tc
    hd, dk, dv = GLA_HEADS, GLA_DK, GLA_DV

    def zspec(width, off):
        blk = off // width
        return pl.BlockSpec((tc, width), lambda b, i: (b * per_b + i, blk))

    def full(a):
        return pl.BlockSpec(a.shape, lambda b, i: (0,) * a.ndim)

    return pl.pallas_call(
        functools.partial(_gla_kernel, nc=tc // CHUNK),
        grid=(bsz, per_b),
        in_specs=[zspec(hd * dk, ZM_GQ), zspec(hd * dk, ZM_GK), zspec(hd * dv, ZM_GV),
                  zspec(hd * dv, ZM_GOG), zspec(LANE, ZM_GLR), full(wlr_pad), full(blr), full(ng)],
        out_specs=pl.BlockSpec((tc, hd * dv), lambda b, i: (b * per_b + i, 0)),
        out_shape=jax.ShapeDtypeStruct((t, hd * dv), F32),
        scratch_shapes=[pltpu.VMEM((hd * dv, hd * dk), F32)],
        compiler_params=_params("parallel", "arbitrary"),
        name="gla_mix",
    )(zm, zm, zm, zm, zm, wlr_pad, blr, ng)


def _gdn_kernel(qkv_ref, og_ref, ba_ref, cw_ref, alog_ref, dtb_ref, ng_ref, o_ref,
                xbuf_ref, s_ref, *, nc):
    hd, dk, dv, kc = GDN_HEADS, GDN_DK, GDN_DV, GDN_CONV
    tc = nc * CHUNK
    pad = 8

    @pl.when(pl.program_id(1) == 0)
    def _():
        s_ref[...] = jnp.zeros_like(s_ref)
        xbuf_ref[0:pad, :] = jnp.zeros((pad, xbuf_ref.shape[1]), F32)

    @pl.when(pl.program_id(1) != 0)
    def _():
        xbuf_ref[0:pad, :] = xbuf_ref[tc:tc + pad, :]

    xbuf_ref[pad:pad + tc, :] = qkv_ref[...].astype(F32)

    ri = lax.broadcasted_iota(jnp.int32, (CHUNK, CHUNK), 0)
    ci = lax.broadcasted_iota(jnp.int32, (CHUNK, CHUNK), 1)
    incl = ri >= ci
    strict = ri > ci
    ltri = incl.astype(BF16)
    eye = (ri == ci).astype(F32)
    cw = [cw_ref[i] for i in range(kc)]
    ng = ng_ref[...]

    ba = ba_ref[...].astype(F32)
    beta_all = _sigmoid(ba)
    g_all = -jnp.exp(alog_ref[...]) * _softplus(ba + dtb_ref[...])

    cr = lambda c: slice(c * CHUNK, (c + 1) * CHUNK)
    gam_c = [_select_rows(ltri, g_all[cr(c), :]) for c in range(nc)]
    gam_tc = [g.T for g in gam_c]
    groups = [list(range(g0, min(g0 + GDN_GROUP_CHUNKS, nc))) for g0 in range(0, nc, GDN_GROUP_CHUNKS)]

    def conv_piece(chunks, col):
        lanes = slice(col * dk, (col + 1) * dk)
        r0, nr = pad + chunks[0] * CHUNK, len(chunks) * CHUNK
        win = lambda off: xbuf_ref[off:off + nr, lanes].reshape(nr // 8, 8, dk)
        acc = cw[kc - 1][:, lanes] * win(r0)
        for i in range(kc - 1):
            acc = acc + cw[i][:, lanes] * win(r0 - (kc - 1) + i)
        y = _silu(acc.reshape(nr, dk))
        if col < 2 * hd:
            y = y * lax.rsqrt(jnp.sum(y * y, axis=-1, keepdims=True) + EPS)
        if col < hd:
            y = y * (dk ** -0.5)
        return [y[i * CHUNK:(i + 1) * CHUNK] for i in range(len(chunks))]

    def chain_stages(chunks, cols, res):
        pairs = [(ci, c, h) for ci, c in enumerate(chunks) for h in range(hd)]
        n = len(pairs)
        q = [cols[h][ci] for ci, c, h in pairs]
        k = [cols[hd + h][ci] for ci, c, h in pairs]
        v = [cols[2 * hd + h][ci] for ci, c, h in pairs]
        beta = [beta_all[cr(c), h:h + 1] for ci, c, h in pairs]
        gam = [gam_c[c][:, hd + h:hd + h + 1] for ci, c, h in pairs]
        gam_row = [gam_tc[c][hd + h:hd + h + 1, :] for ci, c, h in pairs]
        dmask = [jnp.where(incl, jnp.exp(gam[i] - gam_row[i]), 0.0) for i in range(n)]
        k_beta = [k[i] * beta[i] for i in range(n)]
        egam = [jnp.exp(gam[i]) for i in range(n)]
        kk = [_mm_nt(jnp.concatenate([k_beta[i], q[i]], axis=0), k[i]) for i in range(n)]
        low = [jnp.where(strict, kk[i][:CHUNK] * dmask[i], 0.0) for i in range(n)]
        res["attn"] = [kk[i][CHUNK:] * dmask[i] for i in range(n)]
        rhs = [jnp.concatenate([v[i] * beta[i], k_beta[i] * egam[i]], axis=1).astype(BF16) for i in range(n)]
        yield
        lowb = [low[i].astype(BF16) for i in range(n)]
        pw = [_mm(lowb[i], lowb[i]).astype(BF16) for i in range(n)]
        tinv = [eye - low[i] for i in range(n)]
        for s in range(5):
            yield
            if s < 4:
                stk = [_mm(jnp.concatenate([tinv[i].astype(BF16), pw[i]], axis=0), pw[i]) for i in range(n)]
                tinv = [tinv[i] + stk[i][:CHUNK] for i in range(n)]
                pw = [stk[i][CHUNK:].astype(BF16) for i in range(n)]
            else:
                tinv = [tinv[i] + _mm(tinv[i], pw[i]) for i in range(n)]
        yield
        res["sol"] = [_mm(tinv[i], rhs[i]) for i in range(n)]
        yield
        res["q_dec"] = [q[i] * egam[i] for i in range(n)]
        gam_last = [gam[i][CHUNK - 1:CHUNK, :] for i in range(n)]
        res["k_dec"] = [k[i] * jnp.exp(gam_last[i] - gam[i]) for i in range(n)]
        res["decay"] = [jnp.exp(gam_last[i]) for i in range(n)]

    ncols = 3 * hd
    cols = [conv_piece(groups[0], col) for col in range(ncols)]
    st = [s_ref[h] for h in range(hd)]
    for gi, chunks in enumerate(groups):
        res = {}
        nxt = groups[gi + 1] if gi + 1 < len(groups) else None
        nxt_cols = []
        for _ in chain_stages(chunks, cols, res):
            if nxt is not None and len(nxt_cols) < ncols:
                nxt_cols.append(conv_piece(nxt, len(nxt_cols)))
        while nxt is not None and len(nxt_cols) < ncols:
            nxt_cols.append(conv_piece(nxt, len(nxt_cols)))
        cols = nxt_cols
        sol, attn, q_dec, k_dec, decay = res["sol"], res["attn"], res["q_dec"], res["k_dec"], res["decay"]
        for ci, c in enumerate(chunks):
            ids = [ci * hd + h for h in range(hd)]
            ws = [_mm(jnp.concatenate([sol[i][:, dv:], q_dec[i]], axis=0), st[h]) for h, i in enumerate(ids)]
            v_new = [sol[i][:, :dv] - ws[h][:CHUNK] for h, i in enumerate(ids)]
            o = [ws[h][CHUNK:] + _mm(attn[i], v_new[h]) for h, i in enumerate(ids)]
            st = [decay[i] * st[h] + _mm_tn(k_dec[i], v_new[h]) for h, i in enumerate(ids)]
            for h in range(hd):
                ms = jnp.mean(o[h] * o[h], axis=-1, keepdims=True)
                cl = slice(h * dv, (h + 1) * dv)
                o_ref[cr(c), cl] = (o[h] * lax.rsqrt(ms + EPS) * ng * _silu(og_ref[cr(c), cl].astype(F32))).astype(o_ref.dtype)
    for h in range(hd):
        s_ref[h] = st[h]


def gdn_mix(zm, conv_w_pad, alog_row, dtb_row, ng, bsz, seq, tc):
    t = zm.shape[0]
    per_b = seq // tc
    hd, dk, dv = GDN_HEADS, GDN_DK, GDN_DV
    qkv_w = 3 * hd * dk

    def zspec(width, off):
        blk = off // width
        return pl.BlockSpec((tc, width), lambda b, i: (b * per_b + i, blk))

    def full(a):
        return pl.BlockSpec(a.shape, lambda b, i: (0,) * a.ndim)

    return pl.pallas_call(
        functools.partial(_gdn_kernel, nc=tc // CHUNK),
        grid=(bsz, per_b),
        in_specs=[zspec(qkv_w, ZM_DQKV), zspec(hd * dv, ZM_DOG), zspec(LANE, ZM_DBA),
                  full(conv_w_pad), full(alog_row), full(dtb_row), full(ng)],
        out_specs=pl.BlockSpec((tc, hd * dv), lambda b, i: (b * per_b + i, 0)),
        out_shape=jax.ShapeDtypeStruct((t, hd * dv), F32),
        scratch_shapes=[pltpu.VMEM((tc + 8, qkv_w), F32), pltpu.VMEM((hd, dk, dv), F32)],
        compiler_params=_params("parallel", "arbitrary"),
        name="gdn_mix",
    )(zm, zm, zm, conv_w_pad, alog_row, dtb_row, ng)


def _merge_kernel(h3_ref, ygla_ref, ygdn_ref, yst_ref, wglu_ref,
                  wz0_ref, wz1_ref, wz2_ref, wbg_ref, wbs_ref, wbd_ref, o_ref,
                  h_ref, agla_ref, as5_ref, agdn_ref, s5t_ref, *, jt, bw):
    n_rows = jt * LANE

    @pl.when(pl.program_id(2) == 0)
    def _():
        h_ref[...] = h3_ref[...].reshape(n_rows, h3_ref.shape[-1]).astype(BF16)
        agla_ref[...] = ygla_ref[...].reshape(n_rows, bw).astype(BF16)
        agdn_ref[...] = ygdn_ref[...].reshape(n_rows, bw).astype(BF16)
        for jj in range(jt):
            ys = _gelu_tanh(yst_ref[jj].astype(F32))
            glu = _mm_tn(ys, wglu_ref[...])
            s5t_ref[jj * LANE:(jj + 1) * LANE, :] = glu[:, :bw] * _sigmoid(glu[:, bw:])
        as5_ref[...] = pltpu.einshape("(jr)d->rjd", s5t_ref[...], j=jt).reshape(n_rows, bw).astype(BF16)

    def branch(rows, wz_ref, a_ref, wb_ref):
        gate = _sigmoid(jnp.dot(h_ref[rows, :], wz_ref[...], preferred_element_type=F32))
        return gate * jnp.dot(a_ref[rows, :], wb_ref[...], preferred_element_type=F32)

    per = MERGE_ROWS // jt
    for rb in range(n_rows // MERGE_ROWS):
        rows = slice(rb * MERGE_ROWS, (rb + 1) * MERGE_ROWS)
        m = (branch(rows, wz0_ref, agla_ref, wbg_ref) + branch(rows, wz1_ref, as5_ref, wbs_ref)
             + branch(rows, wz2_ref, agdn_ref, wbd_ref))
        o_ref[rb * per:(rb + 1) * per] = m.reshape(per, jt, m.shape[-1])


def merge_gate(h3, ygla3, ygdn3, yst, wglu, wzg, wbg, wbs, wbd, layer, jt, tn):
    r, _, d = h3.shape
    bw = ygla3.shape[2]
    ncol = d // tn
    rows = jt * LANE

    def view(width):
        return pl.BlockSpec((LANE, jt, width), lambda i, j, n: (i, j, 0))

    def wz(b):
        return pl.BlockSpec((None, d, tn), lambda i, j, n: (layer, 0, b * ncol + n))

    wb = pl.BlockSpec((None, bw, tn), lambda i, j, n: (layer, 0, n))
    return pl.pallas_call(
        functools.partial(_merge_kernel, jt=jt, bw=bw),
        grid=(r // LANE, CHUNK // jt, ncol),
        in_specs=[view(d), view(bw), view(bw),
                  pl.BlockSpec((jt, yst.shape[1], LANE), lambda i, j, n: (j, 0, i)),
                  pl.BlockSpec((None,) + wglu.shape[1:], lambda i, j, n: (layer, 0, 0)),
                  wz(0), wz(1), wz(2), wb, wb, wb],
        out_specs=pl.BlockSpec((LANE, jt, tn), lambda i, j, n: (i, j, n)),
        out_shape=jax.ShapeDtypeStruct((r, CHUNK, d), F32),
        scratch_shapes=[pltpu.VMEM((rows, d), BF16), pltpu.VMEM((rows, bw), BF16),
                        pltpu.VMEM((rows, bw), BF16), pltpu.VMEM((rows, bw), BF16),
                        pltpu.VMEM((rows, bw), F32)],
        compiler_params=_params("parallel", "parallel", "arbitrary"),
        name="merge_gate",
    )(h3, ygla3, ygdn3, yst, wglu, wzg, wzg, wzg, wbg, wbs, wbd)


def _ffn_kernel(x_ref, m_ref, wout_ref, gt1_ref, g_ref, sc_ref, sh_ref, gt2_ref, wa_ref, wb_ref, wo_ref,
                fg_ref, o_ref, x1_ref, h_ref, acc_ref, *, final_norm):
    j = pl.program_id(1)

    @pl.when(j == 0)
    def _():
        x1 = x_ref[...] + gt1_ref[...] * _mm(m_ref[...], wout_ref[...])
        x1_ref[...] = x1
        h_ref[...] = _norm_mod(x1, g_ref[...], sc_ref[...], sh_ref[...]).astype(BF16)
        acc_ref[...] = jnp.zeros_like(acc_ref)

    h = h_ref[...]
    a = jnp.dot(h, wa_ref[...], preferred_element_type=F32)
    b = jnp.dot(h, wb_ref[...], preferred_element_type=F32)
    acc_ref[...] += _mm(_silu(a) * b, wo_ref[...])

    @pl.when(j == pl.num_programs(1) - 1)
    def _():
        y = x1_ref[...] + gt2_ref[...] * acc_ref[...]
        if final_norm:
            ms = jnp.mean(y * y, axis=-1, keepdims=True)
            y = y * lax.rsqrt(ms + EPS) * fg_ref[...]
        o_ref[...] = y


def ffn_block(x2, merged, w_o, gt1, g, sc, sh, gt2, w_in, w_out, final_g, layer, seq, tm, tf, final_norm):
    t, d = x2.shape
    dff = w_out.shape[1]
    nf = dff // tf
    per_b = seq // tm

    def bspec():
        return pl.BlockSpec((None, 1, d), lambda i, j: (i // per_b, 0, 0))

    def const(a):
        return pl.BlockSpec(a.shape, lambda i, j: (0,) * a.ndim)

    row = pl.BlockSpec((tm, d), lambda i, j: (i, 0))
    return pl.pallas_call(
        functools.partial(_ffn_kernel, final_norm=final_norm),
        grid=(t // tm, nf),
        in_specs=[row, row, pl.BlockSpec((None, d, d), lambda i, j: (layer, 0, 0)),
                  bspec(), const(g), bspec(), bspec(), bspec(),
                  pl.BlockSpec((None, d, tf), lambda i, j: (layer, 0, j)),
                  pl.BlockSpec((None, d, tf), lambda i, j: (layer, 0, nf + j)),
                  pl.BlockSpec((None, tf, d), lambda i, j: (layer, j, 0)),
                  const(final_g)],
        out_specs=row,
        out_shape=jax.ShapeDtypeStruct((t, d), F32),
        scratch_shapes=[pltpu.VMEM((tm, d), F32), pltpu.VMEM((tm, d), BF16), pltpu.VMEM((tm, d), F32)],
        compiler_params=_params("parallel", "arbitrary"),
        name="ffn_block",
    )(x2, merged, w_o, gt1, g, sc, sh, gt2, w_in, w_in, w_out, final_g)


def _wprep_kernel(w_ref, zm_ref, zg_ref, s5t_ref, *, segs, pads, zg_off, s5_off):
    for dst, width in pads:
        zm_ref[0, :, dst:dst + width] = jnp.zeros((zm_ref.shape[1], width), BF16)
    for src, width, dst in segs:
        zm_ref[0, :, dst:dst + width] = w_ref[0, :, src:src + width].astype(BF16)
    zg_ref[0] = w_ref[0, :, zg_off:zg_off + zg_ref.shape[2]].astype(BF16)
    s5t_ref[0] = w_ref[0, :, s5_off:s5_off + s5t_ref.shape[1]].astype(F32).T.astype(BF16)


def prep_in_weights(w_in, segs, pads, zg_off, zg_w, s5_off, s5_w, tr):
    depth, d, d_in = w_in.shape
    return pl.pallas_call(
        functools.partial(_wprep_kernel, segs=segs, pads=pads, zg_off=zg_off, s5_off=s5_off),
        grid=(depth, d // tr),
        in_specs=[pl.BlockSpec((1, tr, d_in), lambda i, j: (i, j, 0))],
        out_specs=[pl.BlockSpec((1, tr, ZM_WIDTH), lambda i, j: (i, j, 0)),
                   pl.BlockSpec((1, tr, zg_w), lambda i, j: (i, j, 0)),
                   pl.BlockSpec((1, s5_w, tr), lambda i, j: (i, 0, j))],
        out_shape=[jax.ShapeDtypeStruct((depth, d, ZM_WIDTH), BF16),
                   jax.ShapeDtypeStruct((depth, d, zg_w), BF16),
                   jax.ShapeDtypeStruct((depth, s5_w, d), BF16)],
        compiler_params=_params("parallel", "parallel"),
        name="prep_in_weights",
    )(w_in)


def _forward(x, c, w_ada, b_ada, norm1_g, w_in, gla_w_lr, gla_b_lr, gla_norm_g,
             s5_lambda_re, s5_lambda_im, s5_log_dt, s5_b_re, s5_b_im, s5_c_re, s5_c_im,
             s5_d, s5_w_glu, gdn_conv_w, gdn_a_log, gdn_dt_bias, gdn_norm_g,
             w_branch_gla, w_branch_s5, w_branch_gdn, w_out, norm2_g, w_ffn_in, w_ffn_out,
             final_g):
    bsz, seq, d = x.shape
    depth = w_ada.shape[0]
    t = bsz * seq
    nchunk = seq // CHUNK
    r = bsz * nchunk
    gqk, gw = GLA_HEADS * GLA_DK, GLA_HEADS * GLA_DV
    s5w = S5_GROUPS * S5_GROUP_CH
    dqkv, dw = 3 * GDN_HEADS * GDN_DK, GDN_HEADS * GDN_DV

    mod = ada_modulation(c, w_ada, b_ada).reshape(depth, bsz, 6, 1, d)
    s5m, s5et, s5ft, s5ac = s5_params(s5_lambda_re, s5_lambda_im, s5_log_dt, s5_b_re, s5_b_im,
                                      s5_c_re, s5_c_im, s5_d)

    o_gq, o_gk, o_gv = 0, gqk, 2 * gqk
    o_glr = o_gv + gw
    o_gog = o_glr + GLA_LOWRANK
    o_s5 = o_gog + gw
    o_dqkv = o_s5 + s5w
    o_dbeta = o_dqkv + dqkv
    o_da = o_dbeta + GDN_HEADS
    o_dog = o_da + GDN_HEADS
    o_zg = o_dog + dw

    segs = ((o_gq, gqk, ZM_GQ), (o_gk, gqk, ZM_GK), (o_gv, gw, ZM_GV), (o_gog, gw, ZM_GOG),
            (o_dqkv, dqkv, ZM_DQKV), (o_dog, dw, ZM_DOG), (o_glr, GLA_LOWRANK, ZM_GLR),
            (o_dbeta, 2 * GDN_HEADS, ZM_DBA))
    pads = ((ZM_GLR, LANE), (ZM_DBA, LANE))
    w_in_b = jnp.pad(w_in.astype(BF16), ((0, 0), (0, 0), (0, -w_in.shape[2] % LANE)))
    w_zm_all, w_zg_all, w_s5t_all = prep_in_weights(w_in_b, segs, pads, o_zg, N_BRANCH * d, o_s5, s5w, tr=256)
    s5_w_glu, w_branch_gla, w_branch_s5, w_branch_gdn, w_out, w_ffn_in, w_ffn_out = [
        a.astype(BF16) for a in (s5_w_glu, w_branch_gla, w_branch_s5, w_branch_gdn, w_out, w_ffn_in, w_ffn_out)]

    x2 = x.reshape(t, d)
    for i in range(depth):

        sh1, sc1, gt1, sh2, sc2, gt2 = [mod[i, :, k] for k in range(6)]
        g1 = norm1_g[i].reshape(1, d)

        zm, hf = norm_mod_matmul(x2, g1, sc1, sh1, w_zm_all, i, seq, tm=512, tn=3840)
        h3 = hf.reshape(r, CHUNK, d)
        ut = s5_inproj(h3, w_s5t_all, i, jt=8)

        wlr_pad = jnp.pad(gla_w_lr[i], ((0, LANE - GLA_LOWRANK), (0, 0)))
        y_gla = gla_mix(zm, wlr_pad, gla_b_lr[i].reshape(1, gqk), gla_norm_g[i].reshape(1, GLA_DV),
                        bsz, seq, tc=256)
        yst = s5_mix(ut, s5m, s5et, s5ft, s5ac, i, nchunk, gsub=4)
        conv_pad = jnp.broadcast_to(gdn_conv_w[i][:, None, :], (GDN_CONV, 8, dqkv))
        alog_row = jnp.pad(gdn_a_log[i], (GDN_HEADS, LANE - 2 * GDN_HEADS)).reshape(1, LANE)
        dtb_row = jnp.pad(gdn_dt_bias[i], (GDN_HEADS, LANE - 2 * GDN_HEADS)).reshape(1, LANE)
        y_gdn = gdn_mix(zm, conv_pad, alog_row, dtb_row, gdn_norm_g[i].reshape(1, GDN_DV),
                        bsz, seq, tc=256)

        merged = merge_gate(h3, y_gla.reshape(r, CHUNK, gw), y_gdn.reshape(r, CHUNK, dw),
                            yst, s5_w_glu, w_zg_all, w_branch_gla, w_branch_s5, w_branch_gdn, i,
                            jt=8, tn=256)
        x2 = ffn_block(x2, merged.reshape(t, d), w_out, gt1,
                       norm2_g[i].reshape(1, d), sc2, sh2, gt2,
                       w_ffn_in, w_ffn_out,
                       final_g.reshape(1, d), i, seq, tm=1024, tf=256, final_norm=(i == depth - 1))
    return x2.reshape(bsz, seq, d)


def kernel(x, c, w_ada, b_ada, norm1_g, w_in, gla_w_lr, gla_b_lr, gla_norm_g, s5_lambda_re, s5_lambda_im, s5_log_dt, s5_b_re, s5_b_im, s5_c_re, s5_c_im, s5_d, s5_w_glu, gdn_conv_w, gdn_a_log, gdn_dt_bias, gdn_norm_g, w_branch_gla, w_branch_s5, w_branch_gdn, w_out, norm2_g, w_ffn_in, w_ffn_out, final_g):
    return _forward(x, c, w_ada, b_ada, norm1_g, w_in, gla_w_lr, gla_b_lr, gla_norm_g,
                    s5_lambda_re, s5_lambda_im, s5_log_dt, s5_b_re, s5_b_im, s5_c_re, s5_c_im,
                    s5_d, s5_w_glu, gdn_conv_w, gdn_a_log, gdn_dt_bias, gdn_norm_g,
                    w_branch_gla, w_branch_s5, w_branch_gdn, w_out, norm2_g, w_ffn_in, w_ffn_out,
                    final_g)
```

```python
import functools
import math

import jax
import jax.numpy as jnp
from jax import lax
from jax.experimental import pallas as pl
from jax.experimental.pallas import tpu as pltpu

F32 = jnp.float32
BF16 = jnp.bfloat16
HI = lax.Precision.HIGHEST

EPS = 1e-6
CHUNK = 64
LANE = 128
VMEM_LIMIT = 56 * 1024 * 1024

GLA_HEADS, GLA_DK, GLA_DV, GLA_LOWRANK = 4, 64, 128, 16
GLA_GATE_NORM = 16.0
S5_GROUPS, S5_GROUP_CH, S5_STATE = 32, 16, 64
GDN_HEADS, GDN_DK, GDN_DV, GDN_CONV = 4, 128, 128, 4
N_BRANCH = 3
MERGE_ROWS = 512
FFN_PROLOGUE_ROWS = 256
GDN_GROUP_CHUNKS = 4

ZM_GQ, ZM_GK, ZM_GV, ZM_GOG = 0, 256, 512, 1024
ZM_DQKV, ZM_DOG, ZM_GLR, ZM_DBA = 1536, 3072, 3584, 3712
ZM_WIDTH = 3840


def _mm(a, b):
    return jnp.dot(a.astype(BF16), b.astype(BF16), preferred_element_type=F32)


def _mm_nt(a, b):
    return lax.dot_general(a.astype(BF16), b.astype(BF16), (((1,), (1,)), ((), ())),
                           preferred_element_type=F32)


def _mm_tn(a, b):
    return lax.dot_general(a.astype(BF16), b.astype(BF16), (((0,), (0,)), ((), ())),
                           preferred_element_type=F32)


def _mm_hi(a, b):
    return jnp.dot(a, b, precision=HI, preferred_element_type=F32)


def _split3(x):
    hi = x.astype(BF16)
    r1 = x - hi.astype(F32)
    mid = r1.astype(BF16)
    return hi, mid, (r1 - mid.astype(F32)).astype(BF16)


def _select_rows(m01, x):
    return sum(jnp.dot(m01, t, preferred_element_type=F32) for t in _split3(x))


def _select_cols(x, m01):
    return sum(jnp.dot(t, m01, preferred_element_type=F32) for t in _split3(x))


def _sigmoid(x):
    return 0.5 + 0.5 * jnp.tanh(0.5 * x)


def _silu(x):
    h = 0.5 * x
    return h + h * jnp.tanh(h)


def _softplus(x):
    return jnp.maximum(x, 0.0) + jnp.log(1.0 + jnp.exp(-jnp.abs(x)))


def _log_sigmoid(x):
    return -_softplus(-x)


def _gelu_tanh(x):
    c = math.sqrt(2.0 / math.pi)
    return 0.5 * x * (1.0 + jnp.tanh(c * (x + 0.044715 * (x * x * x))))


def _norm_mod(x, g, sc, sh):
    ms = jnp.mean(x * x, axis=-1, keepdims=True)
    return (x * lax.rsqrt(ms + EPS) * g) * (1.0 + sc) + sh


def _tok_major(ref):
    return pltpu.einshape("rjd->(jr)d", ref[...])


def _params(*sem):
    return pltpu.CompilerParams(dimension_semantics=sem, vmem_limit_bytes=VMEM_LIMIT)


def _ada_kernel(c_ref, w_ref, b_ref, o_ref):
    c = c_ref[...]
    o_ref[0] = _mm(_silu(c), w_ref[0]) + b_ref[0]


def ada_modulation(c, w_ada, b_ada):
    depth, d, d6 = w_ada.shape
    bsz = c.shape[0]
    rows = -(-bsz // 8) * 8
    c_pad = jnp.pad(c, ((0, rows - bsz), (0, 0)))
    out = pl.pallas_call(
        _ada_kernel,
        grid=(depth, d6 // d),
        in_specs=[pl.BlockSpec((rows, d), lambda i, j: (0, 0)),
                  pl.BlockSpec((1, d, d), lambda i, j: (i, 0, j)),
                  pl.BlockSpec((1, 1, d), lambda i, j: (i, 0, j))],
        out_specs=pl.BlockSpec((1, rows, d), lambda i, j: (i, 0, j)),
        out_shape=jax.ShapeDtypeStruct((depth, rows, d6), F32),
        compiler_params=_params("parallel", "parallel"),
        name="ada_modulation",
    )(c_pad, w_ada, b_ada.reshape(depth, 1, d6))
    return out[:, :bsz]


def _inproj_kernel(x_ref, g_ref, sc_ref, sh_ref, w_ref, o_ref, hf_ref, h_ref):
    @pl.when(pl.program_id(1) == 0)
    def _():
        h = _norm_mod(x_ref[...], g_ref[...], sc_ref[...], sh_ref[...])
        hf_ref[...] = h
        h_ref[...] = h.astype(BF16)

    o_ref[...] = jnp.dot(h_ref[...], w_ref[...], preferred_element_type=F32).astype(o_ref.dtype)


def norm_mod_matmul(x2, g, sc, sh, w, layer, seq, tm, tn):
    t, d = x2.shape
    c = w.shape[2]
    per_b = seq // tm
    return pl.pallas_call(
        _inproj_kernel,
        grid=(t // tm, c // tn),
        in_specs=[pl.BlockSpec((tm, d), lambda i, j: (i, 0)),
                  pl.BlockSpec((1, d), lambda i, j: (0, 0)),
                  pl.BlockSpec((None, 1, d), lambda i, j: (i // per_b, 0, 0)),
                  pl.BlockSpec((None, 1, d), lambda i, j: (i // per_b, 0, 0)),
                  pl.BlockSpec((None, d, tn), lambda i, j: (layer, 0, j))],
        out_specs=[pl.BlockSpec((tm, tn), lambda i, j: (i, j)), pl.BlockSpec((tm, d), lambda i, j: (i, 0))],
        out_shape=[jax.ShapeDtypeStruct((t, c), BF16), jax.ShapeDtypeStruct((t, d), F32)],
        scratch_shapes=[pltpu.VMEM((tm, d), BF16)],
        compiler_params=_params("parallel", "arbitrary"),
        name="norm_mod_matmul",
    )(x2, g, sc, sh, w)


def _s5_inproj_kernel(h_ref, wt_ref, o_ref, *, jt):
    h = _tok_major(h_ref).astype(BF16)
    ut = lax.dot_general(wt_ref[...], h, (((1,), (1,)), ((), ())), preferred_element_type=F32)
    for jj in range(jt):
        o_ref[jj] = ut[:, jj * LANE:(jj + 1) * LANE].astype(BF16)


def s5_inproj(h3, wt, layer, jt):
    r, _, d = h3.shape
    s5w = wt.shape[1]
    return pl.pallas_call(
        functools.partial(_s5_inproj_kernel, jt=jt),
        grid=(r // LANE, CHUNK // jt),
        in_specs=[pl.BlockSpec((LANE, jt, d), lambda i, j: (i, j, 0)),
                  pl.BlockSpec((None, s5w, d), lambda i, j: (layer, 0, 0))],
        out_specs=pl.BlockSpec((jt, s5w, LANE), lambda i, j: (j, 0, i)),
        out_shape=jax.ShapeDtypeStruct((CHUNK, s5w, r), BF16),
        compiler_params=_params("parallel", "parallel"),
        name="s5_inproj",
    )(h3, wt)


S5_CW = CHUNK * S5_GROUP_CH


def _cpow(lr_dt, li_dt, e):
    mag = jnp.exp(lr_dt * e)
    ang = li_dt * e
    return mag * jnp.cos(ang), mag * jnp.sin(ang)


def _s5_param_kernel(ldt_ref, lrc_ref, lic_ref, lrr_ref, lir_ref, bre_ref, bim_ref,
                     c1_ref, c2_ref, dcol_ref, m_ref, et_ref, ft_ref, ac_ref):
    p, cw, h = S5_STATE, S5_CW, S5_GROUP_CH
    dt = jnp.exp(ldt_ref[0])
    lrc, lic = lrc_ref[0], lic_ref[0]
    ab_re, ab_im = _cpow(lrc * dt, lic * dt, 1.0)
    den = lrc * lrc + lic * lic
    nr, ni = ab_re - 1.0, ab_im
    w_re = (nr * lrc + ni * lic) / den
    w_im = (ni * lrc - nr * lic) / den
    trow = lax.broadcasted_iota(jnp.int32, (h, cw), 0)
    tcol = lax.broadcasted_iota(jnp.int32, (h, cw), 1)
    tile_l = (trow == tcol % h).astype(BF16)
    bre, bim = _select_cols(bre_ref[0], tile_l), _select_cols(bim_ref[0], tile_l)
    bb_re = w_re * bre - w_im * bim
    bb_im = w_re * bim + w_im * bre
    lane = lax.broadcasted_iota(jnp.int32, (1, LANE), 1)
    e_m = jnp.where(lane < CHUNK, CHUNK - 1 - lane, 0).astype(F32)
    pd_re, pd_im = _cpow(lrc * dt, lic * dt, e_m)
    xrow = lax.broadcasted_iota(jnp.int32, (LANE, cw), 0)
    xcol = lax.broadcasted_iota(jnp.int32, (LANE, cw), 1)
    expand_l = (xrow == xcol // h).astype(BF16)
    p_re, p_im = _select_cols(pd_re, expand_l), _select_cols(pd_im, expand_l)
    e_re = p_re * bb_re - p_im * bb_im
    e_im = p_re * bb_im + p_im * bb_re
    et = jnp.concatenate([e_re, e_im], axis=0)
    et_ref[0] = et.astype(BF16)
    a_re, a_im = _cpow(lrc * dt, lic * dt, float(CHUNK))
    ac_ref[0] = jnp.concatenate([a_re, a_im], axis=0)
    sgn = jnp.where(lax.broadcasted_iota(jnp.int32, (1, 2 * p), 1) < p, 1.0, -1.0)
    krev = _mm_hi(c1_ref[0] * sgn, et)
    row = lax.broadcasted_iota(jnp.int32, (h, cw), 0)
    col = lax.broadcasted_iota(jnp.int32, (h, cw), 1)
    krev = krev + jnp.where(col == (cw - h) + row, dcol_ref[0], 0.0)
    rrev = jnp.concatenate([krev, jnp.zeros_like(krev)], axis=1)
    per_tile = LANE // h
    rolled = [rrev if r == 0 else pltpu.roll(rrev, 2 * cw - r * h, axis=1) for r in range(per_tile)]
    for i in range(CHUNK):
        s = (CHUNK - 1 - i) * h
        a, r = s // LANE, (s % LANE) // h
        m_ref[0, i * h:(i + 1) * h, :] = rolled[r][:, a * LANE:a * LANE + cw].astype(BF16)
    f_i = (lax.broadcasted_iota(jnp.int32, (CHUNK, 1), 0) + 1).astype(F32)
    qd_re, qd_im = _cpow(lrr_ref[0] * dt, lir_ref[0] * dt, f_i)
    ft = (c1_ref[0] * sgn)[None] * qd_re[:, None, :] - c2_ref[0][None] * qd_im[:, None, :]
    ft_ref[0] = ft.reshape(cw, 2 * p).astype(BF16)


def s5_params(lam_re, lam_im, log_dt, b_re, b_im, c_re, c_im, dpar):
    ng = lam_re.shape[0] * lam_re.shape[1]
    p, h, cw = S5_STATE, S5_GROUP_CH, S5_CW
    lam_re = lam_re.reshape(ng, p)
    lam_im = lam_im.reshape(ng, p)
    c_re = c_re.reshape(ng, h, p)
    c_im = c_im.reshape(ng, h, p)
    c1 = jnp.concatenate([c_re, c_im], axis=-1)
    c2 = jnp.concatenate([c_im, c_re], axis=-1)
    args = (log_dt.reshape(ng, 1, 1),
            lam_re.reshape(ng, p, 1), lam_im.reshape(ng, p, 1),
            jnp.tile(lam_re.reshape(ng, 1, p), (1, 1, 2)), jnp.tile(lam_im.reshape(ng, 1, p), (1, 1, 2)),
            b_re.reshape(ng, p, h), b_im.reshape(ng, p, h), c1, c2,
            dpar.reshape(ng, h, 1))

    def spec(a):
        return pl.BlockSpec((1,) + a.shape[1:], lambda i: (i, 0, 0))

    return pl.pallas_call(
        _s5_param_kernel,
        grid=(ng,),
        in_specs=[spec(a) for a in args],
        out_specs=[pl.BlockSpec((1, cw, cw), lambda i: (i, 0, 0)),
                   pl.BlockSpec((1, 2 * p, cw), lambda i: (i, 0, 0)),
                   pl.BlockSpec((1, cw, 2 * p), lambda i: (i, 0, 0)),
                   pl.BlockSpec((1, 2 * p, 1), lambda i: (i, 0, 0))],
        out_shape=[jax.ShapeDtypeStruct((ng, cw, cw), BF16),
                   jax.ShapeDtypeStruct((ng, 2 * p, cw), BF16),
                   jax.ShapeDtypeStruct((ng, cw, 2 * p), BF16),
                   jax.ShapeDtypeStruct((ng, 2 * p, 1), F32)],
        compiler_params=_params("parallel"),
        name="s5_params",
    )(*args)


def _s5_mix_kernel(u_ref, m_ref, et_ref, ft_ref, ac_ref, y_ref, *, nchunk, gsub):
    p, h = S5_STATE, S5_GROUP_CH
    r = u_ref.shape[-1]
    ks = range(gsub)
    u = [u_ref[:, k * h:(k + 1) * h, :].reshape(S5_CW, r) for k in ks]
    s = [jnp.dot(et_ref[k], u[k], preferred_element_type=F32) for k in ks]
    s_re, s_im = [s[k][:p] for k in ks], [s[k][p:] for k in ks]
    a_re, a_im = [ac_ref[k][:p] for k in ks], [ac_ref[k][p:] for k in ks]
    n_idx = lax.broadcasted_iota(jnp.int32, (1, r), 1) % nchunk
    shift = 1
    while shift < nchunk:
        keep = n_idx >= shift
        t_re = [jnp.where(keep, pltpu.roll(s_re[k], shift, axis=1), 0.0) for k in ks]
        t_im = [jnp.where(keep, pltpu.roll(s_im[k], shift, axis=1), 0.0) for k in ks]
        s_re, s_im = ([s_re[k] + a_re[k] * t_re[k] - a_im[k] * t_im[k] for k in ks],
                      [s_im[k] + a_re[k] * t_im[k] + a_im[k] * t_re[k] for k in ks])
        a_re, a_im = ([a_re[k] * a_re[k] - a_im[k] * a_im[k] for k in ks], [2.0 * a_re[k] * a_im[k] for k in ks])
        shift *= 2
    keep = n_idx >= 1
    for k in ks:
        h_prev = jnp.concatenate([jnp.where(keep, pltpu.roll(s_re[k], 1, axis=1), 0.0),
                                  jnp.where(keep, pltpu.roll(s_im[k], 1, axis=1), 0.0)], axis=0)
        y = jnp.dot(m_ref[k], u[k], preferred_element_type=F32)
        y = y + jnp.dot(ft_ref[k], h_prev.astype(BF16), preferred_element_type=F32)
        y_ref[:, k * h:(k + 1) * h, :] = y.reshape(CHUNK, h, r).astype(y_ref.dtype)


def s5_mix(ut, m, et, ft, ac, layer, nchunk, gsub):
    _, s5w, r = ut.shape
    g, h, p, cw = S5_GROUPS, S5_GROUP_CH, S5_STATE, S5_CW
    base = layer * g // gsub
    return pl.pallas_call(
        functools.partial(_s5_mix_kernel, nchunk=nchunk, gsub=gsub),
        grid=(g // gsub,),
        in_specs=[pl.BlockSpec((CHUNK, gsub * h, r), lambda i: (0, i, 0)),
                  pl.BlockSpec((gsub, cw, cw), lambda i: (base + i, 0, 0)),
                  pl.BlockSpec((gsub, 2 * p, cw), lambda i: (base + i, 0, 0)),
                  pl.BlockSpec((gsub, cw, 2 * p), lambda i: (base + i, 0, 0)),
                  pl.BlockSpec((gsub, 2 * p, 1), lambda i: (base + i, 0, 0))],
        out_specs=pl.BlockSpec((CHUNK, gsub * h, r), lambda i: (0, i, 0)),
        out_shape=jax.ShapeDtypeStruct((CHUNK, s5w, r), BF16),
        compiler_params=_params("parallel"),
        name="s5_mix",
    )(ut, m, et, ft, ac)


def _gla_kernel(q_ref, k_ref, v_ref, og_ref, lr_ref, wlr_ref, blr_ref, ng_ref, o_ref, st_ref, *, nc):
    hd, dk, dv = GLA_HEADS, GLA_DK, GLA_DV

    @pl.when(pl.program_id(1) == 0)
    def _():
        st_ref[...] = jnp.zeros_like(st_ref)

    ri = lax.broadcasted_iota(jnp.int32, (CHUNK, CHUNK), 0)
    ci = lax.broadcasted_iota(jnp.int32, (CHUNK, CHUNK), 1)
    incl = ri >= ci
    ltri = incl.astype(BF16)
    lane_k = lax.broadcasted_iota(jnp.int32, (1, hd * dk), 1)
    srow = lax.broadcasted_iota(jnp.int32, (hd * dv, hd * dk), 0)
    scol = lax.broadcasted_iota(jnp.int32, (hd * dv, hd * dk), 1)
    same_head = (srow // dv) == (scol // dk)
    wlr, blr, ng = wlr_ref[...], blr_ref[...], ng_ref[...]

    cs = range(nc)
    cr = lambda c: slice(c * CHUNK, (c + 1) * CHUNK)
    g = [_log_sigmoid(_mm(lr_ref[cr(c), :], wlr) + blr) * (1.0 / GLA_GATE_NORM) for c in cs]
    bc = [_select_rows(ltri, g[c]) for c in cs]
    bl = [bc[c][CHUNK - 1:CHUNK, :] for c in cs]
    q_e = [q_ref[cr(c), :].astype(F32) * (dk ** -0.5) * jnp.exp(bc[c]) for c in cs]
    k_e = [k_ref[cr(c), :].astype(F32) * jnp.exp(-bc[c]) for c in cs]
    k_d = [k_ref[cr(c), :].astype(F32) * jnp.exp(bl[c] - bc[c]) for c in cs]
    kv = [jnp.where(same_head, _mm_tn(v_ref[cr(c), :], k_d[c]), 0.0) for c in cs]
    sts = []
    st = st_ref[...]
    for c in cs:
        sts.append(st)
        st = jnp.exp(bl[c]) * st + kv[c]
    st_ref[...] = st
    o_inter = [_mm_nt(q_e[c], sts[c]) for c in cs]
    q_heads = [jnp.concatenate([jnp.where((lane_k // dk) == h, q_e[c], 0.0) for h in range(hd)], axis=0)
               for c in cs]
    sc_all = [_mm_nt(q_heads[c], k_e[c]) for c in cs]
    sc = [[jnp.where(incl, sc_all[c][h * CHUNK:(h + 1) * CHUNK], 0.0) for h in range(hd)] for c in cs]
    for c in cs:
        for h in range(hd):
            cols = slice(h * dv, (h + 1) * dv)
            oh = _mm(sc[c][h], v_ref[cr(c), cols]) + o_inter[c][:, cols]
            ms = jnp.mean(oh * oh, axis=-1, keepdims=True)
            o_ref[cr(c), cols] = (oh * lax.rsqrt(ms + EPS) * ng * _silu(og_ref[cr(c), cols].astype(F32))).astype(o_ref.dtype)


def gla_mix(zm, wlr_pad, blr, ng, bsz, seq, tc):
    t = zm.shape[0]
    per_b = seq // tc
    hd, dk, dv = GLA_HEADS, GLA_DK, GLA_DV

    def zspec(width, off):
        blk = off // width
        return pl.BlockSpec((tc, width), lambda b, i: (b * per_b + i, blk))

    def full(a):
        return pl.BlockSpec(a.shape, lambda b, i: (0,) * a.ndim)

    return pl.pallas_call(
        functools.partial(_gla_kernel, nc=tc // CHUNK),
        grid=(bsz, per_b),
        in_specs=[zspec(hd * dk, ZM_GQ), zspec(hd * dk, ZM_GK), zspec(hd * dv, ZM_GV),
                  zspec(hd * dv, ZM_GOG), zspec(LANE, ZM_GLR), full(wlr_pad), full(blr), full(ng)],
        out_specs=pl.BlockSpec((tc, hd * dv), lambda b, i: (b * per_b + i, 0)),
        out_shape=jax.ShapeDtypeStruct((t, hd * dv), F32),
        scratch_shapes=[pltpu.VMEM((hd * dv, hd * dk), F32)],
        compiler_params=_params("parallel", "arbitrary"),
        name="gla_mix",
    )(zm, zm, zm, zm, zm, wlr_pad, blr, ng)


def _gdn_kernel(qkv_ref, og_ref, ba_ref, cw_ref, shift_ref, alog_ref, dtb_ref, ng_ref, o_ref,
                head_ref, s_ref, *, nc):
    hd, dk, dv, kc = GDN_HEADS, GDN_DK, GDN_DV, GDN_CONV
    tc = nc * CHUNK
    pad = 8

    @pl.when(pl.program_id(1) == 0)
    def _():
        s_ref[...] = jnp.zeros_like(s_ref)
        head_ref[0:pad, :] = jnp.zeros((pad, head_ref.shape[1]), F32)

    head_ref[pad:2 * pad, :] = qkv_ref[0:2 * pad, :].astype(F32)[0:pad]
    lagged = jnp.dot(shift_ref[...], qkv_ref[...], preferred_element_type=F32)

    ri = lax.broadcasted_iota(jnp.int32, (CHUNK, CHUNK), 0)
    ci = lax.broadcasted_iota(jnp.int32, (CHUNK, CHUNK), 1)
    incl = ri >= ci
    strict = ri > ci
    ltri = incl.astype(BF16)
    eye = (ri == ci).astype(F32)
    cw = [cw_ref[i] for i in range(kc)]
    ng = ng_ref[...]

    ba = ba_ref[...].astype(F32)
    beta_all = _sigmoid(ba)
    g_all = -jnp.exp(alog_ref[...]) * _softplus(ba + dtb_ref[...])

    cr = lambda c: slice(c * CHUNK, (c + 1) * CHUNK)
    gam_c = [_select_rows(ltri, g_all[cr(c), :]) for c in range(nc)]
    gam_tc = [g.T for g in gam_c]
    groups = [list(range(g0, min(g0 + GDN_GROUP_CHUNKS, nc))) for g0 in range(0, nc, GDN_GROUP_CHUNKS)]

    def conv_piece(chunks, col):
        lanes = slice(col * dk, (col + 1) * dk)
        nr = len(chunks) * CHUNK
        assert chunks[0] == 0 and nr == tc
        blocks = lambda a: a.reshape(nr // 8, 8, dk)
        acc = cw[kc - 1][:, lanes] * blocks(qkv_ref[:, lanes].astype(F32))
        head = cw[kc - 1][:, lanes] * head_ref[pad:2 * pad, lanes]
        for i in range(kc - 1):
            lag = kc - 1 - i
            acc = acc + cw[i][:, lanes] * blocks(lagged[(lag - 1) * tc:lag * tc, lanes])
            head = head + cw[i][:, lanes] * head_ref[pad - lag:2 * pad - lag, lanes]
        acc = jnp.concatenate([head, acc.reshape(nr, dk)[pad:]], axis=0)
        y = _silu(acc)
        if col < 2 * hd:
            y = y * lax.rsqrt(jnp.sum(y * y, axis=-1, keepdims=True) + EPS)
        if col < hd:
            y = y * (dk ** -0.5)
        return [y[i * CHUNK:(i + 1) * CHUNK] for i in range(len(chunks))]

    def chain_stages(chunks, cols, res):
        pairs = [(ci, c, h) for ci, c in enumerate(chunks) for h in range(hd)]
        n = len(pairs)
        q = [cols[h][ci] for ci, c, h in pairs]
        k = [cols[hd + h][ci] for ci, c, h in pairs]
        v = [cols[2 * hd + h][ci] for ci, c, h in pairs]
        beta = [beta_all[cr(c), h:h + 1] for ci, c, h in pairs]
        gam = [gam_c[c][:, hd + h:hd + h + 1] for ci, c, h in pairs]
        gam_row = [gam_tc[c][hd + h:hd + h + 1, :] for ci, c, h in pairs]
        dmask = [jnp.where(incl, jnp.exp(gam[i] - gam_row[i]), 0.0) for i in range(n)]
        k_beta = [k[i] * beta[i] for i in range(n)]
        egam = [jnp.exp(gam[i]) for i in range(n)]
        kk = [_mm_nt(jnp.concatenate([k_beta[i], q[i]], axis=0), k[i]) for i in range(n)]
        low = [jnp.where(strict, kk[i][:CHUNK] * dmask[i], 0.0) for i in range(n)]
        res["attn"] = [kk[i][CHUNK:] * dmask[i] for i in range(n)]
        rhs = [jnp.concatenate([v[i] * beta[i], k_beta[i] * egam[i]], axis=1).astype(BF16) for i in range(n)]
        yield
        lowb = [low[i].astype(BF16) for i in range(n)]
        pw = [_mm(lowb[i], lowb[i]).astype(BF16) for i in range(n)]
        tinv = [eye - low[i] for i in range(n)]
        for s in range(5):
            yield
            if s < 4:
                stk = [_mm(jnp.concatenate([tinv[i].astype(BF16), pw[i]], axis=0), pw[i]) for i in range(n)]
                tinv = [tinv[i] + stk[i][:CHUNK] for i in range(n)]
                pw = [stk[i][CHUNK:].astype(BF16) for i in range(n)]
            else:
                tinv = [tinv[i] + _mm(tinv[i], pw[i]) for i in range(n)]
        yield
        sol = [_mm(tinv[i], rhs[i]).astype(BF16) for i in range(n)]
        q_dec = [q[i] * egam[i] for i in range(n)]
        gam_last = [gam[i][CHUNK - 1:CHUNK, :] for i in range(n)]
        k_dec = [k[i] * jnp.exp(gam_last[i] - gam[i]) for i in range(n)]
        yield
        ks = [_mm_tn(k_dec[i], sol[i]) for i in range(n)]
        aw = [_mm(res["attn"][i], sol[i]) for i in range(n)]
        res["lhs"] = [jnp.concatenate([ks[i][:, dv:], q_dec[i] - aw[i][:, dv:]], axis=0).astype(BF16)
                      for i in range(n)]
        res["s_add"] = [ks[i][:, :dv] for i in range(n)]
        res["o_add"] = [aw[i][:, :dv] for i in range(n)]
        res["decay"] = [jnp.exp(gam_last[i]) for i in range(n)]

    ncols = 3 * hd
    cols = [conv_piece(groups[0], col) for col in range(ncols)]
    st = [s_ref[h] for h in range(hd)]
    for gi, chunks in enumerate(groups):
        res = {}
        nxt = groups[gi + 1] if gi + 1 < len(groups) else None
        nxt_cols = []
        for _ in chain_stages(chunks, cols, res):
            if nxt is not None and len(nxt_cols) < ncols:
                nxt_cols.append(conv_piece(nxt, len(nxt_cols)))
        while nxt is not None and len(nxt_cols) < ncols:
            nxt_cols.append(conv_piece(nxt, len(nxt_cols)))
        cols = nxt_cols
        lhs, s_add, o_add, decay = res["lhs"], res["s_add"], res["o_add"], res["decay"]
        for ci, c in enumerate(chunks):
            ids = [ci * hd + h for h in range(hd)]
            prod = [_mm(lhs[i], st[h]) for h, i in enumerate(ids)]
            o = [prod[h][dk:] + o_add[i] for h, i in enumerate(ids)]
            st = [decay[i] * st[h] - prod[h][:dk] + s_add[i] for h, i in enumerate(ids)]
            for h in range(hd):
                ms = jnp.mean(o[h] * o[h], axis=-1, keepdims=True)
                cl = slice(h * dv, (h + 1) * dv)
                o_ref[cr(c), cl] = (o[h] * lax.rsqrt(ms + EPS) * ng * _silu(og_ref[cr(c), cl].astype(F32))).astype(o_ref.dtype)
    for h in range(hd):
        s_ref[h] = st[h]
    head_ref[0:pad, :] = qkv_ref[tc - 2 * pad:tc, :].astype(F32)[pad:]


def gdn_mix(zm, conv_w_pad, shift_m, alog_row, dtb_row, ng, bsz, seq, tc):
    t = zm.shape[0]
    per_b = seq // tc
    hd, dk, dv = GDN_HEADS, GDN_DK, GDN_DV
    qkv_w = 3 * hd * dk

    def zspec(width, off):
        blk = off // width
        return pl.BlockSpec((tc, width), lambda b, i: (b * per_b + i, blk))

    def full(a):
        return pl.BlockSpec(a.shape, lambda b, i: (0,) * a.ndim)

    return pl.pallas_call(
        functools.partial(_gdn_kernel, nc=tc // CHUNK),
        grid=(bsz, per_b),
        in_specs=[zspec(qkv_w, ZM_DQKV), zspec(hd * dv, ZM_DOG), zspec(LANE, ZM_DBA),
                  full(conv_w_pad), full(shift_m), full(alog_row), full(dtb_row), full(ng)],
        out_specs=pl.BlockSpec((tc, hd * dv), lambda b, i: (b * per_b + i, 0)),
        out_shape=jax.ShapeDtypeStruct((t, hd * dv), F32),
        scratch_shapes=[pltpu.VMEM((16, qkv_w), F32), pltpu.VMEM((hd, dk, dv), F32)],
        compiler_params=_params("parallel", "arbitrary"),
        name="gdn_mix",
    )(zm, zm, zm, conv_w_pad, shift_m, alog_row, dtb_row, ng)


def _merge_kernel(h3_ref, ygla_ref, ygdn_ref, yst_ref, wglu_ref,
                  wz0_ref, wz1_ref, wz2_ref, wbg_ref, wbs_ref, wbd_ref, o_ref,
                  h_ref, agla_ref, as5_ref, agdn_ref, s5t_ref, *, jt, bw):
    n_rows = jt * LANE

    @pl.when(pl.program_id(2) == 0)
    def _():
        h_ref[...] = h3_ref[...].reshape(n_rows, h3_ref.shape[-1]).astype(BF16)
        agla_ref[...] = ygla_ref[...].reshape(n_rows, bw).astype(BF16)
        agdn_ref[...] = ygdn_ref[...].reshape(n_rows, bw).astype(BF16)
        for jj in range(jt):
            ys = _gelu_tanh(yst_ref[jj].astype(F32))
            glu = _mm_tn(ys, wglu_ref[...])
            s5t_ref[jj * LANE:(jj + 1) * LANE, :] = glu[:, :bw] * _sigmoid(glu[:, bw:])
        as5_ref[...] = pltpu.einshape("(jr)d->rjd", s5t_ref[...], j=jt).reshape(n_rows, bw).astype(BF16)

    def branch(rows, wz_ref, a_ref, wb_ref):
        gate = _sigmoid(jnp.dot(h_ref[rows, :], wz_ref[...], preferred_element_type=F32))
        return gate * jnp.dot(a_ref[rows, :], wb_ref[...], preferred_element_type=F32)

    per = MERGE_ROWS // jt
    for rb in range(n_rows // MERGE_ROWS):
        rows = slice(rb * MERGE_ROWS, (rb + 1) * MERGE_ROWS)
        m = (branch(rows, wz0_ref, agla_ref, wbg_ref) + branch(rows, wz1_ref, as5_ref, wbs_ref)
             + branch(rows, wz2_ref, agdn_ref, wbd_ref))
        o_ref[rb * per:(rb + 1) * per] = m.reshape(per, jt, m.shape[-1])


def merge_gate(h3, ygla3, ygdn3, yst, wglu, wzg, wbg, wbs, wbd, layer, jt, tn):
    r, _, d = h3.shape
    bw = ygla3.shape[2]
    ncol = d // tn
    rows = jt * LANE

    def view(width):
        return pl.BlockSpec((LANE, jt, width), lambda i, j, n: (i, j, 0))

    def wz(b):
        return pl.BlockSpec((None, d, tn), lambda i, j, n: (layer, 0, b * ncol + n))

    wb = pl.BlockSpec((None, bw, tn), lambda i, j, n: (layer, 0, n))
    return pl.pallas_call(
        functools.partial(_merge_kernel, jt=jt, bw=bw),
        grid=(r // LANE, CHUNK // jt, ncol),
        in_specs=[view(d), view(bw), view(bw),
                  pl.BlockSpec((jt, yst.shape[1], LANE), lambda i, j, n: (j, 0, i)),
                  pl.BlockSpec((None,) + wglu.shape[1:], lambda i, j, n: (layer, 0, 0)),
                  wz(0), wz(1), wz(2), wb, wb, wb],
        out_specs=pl.BlockSpec((LANE, jt, tn), lambda i, j, n: (i, j, n)),
        out_shape=jax.ShapeDtypeStruct((r, CHUNK, d), F32),
        scratch_shapes=[pltpu.VMEM((rows, d), BF16), pltpu.VMEM((rows, bw), BF16),
                        pltpu.VMEM((rows, bw), BF16), pltpu.VMEM((rows, bw), BF16),
                        pltpu.VMEM((rows, bw), F32)],
        compiler_params=_params("parallel", "parallel", "arbitrary"),
        name="merge_gate",
    )(h3, ygla3, ygdn3, yst, wglu, wzg, wzg, wzg, wbg, wbs, wbd)


def _ffn_kernel(x_ref, m_ref, wout_ref, gt1_ref, g_ref, sc_ref, sh_ref, gt2_ref, wa_ref, wb_ref, wo_ref,
                fg_ref, o_ref, x1_ref, h_ref, acc_ref, *, final_norm):
    j = pl.program_id(1)

    @pl.when(j == 0)
    def _():
        for rb in range(x_ref.shape[0] // FFN_PROLOGUE_ROWS):
            rows = slice(rb * FFN_PROLOGUE_ROWS, (rb + 1) * FFN_PROLOGUE_ROWS)
            x1 = x_ref[rows, :] + gt1_ref[...] * _mm(m_ref[rows, :], wout_ref[...])
            x1_ref[rows, :] = x1
            h_ref[rows, :] = _norm_mod(x1, g_ref[...], sc_ref[...], sh_ref[...]).astype(BF16)
        acc_ref[...] = jnp.zeros_like(acc_ref)

    h = h_ref[...]
    a = jnp.dot(h, wa_ref[...], preferred_element_type=F32)
    b = jnp.dot(h, wb_ref[...], preferred_element_type=F32)
    acc_ref[...] += _mm(_silu(a) * b, wo_ref[...])

    @pl.when(j == pl.num_programs(1) - 1)
    def _():
        y = x1_ref[...] + gt2_ref[...] * acc_ref[...]
        if final_norm:
            ms = jnp.mean(y * y, axis=-1, keepdims=True)
            y = y * lax.rsqrt(ms + EPS) * fg_ref[...]
        o_ref[...] = y


def ffn_block(x2, merged, w_o, gt1, g, sc, sh, gt2, w_in, w_out, final_g, layer, seq, tm, tf, final_norm):
    t, d = x2.shape
    dff = w_out.shape[1]
    nf = dff // tf
    per_b = seq // tm

    def bspec():
        return pl.BlockSpec((None, 1, d), lambda i, j: (i // per_b, 0, 0))

    def const(a):
        return pl.BlockSpec(a.shape, lambda i, j: (0,) * a.ndim)

    row = pl.BlockSpec((tm, d), lambda i, j: (i, 0))
    return pl.pallas_call(
        functools.partial(_ffn_kernel, final_norm=final_norm),
        grid=(t // tm, nf),
        in_specs=[row, row, pl.BlockSpec((None, d, d), lambda i, j: (layer, 0, 0)),
                  bspec(), const(g), bspec(), bspec(), bspec(),
                  pl.BlockSpec((None, d, tf), lambda i, j: (layer, 0, j)),
                  pl.BlockSpec((None, d, tf), lambda i, j: (layer, 0, nf + j)),
                  pl.BlockSpec((None, tf, d), lambda i, j: (layer, j, 0)),
                  const(final_g)],
        out_specs=row,
        out_shape=jax.ShapeDtypeStruct((t, d), F32),
        scratch_shapes=[pltpu.VMEM((tm, d), F32), pltpu.VMEM((tm, d), BF16), pltpu.VMEM((tm, d), F32)],
        compiler_params=_params("parallel", "arbitrary"),
        name="ffn_block",
    )(x2, merged, w_o, gt1, g, sc, sh, gt2, w_in, w_in, w_out, final_g)


def _wprep_kernel(w_ref, zm_ref, zg_ref, s5t_ref, *, segs, pads, zg_off, s5_off):
    for dst, width in pads:
        zm_ref[0, :, dst:dst + width] = jnp.zeros((zm_ref.shape[1], width), BF16)
    for src, width, dst in segs:
        zm_ref[0, :, dst:dst + width] = w_ref[0, :, src:src + width].astype(BF16)
    zg_ref[0] = w_ref[0, :, zg_off:zg_off + zg_ref.shape[2]].astype(BF16)
    s5t_ref[0] = w_ref[0, :, s5_off:s5_off + s5t_ref.shape[1]].astype(F32).T.astype(BF16)


def prep_in_weights(w_in, segs, pads, zg_off, zg_w, s5_off, s5_w, tr):
    depth, d, d_in = w_in.shape
    return pl.pallas_call(
        functools.partial(_wprep_kernel, segs=segs, pads=pads, zg_off=zg_off, s5_off=s5_off),
        grid=(depth, d // tr),
        in_specs=[pl.BlockSpec((1, tr, d_in), lambda i, j: (i, j, 0))],
        out_specs=[pl.BlockSpec((1, tr, ZM_WIDTH), lambda i, j: (i, j, 0)),
                   pl.BlockSpec((1, tr, zg_w), lambda i, j: (i, j, 0)),
                   pl.BlockSpec((1, s5_w, tr), lambda i, j: (i, 0, j))],
        out_shape=[jax.ShapeDtypeStruct((depth, d, ZM_WIDTH), BF16),
                   jax.ShapeDtypeStruct((depth, d, zg_w), BF16),
                   jax.ShapeDtypeStruct((depth, s5_w, d), BF16)],
        compiler_params=_params("parallel", "parallel"),
        name="prep_in_weights",
    )(w_in)


def _forward(x, c, w_ada, b_ada, norm1_g, w_in, gla_w_lr, gla_b_lr, gla_norm_g,
             s5_lambda_re, s5_lambda_im, s5_log_dt, s5_b_re, s5_b_im, s5_c_re, s5_c_im,
             s5_d, s5_w_glu, gdn_conv_w, gdn_a_log, gdn_dt_bias, gdn_norm_g,
             w_branch_gla, w_branch_s5, w_branch_gdn, w_out, norm2_g, w_ffn_in, w_ffn_out,
             final_g):
    bsz, seq, d = x.shape
    depth = w_ada.shape[0]
    t = bsz * seq
    nchunk = seq // CHUNK
    r = bsz * nchunk
    gqk, gw = GLA_HEADS * GLA_DK, GLA_HEADS * GLA_DV
    s5w = S5_GROUPS * S5_GROUP_CH
    dqkv, dw = 3 * GDN_HEADS * GDN_DK, GDN_HEADS * GDN_DV

    mod = ada_modulation(c, w_ada, b_ada).reshape(depth, bsz, 6, 1, d)
    s5m, s5et, s5ft, s5ac = s5_params(s5_lambda_re, s5_lambda_im, s5_log_dt, s5_b_re, s5_b_im,
                                      s5_c_re, s5_c_im, s5_d)

    o_gq, o_gk, o_gv = 0, gqk, 2 * gqk
    o_glr = o_gv + gw
    o_gog = o_glr + GLA_LOWRANK
    o_s5 = o_gog + gw
    o_dqkv = o_s5 + s5w
    o_dbeta = o_dqkv + dqkv
    o_da = o_dbeta + GDN_HEADS
    o_dog = o_da + GDN_HEADS
    o_zg = o_dog + dw

    segs = ((o_gq, gqk, ZM_GQ), (o_gk, gqk, ZM_GK), (o_gv, gw, ZM_GV), (o_gog, gw, ZM_GOG),
            (o_dqkv, dqkv, ZM_DQKV), (o_dog, dw, ZM_DOG), (o_glr, GLA_LOWRANK, ZM_GLR),
            (o_dbeta, 2 * GDN_HEADS, ZM_DBA))
    pads = ((ZM_GLR, LANE), (ZM_DBA, LANE))
    w_in_b = jnp.pad(w_in.astype(BF16), ((0, 0), (0, 0), (0, -w_in.shape[2] % LANE)))
    w_zm_all, w_zg_all, w_s5t_all = prep_in_weights(w_in_b, segs, pads, o_zg, N_BRANCH * d, o_s5, s5w, tr=256)
    s5_w_glu, w_branch_gla, w_branch_s5, w_branch_gdn, w_out, w_ffn_in, w_ffn_out = [
        a.astype(BF16) for a in (s5_w_glu, w_branch_gla, w_branch_s5, w_branch_gdn, w_out, w_ffn_in, w_ffn_out)]

    gdn_tc = 256
    lag_row = jnp.arange((GDN_CONV - 1) * gdn_tc)
    gdn_shift = (jnp.arange(gdn_tc)[None, :]
                 == (lag_row % gdn_tc - lag_row // gdn_tc - 1)[:, None]).astype(BF16)

    x2 = x.reshape(t, d)
    for i in range(depth):

        sh1, sc1, gt1, sh2, sc2, gt2 = [mod[i, :, k] for k in range(6)]
        g1 = norm1_g[i].reshape(1, d)

        zm, hf = norm_mod_matmul(x2, g1, sc1, sh1, w_zm_all, i, seq, tm=512, tn=3840)
        h3 = hf.reshape(r, CHUNK, d)
        ut = s5_inproj(h3, w_s5t_all, i, jt=8)

        wlr_pad = jnp.pad(gla_w_lr[i], ((0, LANE - GLA_LOWRANK), (0, 0)))
        y_gla = gla_mix(zm, wlr_pad, gla_b_lr[i].reshape(1, gqk), gla_norm_g[i].reshape(1, GLA_DV),
                        bsz, seq, tc=256)
        yst = s5_mix(ut, s5m, s5et, s5ft, s5ac, i, nchunk, gsub=4)
        conv_pad = jnp.broadcast_to(gdn_conv_w[i][:, None, :], (GDN_CONV, 8, dqkv))
        alog_row = jnp.pad(gdn_a_log[i], (GDN_HEADS, LANE - 2 * GDN_HEADS)).reshape(1, LANE)
        dtb_row = jnp.pad(gdn_dt_bias[i], (GDN_HEADS, LANE - 2 * GDN_HEADS)).reshape(1, LANE)
        y_gdn = gdn_mix(zm, conv_pad, gdn_shift, alog_row, dtb_row, gdn_norm_g[i].reshape(1, GDN_DV),
                        bsz, seq, tc=gdn_tc)

        merged = merge_gate(h3, y_gla.reshape(r, CHUNK, gw), y_gdn.reshape(r, CHUNK, dw),
                            yst, s5_w_glu, w_zg_all, w_branch_gla, w_branch_s5, w_branch_gdn, i,
                            jt=8, tn=256)
        x2 = ffn_block(x2, merged.reshape(t, d), w_out, gt1,
                       norm2_g[i].reshape(1, d), sc2, sh2, gt2,
                       w_ffn_in, w_ffn_out,
                       final_g.reshape(1, d), i, seq, tm=1024, tf=256, final_norm=(i == depth - 1))
    return x2.reshape(bsz, seq, d)


def kernel(x, c, w_ada, b_ada, norm1_g, w_in, gla_w_lr, gla_b_lr, gla_norm_g, s5_lambda_re, s5_lambda_im, s5_log_dt, s5_b_re, s5_b_im, s5_c_re, s5_c_im, s5_d, s5_w_glu, gdn_conv_w, gdn_a_log, gdn_dt_bias, gdn_norm_g, w_branch_gla, w_branch_s5, w_branch_gdn, w_out, norm2_g, w_ffn_in, w_ffn_out, final_g):
    return _forward(x, c, w_ada, b_ada, norm1_g, w_in, gla_w_lr, gla_b_lr, gla_norm_g,
                    s5_lambda_re, s5_lambda_im, s5_log_dt, s5_b_re, s5_b_im, s5_c_re, s5_c_im,
                    s5_d, s5_w_glu, gdn_conv_w, gdn_a_log, gdn_dt_bias, gdn_norm_g,
                    w_branch_gla, w_branch_s5, w_branch_gdn, w_out, norm2_g, w_ffn_in, w_ffn_out,
                    final_g)
```

```python
import functools
import math

import jax
import jax.numpy as jnp
from jax import lax
from jax.experimental import pallas as pl
from jax.experimental.pallas import tpu as pltpu

F32 = jnp.float32
BF16 = jnp.bfloat16
HI = lax.Precision.HIGHEST

EPS = 1e-6
CHUNK = 64
LANE = 128
VMEM_LIMIT = 56 * 1024 * 1024

GLA_HEADS, GLA_DK, GLA_DV, GLA_LOWRANK = 4, 64, 128, 16
GLA_GATE_NORM = 16.0
S5_GROUPS, S5_GROUP_CH, S5_STATE = 32, 16, 64
GDN_HEADS, GDN_DK, GDN_DV, GDN_CONV = 4, 128, 128, 4
N_BRANCH = 3
GDN_GROUP_CHUNKS = 4

ZM_GQ, ZM_GK, ZM_GV, ZM_GOG = 0, 256, 512, 1024
ZM_DQKV, ZM_DOG, ZM_GLR, ZM_DBA = 1536, 3072, 3584, 3712
ZM_WIDTH = 3840


def _mm(a, b):
    return jnp.dot(a.astype(BF16), b.astype(BF16), preferred_element_type=F32)


def _mm_nt(a, b):
    return lax.dot_general(a.astype(BF16), b.astype(BF16), (((1,), (1,)), ((), ())),
                           preferred_element_type=F32)


def _mm_tn(a, b):
    return lax.dot_general(a.astype(BF16), b.astype(BF16), (((0,), (0,)), ((), ())),
                           preferred_element_type=F32)


def _mm_hi(a, b):
    return jnp.dot(a, b, precision=HI, preferred_element_type=F32)


def _split3(x):
    hi = x.astype(BF16)
    r1 = x - hi.astype(F32)
    mid = r1.astype(BF16)
    return hi, mid, (r1 - mid.astype(F32)).astype(BF16)


def _select_rows(m01, x):
    return sum(jnp.dot(m01, t, preferred_element_type=F32) for t in _split3(x))


def _select_cols(x, m01):
    return sum(jnp.dot(t, m01, preferred_element_type=F32) for t in _split3(x))


def _sigmoid(x):
    return 0.5 + 0.5 * jnp.tanh(0.5 * x)


def _silu(x):
    h = 0.5 * x
    return h + h * jnp.tanh(h)


def _softplus(x):
    return jnp.maximum(x, 0.0) + jnp.log(1.0 + jnp.exp(-jnp.abs(x)))


def _log_sigmoid(x):
    return -_softplus(-x)


def _gelu_tanh(x):
    c = math.sqrt(2.0 / math.pi)
    return 0.5 * x * (1.0 + jnp.tanh(c * (x + 0.044715 * (x * x * x))))


def _norm_mod(x, g, sc, sh):
    ms = jnp.mean(x * x, axis=-1, keepdims=True)
    return (x * lax.rsqrt(ms + EPS) * g) * (1.0 + sc) + sh


def _tok_major(ref):
    return pltpu.einshape("rjd->(jr)d", ref[...])


def _params(*sem):
    return pltpu.CompilerParams(dimension_semantics=sem, vmem_limit_bytes=VMEM_LIMIT)


def _ada_kernel(c_ref, w_ref, b_ref, o_ref):
    c = c_ref[...]
    o_ref[0] = _mm(_silu(c), w_ref[0]) + b_ref[0]


def ada_modulation(c, w_ada, b_ada):
    depth, d, d6 = w_ada.shape
    bsz = c.shape[0]
    rows = -(-bsz // 8) * 8
    c_pad = jnp.pad(c, ((0, rows - bsz), (0, 0)))
    out = pl.pallas_call(
        _ada_kernel,
        grid=(depth, d6 // d),
        in_specs=[pl.BlockSpec((rows, d), lambda i, j: (0, 0)),
                  pl.BlockSpec((1, d, d), lambda i, j: (i, 0, j)),
                  pl.BlockSpec((1, 1, d), lambda i, j: (i, 0, j))],
        out_specs=pl.BlockSpec((1, rows, d), lambda i, j: (i, 0, j)),
        out_shape=jax.ShapeDtypeStruct((depth, rows, d6), F32),
        compiler_params=_params("parallel", "parallel"),
        name="ada_modulation",
    )(c_pad, w_ada, b_ada.reshape(depth, 1, d6))
    return out[:, :bsz]


def _inproj_kernel(x_ref, g_ref, sc_ref, sh_ref, w_ref, o_ref, hf_ref, h_ref):
    @pl.when(pl.program_id(1) == 0)
    def _():
        h = _norm_mod(x_ref[...], g_ref[...], sc_ref[...], sh_ref[...])
        hf_ref[...] = h
        h_ref[...] = h.astype(BF16)

    o_ref[...] = jnp.dot(h_ref[...], w_ref[...], preferred_element_type=F32).astype(o_ref.dtype)


def norm_mod_matmul(x2, g, sc, sh, w, layer, seq, tm, tn):
    t, d = x2.shape
    c = w.shape[2]
    per_b = seq // tm
    return pl.pallas_call(
        _inproj_kernel,
        grid=(t // tm, c // tn),
        in_specs=[pl.BlockSpec((tm, d), lambda i, j: (i, 0)),
                  pl.BlockSpec((1, d), lambda i, j: (0, 0)),
                  pl.BlockSpec((None, 1, d), lambda i, j: (i // per_b, 0, 0)),
                  pl.BlockSpec((None, 1, d), lambda i, j: (i // per_b, 0, 0)),
                  pl.BlockSpec((None, d, tn), lambda i, j: (layer, 0, j))],
        out_specs=[pl.BlockSpec((tm, tn), lambda i, j: (i, j)), pl.BlockSpec((tm, d), lambda i, j: (i, 0))],
        out_shape=[jax.ShapeDtypeStruct((t, c), BF16), jax.ShapeDtypeStruct((t, d), F32)],
        scratch_shapes=[pltpu.VMEM((tm, d), BF16)],
        compiler_params=_params("parallel", "arbitrary"),
        name="norm_mod_matmul",
    )(x2, g, sc, sh, w)


def _s5_inproj_kernel(h_ref, wt_ref, o_ref, *, jt):
    h = _tok_major(h_ref).astype(BF16)
    ut = lax.dot_general(wt_ref[...], h, (((1,), (1,)), ((), ())), preferred_element_type=F32)
    for jj in range(jt):
        o_ref[jj] = ut[:, jj * LANE:(jj + 1) * LANE].astype(BF16)


def s5_inproj(h3, wt, layer, jt):
    r, _, d = h3.shape
    s5w = wt.shape[1]
    return pl.pallas_call(
        functools.partial(_s5_inproj_kernel, jt=jt),
        grid=(r // LANE, CHUNK // jt),
        in_specs=[pl.BlockSpec((LANE, jt, d), lambda i, j: (i, j, 0)),
                  pl.BlockSpec((None, s5w, d), lambda i, j: (layer, 0, 0))],
        out_specs=pl.BlockSpec((jt, s5w, LANE), lambda i, j: (j, 0, i)),
        out_shape=jax.ShapeDtypeStruct((CHUNK, s5w, r), BF16),
        compiler_params=_params("parallel", "parallel"),
        name="s5_inproj",
    )(h3, wt)


S5_CW = CHUNK * S5_GROUP_CH


def _cpow(lr_dt, li_dt, e):
    mag = jnp.exp(lr_dt * e)
    ang = li_dt * e
    return mag * jnp.cos(ang), mag * jnp.sin(ang)


def _s5_param_kernel(ldt_ref, lrc_ref, lic_ref, lrr_ref, lir_ref, bre_ref, bim_ref,
                     c1_ref, c2_ref, dcol_ref, m_ref, et_ref, ft_ref, ac_ref):
    p, cw, h = S5_STATE, S5_CW, S5_GROUP_CH
    dt = jnp.exp(ldt_ref[0])
    lrc, lic = lrc_ref[0], lic_ref[0]
    ab_re, ab_im = _cpow(lrc * dt, lic * dt, 1.0)
    den = lrc * lrc + lic * lic
    nr, ni = ab_re - 1.0, ab_im
    w_re = (nr * lrc + ni * lic) / den
    w_im = (ni * lrc - nr * lic) / den
    trow = lax.broadcasted_iota(jnp.int32, (h, cw), 0)
    tcol = lax.broadcasted_iota(jnp.int32, (h, cw), 1)
    tile_l = (trow == tcol % h).astype(BF16)
    bre, bim = _select_cols(bre_ref[0], tile_l), _select_cols(bim_ref[0], tile_l)
    bb_re = w_re * bre - w_im * bim
    bb_im = w_re * bim + w_im * bre
    lane = lax.broadcasted_iota(jnp.int32, (1, LANE), 1)
    e_m = jnp.where(lane < CHUNK, CHUNK - 1 - lane, 0).astype(F32)
    pd_re, pd_im = _cpow(lrc * dt, lic * dt, e_m)
    xrow = lax.broadcasted_iota(jnp.int32, (LANE, cw), 0)
    xcol = lax.broadcasted_iota(jnp.int32, (LANE, cw), 1)
    expand_l = (xrow == xcol // h).astype(BF16)
    p_re, p_im = _select_cols(pd_re, expand_l), _select_cols(pd_im, expand_l)
    e_re = p_re * bb_re - p_im * bb_im
    e_im = p_re * bb_im + p_im * bb_re
    et = jnp.concatenate([e_re, e_im], axis=0)
    et_ref[0] = et.astype(BF16)
    a_re, a_im = _cpow(lrc * dt, lic * dt, float(CHUNK))
    ac_ref[0] = jnp.concatenate([a_re, a_im], axis=0)
    sgn = jnp.where(lax.broadcasted_iota(jnp.int32, (1, 2 * p), 1) < p, 1.0, -1.0)
    krev = _mm_hi(c1_ref[0] * sgn, et)
    row = lax.broadcasted_iota(jnp.int32, (h, cw), 0)
    col = lax.broadcasted_iota(jnp.int32, (h, cw), 1)
    krev = krev + jnp.where(col == (cw - h) + row, dcol_ref[0], 0.0)
    rrev = jnp.concatenate([krev, jnp.zeros_like(krev)], axis=1)
    per_tile = LANE // h
    rolled = [rrev if r == 0 else pltpu.roll(rrev, 2 * cw - r * h, axis=1) for r in range(per_tile)]
    for i in range(CHUNK):
        s = (CHUNK - 1 - i) * h
        a, r = s // LANE, (s % LANE) // h
        m_ref[0, i * h:(i + 1) * h, :] = rolled[r][:, a * LANE:a * LANE + cw].astype(BF16)
    f_i = (lax.broadcasted_iota(jnp.int32, (CHUNK, 1), 0) + 1).astype(F32)
    qd_re, qd_im = _cpow(lrr_ref[0] * dt, lir_ref[0] * dt, f_i)
    ft = (c1_ref[0] * sgn)[None] * qd_re[:, None, :] - c2_ref[0][None] * qd_im[:, None, :]
    ft_ref[0] = ft.reshape(cw, 2 * p).astype(BF16)


def s5_params(lam_re, lam_im, log_dt, b_re, b_im, c_re, c_im, dpar):
    ng = lam_re.shape[0] * lam_re.shape[1]
    p, h, cw = S5_STATE, S5_GROUP_CH, S5_CW
    lam_re = lam_re.reshape(ng, p)
    lam_im = lam_im.reshape(ng, p)
    c_re = c_re.reshape(ng, h, p)
    c_im = c_im.reshape(ng, h, p)
    c1 = jnp.concatenate([c_re, c_im], axis=-1)
    c2 = jnp.concatenate([c_im, c_re], axis=-1)
    args = (log_dt.reshape(ng, 1, 1),
            lam_re.reshape(ng, p, 1), lam_im.reshape(ng, p, 1),
            jnp.tile(lam_re.reshape(ng, 1, p), (1, 1, 2)), jnp.tile(lam_im.reshape(ng, 1, p), (1, 1, 2)),
            b_re.reshape(ng, p, h), b_im.reshape(ng, p, h), c1, c2,
            dpar.reshape(ng, h, 1))

    def spec(a):
        return pl.BlockSpec((1,) + a.shape[1:], lambda i: (i, 0, 0))

    return pl.pallas_call(
        _s5_param_kernel,
        grid=(ng,),
        in_specs=[spec(a) for a in args],
        out_specs=[pl.BlockSpec((1, cw, cw), lambda i: (i, 0, 0)),
                   pl.BlockSpec((1, 2 * p, cw), lambda i: (i, 0, 0)),
                   pl.BlockSpec((1, cw, 2 * p), lambda i: (i, 0, 0)),
                   pl.BlockSpec((1, 2 * p, 1), lambda i: (i, 0, 0))],
        out_shape=[jax.ShapeDtypeStruct((ng, cw, cw), BF16),
                   jax.ShapeDtypeStruct((ng, 2 * p, cw), BF16),
                   jax.ShapeDtypeStruct((ng, cw, 2 * p), BF16),
                   jax.ShapeDtypeStruct((ng, 2 * p, 1), F32)],
        compiler_params=_params("parallel"),
        name="s5_params",
    )(*args)


def _s5_mix_kernel(u_ref, m_ref, et_ref, ft_ref, ac_ref, y_ref, *, nchunk, gsub):
    p, h = S5_STATE, S5_GROUP_CH
    r = u_ref.shape[-1]
    ks = range(gsub)
    u = [u_ref[:, k * h:(k + 1) * h, :].reshape(S5_CW, r) for k in ks]
    s = [jnp.dot(et_ref[k], u[k], preferred_element_type=F32) for k in ks]
    s_re, s_im = [s[k][:p] for k in ks], [s[k][p:] for k in ks]
    a_re, a_im = [ac_ref[k][:p] for k in ks], [ac_ref[k][p:] for k in ks]
    n_idx = lax.broadcasted_iota(jnp.int32, (1, r), 1) % nchunk
    shift = 1
    while shift < nchunk:
        keep = n_idx >= shift
        t_re = [jnp.where(keep, pltpu.roll(s_re[k], shift, axis=1), 0.0) for k in ks]
        t_im = [jnp.where(keep, pltpu.roll(s_im[k], shift, axis=1), 0.0) for k in ks]
        s_re, s_im = ([s_re[k] + a_re[k] * t_re[k] - a_im[k] * t_im[k] for k in ks],
                      [s_im[k] + a_re[k] * t_im[k] + a_im[k] * t_re[k] for k in ks])
        a_re, a_im = ([a_re[k] * a_re[k] - a_im[k] * a_im[k] for k in ks], [2.0 * a_re[k] * a_im[k] for k in ks])
        shift *= 2
    keep = n_idx >= 1
    for k in ks:
        h_prev = jnp.concatenate([jnp.where(keep, pltpu.roll(s_re[k], 1, axis=1), 0.0),
                                  jnp.where(keep, pltpu.roll(s_im[k], 1, axis=1), 0.0)], axis=0)
        y = jnp.dot(m_ref[k], u[k], preferred_element_type=F32)
        y = y + jnp.dot(ft_ref[k], h_prev.astype(BF16), preferred_element_type=F32)
        y_ref[:, k * h:(k + 1) * h, :] = y.reshape(CHUNK, h, r).astype(y_ref.dtype)


def s5_mix(ut, m, et, ft, ac, layer, nchunk, gsub):
    _, s5w, r = ut.shape
    g, h, p, cw = S5_GROUPS, S5_GROUP_CH, S5_STATE, S5_CW
    base = layer * g // gsub
    return pl.pallas_call(
        functools.partial(_s5_mix_kernel, nchunk=nchunk, gsub=gsub),
        grid=(g // gsub,),
        in_specs=[pl.BlockSpec((CHUNK, gsub * h, r), lambda i: (0, i, 0)),
                  pl.BlockSpec((gsub, cw, cw), lambda i: (base + i, 0, 0)),
                  pl.BlockSpec((gsub, 2 * p, cw), lambda i: (base + i, 0, 0)),
                  pl.BlockSpec((gsub, cw, 2 * p), lambda i: (base + i, 0, 0)),
                  pl.BlockSpec((gsub, 2 * p, 1), lambda i: (base + i, 0, 0))],
        out_specs=pl.BlockSpec((CHUNK, gsub * h, r), lambda i: (0, i, 0)),
        out_shape=jax.ShapeDtypeStruct((CHUNK, s5w, r), BF16),
        compiler_params=_params("parallel"),
        name="s5_mix",
    )(ut, m, et, ft, ac)


def _gla_kernel(q_ref, k_ref, v_ref, og_ref, lr_ref, wlr_ref, blr_ref, ng_ref, o_ref, st_ref, *, nc):
    hd, dk, dv = GLA_HEADS, GLA_DK, GLA_DV

    @pl.when(pl.program_id(1) == 0)
    def _():
        st_ref[...] = jnp.zeros_like(st_ref)

    ri = lax.broadcasted_iota(jnp.int32, (CHUNK, CHUNK), 0)
    ci = lax.broadcasted_iota(jnp.int32, (CHUNK, CHUNK), 1)
    incl = ri >= ci
    ltri = incl.astype(BF16)
    lane_k = lax.broadcasted_iota(jnp.int32, (1, hd * dk), 1)
    srow = lax.broadcasted_iota(jnp.int32, (hd * dv, hd * dk), 0)
    scol = lax.broadcasted_iota(jnp.int32, (hd * dv, hd * dk), 1)
    same_head = (srow // dv) == (scol // dk)
    wlr, blr, ng = wlr_ref[...], blr_ref[...], ng_ref[...]

    cs = range(nc)
    cr = lambda c: slice(c * CHUNK, (c + 1) * CHUNK)
    g = [_log_sigmoid(_mm(lr_ref[cr(c), :], wlr) + blr) * (1.0 / GLA_GATE_NORM) for c in cs]
    bc = [_select_rows(ltri, g[c]) for c in cs]
    bl = [bc[c][CHUNK - 1:CHUNK, :] for c in cs]
    q_e = [q_ref[cr(c), :].astype(F32) * (dk ** -0.5) * jnp.exp(bc[c]) for c in cs]
    k_e = [k_ref[cr(c), :].astype(F32) * jnp.exp(-bc[c]) for c in cs]
    k_d = [k_ref[cr(c), :].astype(F32) * jnp.exp(bl[c] - bc[c]) for c in cs]
    kv = [jnp.where(same_head, _mm_tn(v_ref[cr(c), :], k_d[c]), 0.0) for c in cs]
    sts = []
    st = st_ref[...]
    for c in cs:
        sts.append(st)
        st = jnp.exp(bl[c]) * st + kv[c]
    st_ref[...] = st
    o_inter = [_mm_nt(q_e[c], sts[c]) for c in cs]
    q_heads = [jnp.concatenate([jnp.where((lane_k // dk) == h, q_e[c], 0.0) for h in range(hd)], axis=0)
               for c in cs]
    sc_all = [_mm_nt(q_heads[c], k_e[c]) for c in cs]
    sc = [[jnp.where(incl, sc_all[c][h * CHUNK:(h + 1) * CHUNK], 0.0) for h in range(hd)] for c in cs]
    for c in cs:
        for h in range(hd):
            cols = slice(h * dv, (h + 1) * dv)
            oh = _mm(sc[c][h], v_ref[cr(c), cols]) + o_inter[c][:, cols]
            ms = jnp.mean(oh * oh, axis=-1, keepdims=True)
            o_ref[cr(c), cols] = (oh * lax.rsqrt(ms + EPS) * ng * _silu(og_ref[cr(c), cols].astype(F32))).astype(o_ref.dtype)


def gla_mix(zm, wlr_pad, blr, ng, bsz, seq, tc):
    t = zm.shape[0]
    per_b = seq // tc
    hd, dk, dv = GLA_HEADS, GLA_DK, GLA_DV

    def zspec(width, off):
        blk = off // width
        return pl.BlockSpec((tc, width), lambda b, i: (b * per_b + i, blk))

    def full(a):
        return pl.BlockSpec(a.shape, lambda b, i: (0,) * a.ndim)

    return pl.pallas_call(
        functools.partial(_gla_kernel, nc=tc // CHUNK),
        grid=(bsz, per_b),
        in_specs=[zspec(hd * dk, ZM_GQ), zspec(hd * dk, ZM_GK), zspec(hd * dv, ZM_GV),
                  zspec(hd * dv, ZM_GOG), zspec(LANE, ZM_GLR), full(wlr_pad), full(blr), full(ng)],
        out_specs=pl.BlockSpec((tc, hd * dv), lambda b, i: (b * per_b + i, 0)),
        out_shape=jax.ShapeDtypeStruct((t, hd * dv), F32),
        scratch_shapes=[pltpu.VMEM((hd * dv, hd * dk), F32)],
        compiler_params=_params("parallel", "arbitrary"),
        name="gla_mix",
    )(zm, zm, zm, zm, zm, wlr_pad, blr, ng)


def _gdn_kernel(qkv_ref, og_ref, ba_ref, cw_ref, shift_ref, alog_ref, dtb_ref, ng_ref, o_ref,
                head_ref, s_ref, *, nc):
    hd, dk, dv, kc = GDN_HEADS, GDN_DK, GDN_DV, GDN_CONV
    tc = nc * CHUNK
    pad = 8

    @pl.when(pl.program_id(1) == 0)
    def _():
        s_ref[...] = jnp.zeros_like(s_ref)
        head_ref[0:pad, :] = jnp.zeros((pad, head_ref.shape[1]), F32)

    head_ref[pad:2 * pad, :] = qkv_ref[0:2 * pad, :].astype(F32)[0:pad]
    lagged = jnp.dot(shift_ref[...], qkv_ref[...], preferred_element_type=F32)

    ri = lax.broadcasted_iota(jnp.int32, (CHUNK, CHUNK), 0)
    ci = lax.broadcasted_iota(jnp.int32, (CHUNK, CHUNK), 1)
    incl = ri >= ci
    strict = ri > ci
    ltri = incl.astype(BF16)
    eye = (ri == ci).astype(F32)
    cw = [cw_ref[i] for i in range(kc)]
    ng = ng_ref[...]

    ba = ba_ref[...].astype(F32)
    beta_all = _sigmoid(ba)
    g_all = -jnp.exp(alog_ref[...]) * _softplus(ba + dtb_ref[...])

    cr = lambda c: slice(c * CHUNK, (c + 1) * CHUNK)
    gam_c = [_select_rows(ltri, g_all[cr(c), :]) for c in range(nc)]
    gam_tc = [g.T for g in gam_c]
    groups = [list(range(g0, min(g0 + GDN_GROUP_CHUNKS, nc))) for g0 in range(0, nc, GDN_GROUP_CHUNKS)]

    def conv_piece(chunks, col):
        lanes = slice(col * dk, (col + 1) * dk)
        nr = len(chunks) * CHUNK
        assert chunks[0] == 0 and nr == tc
        blocks = lambda a: a.reshape(nr // 8, 8, dk)
        acc = cw[kc - 1][:, lanes] * blocks(qkv_ref[:, lanes].astype(F32))
        head = cw[kc - 1][:, lanes] * head_ref[pad:2 * pad, lanes]
        for i in range(kc - 1):
            lag = kc - 1 - i
            acc = acc + cw[i][:, lanes] * blocks(lagged[(lag - 1) * tc:lag * tc, lanes])
            head = head + cw[i][:, lanes] * head_ref[pad - lag:2 * pad - lag, lanes]
        acc = jnp.concatenate([head, acc.reshape(nr, dk)[pad:]], axis=0)
        y = _silu(acc)
        if col < 2 * hd:
            y = y * lax.rsqrt(jnp.sum(y * y, axis=-1, keepdims=True) + EPS)
        if col < hd:
            y = y * (dk ** -0.5)
        return [y[i * CHUNK:(i + 1) * CHUNK] for i in range(len(chunks))]

    def chain_stages(chunks, cols, res):
        pairs = [(ci, c, h) for ci, c in enumerate(chunks) for h in range(hd)]
        n = len(pairs)
        q = [cols[h][ci] for ci, c, h in pairs]
        k = [cols[hd + h][ci] for ci, c, h in pairs]
        v = [cols[2 * hd + h][ci] for ci, c, h in pairs]
        beta = [beta_all[cr(c), h:h + 1] for ci, c, h in pairs]
        gam = [gam_c[c][:, hd + h:hd + h + 1] for ci, c, h in pairs]
        gam_row = [gam_tc[c][hd + h:hd + h + 1, :] for ci, c, h in pairs]
        dmask = [jnp.where(incl, jnp.exp(gam[i] - gam_row[i]), 0.0) for i in range(n)]
        k_beta = [k[i] * beta[i] for i in range(n)]
        egam = [jnp.exp(gam[i]) for i in range(n)]
        kk = [_mm_nt(jnp.concatenate([k_beta[i], q[i]], axis=0), k[i]) for i in range(n)]
        low = [jnp.where(strict, kk[i][:CHUNK] * dmask[i], 0.0) for i in range(n)]
        res["attn"] = [kk[i][CHUNK:] * dmask[i] for i in range(n)]
        rhs = [jnp.concatenate([v[i] * beta[i], k_beta[i] * egam[i]], axis=1).astype(BF16) for i in range(n)]
        yield
        lowb = [low[i].astype(BF16) for i in range(n)]
        pw = [_mm(lowb[i], lowb[i]).astype(BF16) for i in range(n)]
        tinv = [eye - low[i] for i in range(n)]
        for s in range(5):
            yield
            if s < 4:
                stk = [_mm(jnp.concatenate([tinv[i].astype(BF16), pw[i]], axis=0), pw[i]) for i in range(n)]
                tinv = [tinv[i] + stk[i][:CHUNK] for i in range(n)]
                pw = [stk[i][CHUNK:].astype(BF16) for i in range(n)]
            else:
                tinv = [tinv[i] + _mm(tinv[i], pw[i]) for i in range(n)]
        yield
        sol = [_mm(tinv[i], rhs[i]).astype(BF16) for i in range(n)]
        q_dec = [q[i] * egam[i] for i in range(n)]
        gam_last = [gam[i][CHUNK - 1:CHUNK, :] for i in range(n)]
        k_dec = [k[i] * jnp.exp(gam_last[i] - gam[i]) for i in range(n)]
        yield
        ks = [_mm_tn(k_dec[i], sol[i]) for i in range(n)]
        aw = [_mm(res["attn"][i], sol[i]) for i in range(n)]
        res["lhs"] = [jnp.concatenate([ks[i][:, dv:], q_dec[i] - aw[i][:, dv:]], axis=0).astype(BF16)
                      for i in range(n)]
        res["s_add"] = [ks[i][:, :dv] for i in range(n)]
        res["o_add"] = [aw[i][:, :dv] for i in range(n)]
        res["decay"] = [jnp.exp(gam_last[i]) for i in range(n)]

    ncols = 3 * hd
    cols = [conv_piece(groups[0], col) for col in range(ncols)]
    st = [s_ref[h] for h in range(hd)]
    for gi, chunks in enumerate(groups):
        res = {}
        nxt = groups[gi + 1] if gi + 1 < len(groups) else None
        nxt_cols = []
        for _ in chain_stages(chunks, cols, res):
            if nxt is not None and len(nxt_cols) < ncols:
                nxt_cols.append(conv_piece(nxt, len(nxt_cols)))
        while nxt is not None and len(nxt_cols) < ncols:
            nxt_cols.append(conv_piece(nxt, len(nxt_cols)))
        cols = nxt_cols
        lhs, s_add, o_add, decay = res["lhs"], res["s_add"], res["o_add"], res["decay"]
        for ci, c in enumerate(chunks):
            ids = [ci * hd + h for h in range(hd)]
            prod = [_mm(lhs[i], st[h]) for h, i in enumerate(ids)]
            o = [prod[h][dk:] + o_add[i] for h, i in enumerate(ids)]
            st = [decay[i] * st[h] - prod[h][:dk] + s_add[i] for h, i in enumerate(ids)]
            for h in range(hd):
                ms = jnp.mean(o[h] * o[h], axis=-1, keepdims=True)
                cl = slice(h * dv, (h + 1) * dv)
                o_ref[cr(c), cl] = (o[h] * lax.rsqrt(ms + EPS) * ng * _silu(og_ref[cr(c), cl].astype(F32))).astype(o_ref.dtype)
    for h in range(hd):
        s_ref[h] = st[h]
    head_ref[0:pad, :] = qkv_ref[tc - 2 * pad:tc, :].astype(F32)[pad:]


def gdn_mix(zm, conv_w_pad, shift_m, alog_row, dtb_row, ng, bsz, seq, tc):
    t = zm.shape[0]
    per_b = seq // tc
    hd, dk, dv = GDN_HEADS, GDN_DK, GDN_DV
    qkv_w = 3 * hd * dk

    def zspec(width, off):
        blk = off // width
        return pl.BlockSpec((tc, width), lambda b, i: (b * per_b + i, blk))

    def full(a):
        return pl.BlockSpec(a.shape, lambda b, i: (0,) * a.ndim)

    return pl.pallas_call(
        functools.partial(_gdn_kernel, nc=tc // CHUNK),
        grid=(bsz, per_b),
        in_specs=[zspec(qkv_w, ZM_DQKV), zspec(hd * dv, ZM_DOG), zspec(LANE, ZM_DBA),
                  full(conv_w_pad), full(shift_m), full(alog_row), full(dtb_row), full(ng)],
        out_specs=pl.BlockSpec((tc, hd * dv), lambda b, i: (b * per_b + i, 0)),
        out_shape=jax.ShapeDtypeStruct((t, hd * dv), F32),
        scratch_shapes=[pltpu.VMEM((16, qkv_w), F32), pltpu.VMEM((hd, dk, dv), F32)],
        compiler_params=_params("parallel", "arbitrary"),
        name="gdn_mix",
    )(zm, zm, zm, conv_w_pad, shift_m, alog_row, dtb_row, ng)


def _merge_kernel(h3_ref, ygla_ref, ygdn_ref, yst_ref, wglu_ref, wzg_ref, wbg_ref, wbs_ref, wbd_ref, o_ref,
                  s5t_ref, *, jt, bw, tn):
    n_rows = jt * LANE
    d = h3_ref.shape[-1]
    h = h3_ref[...].reshape(n_rows, d).astype(BF16)
    a_gla = ygla_ref[...].reshape(n_rows, bw).astype(BF16)
    a_gdn = ygdn_ref[...].reshape(n_rows, bw).astype(BF16)
    for jj in range(jt):
        ys = _gelu_tanh(yst_ref[jj].astype(F32))
        glu = _mm_tn(ys, wglu_ref[...])
        s5t_ref[jj * LANE:(jj + 1) * LANE, :] = glu[:, :bw] * _sigmoid(glu[:, bw:])
    a_s5 = pltpu.einshape("(jr)d->rjd", s5t_ref[...], j=jt).reshape(n_rows, bw).astype(BF16)

    def branch(b, cols, a, wb_ref):
        gate = _sigmoid(jnp.dot(h, wzg_ref[:, b * d + cols.start:b * d + cols.stop], preferred_element_type=F32))
        return gate * jnp.dot(a, wb_ref[:, cols], preferred_element_type=F32)

    for c0 in range(0, d, tn):
        cols = slice(c0, c0 + tn)
        m = branch(0, cols, a_gla, wbg_ref) + branch(1, cols, a_s5, wbs_ref) + branch(2, cols, a_gdn, wbd_ref)
        o_ref[:, :, cols] = m.reshape(LANE, jt, tn)


def merge_gate(h3, ygla3, ygdn3, yst, wglu, wzg, wbg, wbs, wbd, layer, jt, tn):
    r, _, d = h3.shape
    bw = ygla3.shape[2]

    def view(width):
        return pl.BlockSpec((LANE, jt, width), lambda i, j: (i, j, 0))

    def layer_w(a):
        return pl.BlockSpec((None,) + a.shape[1:], lambda i, j: (layer, 0, 0))

    return pl.pallas_call(
        functools.partial(_merge_kernel, jt=jt, bw=bw, tn=tn),
        grid=(r // LANE, CHUNK // jt),
        in_specs=[view(d), view(bw), view(bw),
                  pl.BlockSpec((jt, yst.shape[1], LANE), lambda i, j: (j, 0, i)),
                  layer_w(wglu), layer_w(wzg), layer_w(wbg), layer_w(wbs), layer_w(wbd)],
        out_specs=view(d),
        out_shape=jax.ShapeDtypeStruct((r, CHUNK, d), F32),
        scratch_shapes=[pltpu.VMEM((jt * LANE, bw), F32)],
        compiler_params=_params("parallel", "parallel"),
        name="merge_gate",
    )(h3, ygla3, ygdn3, yst, wglu, wzg, wbg, wbs, wbd)


def _ffn_kernel(x_ref, m_ref, wout_ref, gt1_ref, g_ref, sc_ref, sh_ref, gt2_ref, win_ref, wo_ref,
                fg_ref, o_ref, *, final_norm, tf):
    dff = wo_ref.shape[0]
    x1 = x_ref[...] + gt1_ref[...] * _mm(m_ref[...], wout_ref[...])
    h = _norm_mod(x1, g_ref[...], sc_ref[...], sh_ref[...]).astype(BF16)
    acc = None
    for f0 in range(0, dff, tf):
        a = jnp.dot(h, win_ref[:, f0:f0 + tf], preferred_element_type=F32)
        b = jnp.dot(h, win_ref[:, dff + f0:dff + f0 + tf], preferred_element_type=F32)
        part = _mm(_silu(a) * b, wo_ref[f0:f0 + tf, :])
        acc = part if acc is None else acc + part
    y = x1 + gt2_ref[...] * acc
    if final_norm:
        ms = jnp.mean(y * y, axis=-1, keepdims=True)
        y = y * lax.rsqrt(ms + EPS) * fg_ref[...]
    o_ref[...] = y


def ffn_block(x2, merged, w_o, gt1, g, sc, sh, gt2, w_in, w_out, final_g, layer, seq, tm, tf, final_norm):
    t, d = x2.shape
    dff = w_out.shape[1]
    per_b = seq // tm

    def bspec():
        return pl.BlockSpec((None, 1, d), lambda i: (i // per_b, 0, 0))

    def const(a):
        return pl.BlockSpec(a.shape, lambda i: (0,) * a.ndim)

    def layer_w(a):
        return pl.BlockSpec((None,) + a.shape[1:], lambda i: (layer, 0, 0))

    row = pl.BlockSpec((tm, d), lambda i: (i, 0))
    return pl.pallas_call(
        functools.partial(_ffn_kernel, final_norm=final_norm, tf=tf),
        grid=(t // tm,),
        in_specs=[row, row, layer_w(w_o), bspec(), const(g), bspec(), bspec(), bspec(),
                  layer_w(w_in), layer_w(w_out), const(final_g)],
        out_specs=row,
        out_shape=jax.ShapeDtypeStruct((t, d), F32),
        compiler_params=_params("parallel"),
        name="ffn_block",
    )(x2, merged, w_o, gt1, g, sc, sh, gt2, w_in, w_out, final_g)


def _wprep_kernel(w_ref, zm_ref, zg_ref, s5t_ref, *, segs, pads, zg_off, s5_off):
    for dst, width in pads:
        zm_ref[0, :, dst:dst + width] = jnp.zeros((zm_ref.shape[1], width), BF16)
    for src, width, dst in segs:
        zm_ref[0, :, dst:dst + width] = w_ref[0, :, src:src + width].astype(BF16)
    zg_ref[0] = w_ref[0, :, zg_off:zg_off + zg_ref.shape[2]].astype(BF16)
    s5t_ref[0] = w_ref[0, :, s5_off:s5_off + s5t_ref.shape[1]].astype(F32).T.astype(BF16)


def prep_in_weights(w_in, segs, pads, zg_off, zg_w, s5_off, s5_w, tr):
    depth, d, d_in = w_in.shape
    return pl.pallas_call(
        functools.partial(_wprep_kernel, segs=segs, pads=pads, zg_off=zg_off, s5_off=s5_off),
        grid=(depth, d // tr),
        in_specs=[pl.BlockSpec((1, tr, d_in), lambda i, j: (i, j, 0))],
        out_specs=[pl.BlockSpec((1, tr, ZM_WIDTH), lambda i, j: (i, j, 0)),
                   pl.BlockSpec((1, tr, zg_w), lambda i, j: (i, j, 0)),
                   pl.BlockSpec((1, s5_w, tr), lambda i, j: (i, 0, j))],
        out_shape=[jax.ShapeDtypeStruct((depth, d, ZM_WIDTH), BF16),
                   jax.ShapeDtypeStruct((depth, d, zg_w), BF16),
                   jax.ShapeDtypeStruct((depth, s5_w, d), BF16)],
        compiler_params=_params("parallel", "parallel"),
        name="prep_in_weights",
    )(w_in)


def _forward(x, c, w_ada, b_ada, norm1_g, w_in, gla_w_lr, gla_b_lr, gla_norm_g,
             s5_lambda_re, s5_lambda_im, s5_log_dt, s5_b_re, s5_b_im, s5_c_re, s5_c_im,
             s5_d, s5_w_glu, gdn_conv_w, gdn_a_log, gdn_dt_bias, gdn_norm_g,
             w_branch_gla, w_branch_s5, w_branch_gdn, w_out, norm2_g, w_ffn_in, w_ffn_out,
             final_g):
    bsz, seq, d = x.shape
    depth = w_ada.shape[0]
    t = bsz * seq
    nchunk = seq // CHUNK
    r = bsz * nchunk
    gqk, gw = GLA_HEADS * GLA_DK, GLA_HEADS * GLA_DV
    s5w = S5_GROUPS * S5_GROUP_CH
    dqkv, dw = 3 * GDN_HEADS * GDN_DK, GDN_HEADS * GDN_DV

    mod = ada_modulation(c, w_ada, b_ada).reshape(depth, bsz, 6, 1, d)
    s5m, s5et, s5ft, s5ac = s5_params(s5_lambda_re, s5_lambda_im, s5_log_dt, s5_b_re, s5_b_im,
                                      s5_c_re, s5_c_im, s5_d)

    o_gq, o_gk, o_gv = 0, gqk, 2 * gqk
    o_glr = o_gv + gw
    o_gog = o_glr + GLA_LOWRANK
    o_s5 = o_gog + gw
    o_dqkv = o_s5 + s5w
    o_dbeta = o_dqkv + dqkv
    o_da = o_dbeta + GDN_HEADS
    o_dog = o_da + GDN_HEADS
    o_zg = o_dog + dw

    segs = ((o_gq, gqk, ZM_GQ), (o_gk, gqk, ZM_GK), (o_gv, gw, ZM_GV), (o_gog, gw, ZM_GOG),
            (o_dqkv, dqkv, ZM_DQKV), (o_dog, dw, ZM_DOG), (o_glr, GLA_LOWRANK, ZM_GLR),
            (o_dbeta, 2 * GDN_HEADS, ZM_DBA))
    pads = ((ZM_GLR, LANE), (ZM_DBA, LANE))
    w_in_b = jnp.pad(w_in.astype(BF16), ((0, 0), (0, 0), (0, -w_in.shape[2] % LANE)))
    w_zm_all, w_zg_all, w_s5t_all = prep_in_weights(w_in_b, segs, pads, o_zg, N_BRANCH * d, o_s5, s5w, tr=256)
    s5_w_glu, w_branch_gla, w_branch_s5, w_branch_gdn, w_out, w_ffn_in, w_ffn_out = [
        a.astype(BF16) for a in (s5_w_glu, w_branch_gla, w_branch_s5, w_branch_gdn, w_out, w_ffn_in, w_ffn_out)]

    gdn_tc = 256
    lag_row = jnp.arange((GDN_CONV - 1) * gdn_tc)
    gdn_shift = (jnp.arange(gdn_tc)[None, :]
                 == (lag_row % gdn_tc - lag_row // gdn_tc - 1)[:, None]).astype(BF16)

    x2 = x.reshape(t, d)
    for i in range(depth):

        sh1, sc1, gt1, sh2, sc2, gt2 = [mod[i, :, k] for k in range(6)]
        g1 = norm1_g[i].reshape(1, d)

        zm, hf = norm_mod_matmul(x2, g1, sc1, sh1, w_zm_all, i, seq, tm=512, tn=3840)
        h3 = hf.reshape(r, CHUNK, d)
        ut = s5_inproj(h3, w_s5t_all, i, jt=8)

        wlr_pad = jnp.pad(gla_w_lr[i], ((0, LANE - GLA_LOWRANK), (0, 0)))
        y_gla = gla_mix(zm, wlr_pad, gla_b_lr[i].reshape(1, gqk), gla_norm_g[i].reshape(1, GLA_DV),
                        bsz, seq, tc=256)
        yst = s5_mix(ut, s5m, s5et, s5ft, s5ac, i, nchunk, gsub=4)
        conv_pad = jnp.broadcast_to(gdn_conv_w[i][:, None, :], (GDN_CONV, 8, dqkv))
        alog_row = jnp.pad(gdn_a_log[i], (GDN_HEADS, LANE - 2 * GDN_HEADS)).reshape(1, LANE)
        dtb_row = jnp.pad(gdn_dt_bias[i], (GDN_HEADS, LANE - 2 * GDN_HEADS)).reshape(1, LANE)
        y_gdn = gdn_mix(zm, conv_pad, gdn_shift, alog_row, dtb_row, gdn_norm_g[i].reshape(1, GDN_DV),
                        bsz, seq, tc=gdn_tc)

        merged = merge_gate(h3, y_gla.reshape(r, CHUNK, gw), y_gdn.reshape(r, CHUNK, dw),
                            yst, s5_w_glu, w_zg_all, w_branch_gla, w_branch_s5, w_branch_gdn, i,
                            jt=8, tn=256)
        x2 = ffn_block(x2, merged.reshape(t, d), w_out, gt1,
                       norm2_g[i].reshape(1, d), sc2, sh2, gt2,
                       w_ffn_in, w_ffn_out,
                       final_g.reshape(1, d), i, seq, tm=512, tf=256, final_norm=(i == depth - 1))
    return x2.reshape(bsz, seq, d)


def kernel(x, c, w_ada, b_ada, norm1_g, w_in, gla_w_lr, gla_b_lr, gla_norm_g, s5_lambda_re, s5_lambda_im, s5_log_dt, s5_b_re, s5_b_im, s5_c_re, s5_c_im, s5_d, s5_w_glu, gdn_conv_w, gdn_a_log, gdn_dt_bias, gdn_norm_g, w_branch_gla, w_branch_s5, w_branch_gdn, w_out, norm2_g, w_ffn_in, w_ffn_out, final_g):
    return _forward(x, c, w_ada, b_ada, norm1_g, w_in, gla_w_lr, gla_b_lr, gla_norm_g,
                    s5_lambda_re, s5_lambda_im, s5_log_dt, s5_b_re, s5_b_im, s5_c_re, s5_c_im,
                    s5_d, s5_w_glu, gdn_conv_w, gdn_a_log, gdn_dt_bias, gdn_norm_g,
                    w_branch_gla, w_branch_s5, w_branch_gdn, w_out, norm2_g, w_ffn_in, w_ffn_out,
                    final_g)
```

```python
import functools
import math

import jax
import jax.numpy as jnp
from jax import lax
from jax.experimental import pallas as pl
from jax.experimental.pallas import tpu as pltpu

F32 = jnp.float32
BF16 = jnp.bfloat16
HI = lax.Precision.HIGHEST

EPS = 1e-6
CHUNK = 64
LANE = 128
VMEM_LIMIT = 56 * 1024 * 1024

GLA_HEADS, GLA_DK, GLA_DV, GLA_LOWRANK = 4, 64, 128, 16
GLA_GATE_NORM = 16.0
S5_GROUPS, S5_GROUP_CH, S5_STATE = 32, 16, 64
GDN_HEADS, GDN_DK, GDN_DV, GDN_CONV = 4, 128, 128, 4
N_BRANCH = 3
GDN_GROUP_CHUNKS = 4

ZM_GQ, ZM_GK, ZM_GV, ZM_GOG = 0, 256, 512, 1024
ZM_DQKV, ZM_DOG, ZM_GLR, ZM_DBA = 1536, 3072, 3584, 3712
ZM_WIDTH = 3840


def _mm(a, b):
    return jnp.dot(a.astype(BF16), b.astype(BF16), preferred_element_type=F32)


def _mm_nt(a, b):
    return lax.dot_general(a.astype(BF16), b.astype(BF16), (((1,), (1,)), ((), ())),
                           preferred_element_type=F32)


def _mm_tn(a, b):
    return lax.dot_general(a.astype(BF16), b.astype(BF16), (((0,), (0,)), ((), ())),
                           preferred_element_type=F32)


def _mm_hi(a, b):
    return jnp.dot(a, b, precision=HI, preferred_element_type=F32)


def _split3(x):
    hi = x.astype(BF16)
    r1 = x - hi.astype(F32)
    mid = r1.astype(BF16)
    return hi, mid, (r1 - mid.astype(F32)).astype(BF16)


def _select_rows(m01, x):
    return sum(jnp.dot(m01, t, preferred_element_type=F32) for t in _split3(x))


def _select_cols(x, m01):
    return sum(jnp.dot(t, m01, preferred_element_type=F32) for t in _split3(x))


def _sigmoid(x):
    return 0.5 + 0.5 * jnp.tanh(0.5 * x)


def _silu(x):
    h = 0.5 * x
    return h + h * jnp.tanh(h)


def _softplus(x):
    return jnp.maximum(x, 0.0) + jnp.log(1.0 + jnp.exp(-jnp.abs(x)))


def _log_sigmoid(x):
    return -_softplus(-x)


def _gelu_tanh(x):
    c = math.sqrt(2.0 / math.pi)
    return 0.5 * x * (1.0 + jnp.tanh(c * (x + 0.044715 * (x * x * x))))


def _norm_mod(x, g, sc, sh):
    ms = jnp.mean(x * x, axis=-1, keepdims=True)
    return (x * lax.rsqrt(ms + EPS) * g) * (1.0 + sc) + sh


def _tok_major(ref):
    return pltpu.einshape("rjd->(jr)d", ref[...])


def _params(*sem):
    return pltpu.CompilerParams(dimension_semantics=sem, vmem_limit_bytes=VMEM_LIMIT)


def _ada_kernel(c_ref, w_ref, b_ref, o_ref):
    c = c_ref[...]
    o_ref[0] = _mm(_silu(c), w_ref[0]) + b_ref[0]


def ada_modulation(c, w_ada, b_ada):
    depth, d, d6 = w_ada.shape
    bsz = c.shape[0]
    rows = -(-bsz // 8) * 8
    c_pad = jnp.pad(c, ((0, rows - bsz), (0, 0)))
    out = pl.pallas_call(
        _ada_kernel,
        grid=(depth, d6 // d),
        in_specs=[pl.BlockSpec((rows, d), lambda i, j: (0, 0)),
                  pl.BlockSpec((1, d, d), lambda i, j: (i, 0, j)),
                  pl.BlockSpec((1, 1, d), lambda i, j: (i, 0, j))],
        out_specs=pl.BlockSpec((1, rows, d), lambda i, j: (i, 0, j)),
        out_shape=jax.ShapeDtypeStruct((depth, rows, d6), F32),
        compiler_params=_params("parallel", "parallel"),
        name="ada_modulation",
    )(c_pad, w_ada, b_ada.reshape(depth, 1, d6))
    return out[:, :bsz]


def _inproj_kernel(x_ref, g_ref, sc_ref, sh_ref, w_ref, o_ref, hf_ref, h_ref):
    @pl.when(pl.program_id(1) == 0)
    def _():
        h = _norm_mod(x_ref[...], g_ref[...], sc_ref[...], sh_ref[...])
        hf_ref[...] = h
        h_ref[...] = h.astype(BF16)

    o_ref[...] = jnp.dot(h_ref[...], w_ref[...], preferred_element_type=F32).astype(o_ref.dtype)


def norm_mod_matmul(x2, g, sc, sh, w, layer, seq, tm, tn):
    t, d = x2.shape
    c = w.shape[2]
    per_b = seq // tm
    return pl.pallas_call(
        _inproj_kernel,
        grid=(t // tm, c // tn),
        in_specs=[pl.BlockSpec((tm, d), lambda i, j: (i, 0)),
                  pl.BlockSpec((1, d), lambda i, j: (0, 0)),
                  pl.BlockSpec((None, 1, d), lambda i, j: (i // per_b, 0, 0)),
                  pl.BlockSpec((None, 1, d), lambda i, j: (i // per_b, 0, 0)),
                  pl.BlockSpec((None, d, tn), lambda i, j: (layer, 0, j))],
        out_specs=[pl.BlockSpec((tm, tn), lambda i, j: (i, j)), pl.BlockSpec((tm, d), lambda i, j: (i, 0))],
        out_shape=[jax.ShapeDtypeStruct((t, c), BF16), jax.ShapeDtypeStruct((t, d), F32)],
        scratch_shapes=[pltpu.VMEM((tm, d), BF16)],
        compiler_params=_params("parallel", "arbitrary"),
        name="norm_mod_matmul",
    )(x2, g, sc, sh, w)


def _s5_inproj_kernel(h_ref, wt_ref, o_ref, *, jt):
    rt = h_ref.shape[0]
    h = _tok_major(h_ref).astype(BF16)
    ut = lax.dot_general(wt_ref[...], h, (((1,), (1,)), ((), ())), preferred_element_type=F32)
    for jj in range(jt):
        o_ref[jj] = ut[:, jj * rt:(jj + 1) * rt].astype(BF16)


def s5_inproj(h3, wt, layer, jt, rt):
    r, _, d = h3.shape
    s5w = wt.shape[1]
    return pl.pallas_call(
        functools.partial(_s5_inproj_kernel, jt=jt),
        grid=(r // rt, CHUNK // jt),
        in_specs=[pl.BlockSpec((rt, jt, d), lambda i, j: (i, j, 0)),
                  pl.BlockSpec((None, s5w, d), lambda i, j: (layer, 0, 0))],
        out_specs=pl.BlockSpec((jt, s5w, rt), lambda i, j: (j, 0, i)),
        out_shape=jax.ShapeDtypeStruct((CHUNK, s5w, r), BF16),
        compiler_params=_params("parallel", "parallel"),
        name="s5_inproj",
    )(h3, wt)


S5_CW = CHUNK * S5_GROUP_CH


def _cpow(lr_dt, li_dt, e):
    mag = jnp.exp(lr_dt * e)
    ang = li_dt * e
    return mag * jnp.cos(ang), mag * jnp.sin(ang)


def _s5_param_kernel(ldt_ref, lrc_ref, lic_ref, lrr_ref, lir_ref, bre_ref, bim_ref,
                     c1_ref, c2_ref, dcol_ref, m_ref, et_ref, ft_ref, ac_ref):
    p, cw, h = S5_STATE, S5_CW, S5_GROUP_CH
    dt = jnp.exp(ldt_ref[0])
    lrc, lic = lrc_ref[0], lic_ref[0]
    ab_re, ab_im = _cpow(lrc * dt, lic * dt, 1.0)
    den = lrc * lrc + lic * lic
    nr, ni = ab_re - 1.0, ab_im
    w_re = (nr * lrc + ni * lic) / den
    w_im = (ni * lrc - nr * lic) / den
    trow = lax.broadcasted_iota(jnp.int32, (h, cw), 0)
    tcol = lax.broadcasted_iota(jnp.int32, (h, cw), 1)
    tile_l = (trow == tcol % h).astype(BF16)
    bre, bim = _select_cols(bre_ref[0], tile_l), _select_cols(bim_ref[0], tile_l)
    bb_re = w_re * bre - w_im * bim
    bb_im = w_re * bim + w_im * bre
    lane = lax.broadcasted_iota(jnp.int32, (1, LANE), 1)
    e_m = jnp.where(lane < CHUNK, CHUNK - 1 - lane, 0).astype(F32)
    pd_re, pd_im = _cpow(lrc * dt, lic * dt, e_m)
    xrow = lax.broadcasted_iota(jnp.int32, (LANE, cw), 0)
    xcol = lax.broadcasted_iota(jnp.int32, (LANE, cw), 1)
    expand_l = (xrow == xcol // h).astype(BF16)
    p_re, p_im = _select_cols(pd_re, expand_l), _select_cols(pd_im, expand_l)
    e_re = p_re * bb_re - p_im * bb_im
    e_im = p_re * bb_im + p_im * bb_re
    et = jnp.concatenate([e_re, e_im], axis=0)
    et_ref[0] = et.astype(BF16)
    a_re, a_im = _cpow(lrc * dt, lic * dt, float(CHUNK))
    ac_ref[0] = jnp.concatenate([a_re, a_im], axis=0)
    sgn = jnp.where(lax.broadcasted_iota(jnp.int32, (1, 2 * p), 1) < p, 1.0, -1.0)
    krev = _mm_hi(c1_ref[0] * sgn, et)
    row = lax.broadcasted_iota(jnp.int32, (h, cw), 0)
    col = lax.broadcasted_iota(jnp.int32, (h, cw), 1)
    krev = krev + jnp.where(col == (cw - h) + row, dcol_ref[0], 0.0)
    rrev = jnp.concatenate([krev, jnp.zeros_like(krev)], axis=1)
    per_tile = LANE // h
    rolled = [rrev if r == 0 else pltpu.roll(rrev, 2 * cw - r * h, axis=1) for r in range(per_tile)]
    for i in range(CHUNK):
        s = (CHUNK - 1 - i) * h
        a, r = s // LANE, (s % LANE) // h
        m_ref[0, i * h:(i + 1) * h, :] = rolled[r][:, a * LANE:a * LANE + cw].astype(BF16)
    f_i = (lax.broadcasted_iota(jnp.int32, (CHUNK, 1), 0) + 1).astype(F32)
    qd_re, qd_im = _cpow(lrr_ref[0] * dt, lir_ref[0] * dt, f_i)
    ft = (c1_ref[0] * sgn)[None] * qd_re[:, None, :] - c2_ref[0][None] * qd_im[:, None, :]
    ft_ref[0] = ft.reshape(cw, 2 * p).astype(BF16)


def s5_params(lam_re, lam_im, log_dt, b_re, b_im, c_re, c_im, dpar):
    ng = lam_re.shape[0] * lam_re.shape[1]
    p, h, cw = S5_STATE, S5_GROUP_CH, S5_CW
    lam_re = lam_re.reshape(ng, p)
    lam_im = lam_im.reshape(ng, p)
    c_re = c_re.reshape(ng, h, p)
    c_im = c_im.reshape(ng, h, p)
    c1 = jnp.concatenate([c_re, c_im], axis=-1)
    c2 = jnp.concatenate([c_im, c_re], axis=-1)
    args = (log_dt.reshape(ng, 1, 1),
            lam_re.reshape(ng, p, 1), lam_im.reshape(ng, p, 1),
            jnp.tile(lam_re.reshape(ng, 1, p), (1, 1, 2)), jnp.tile(lam_im.reshape(ng, 1, p), (1, 1, 2)),
            b_re.reshape(ng, p, h), b_im.reshape(ng, p, h), c1, c2,
            dpar.reshape(ng, h, 1))

    def spec(a):
        return pl.BlockSpec((1,) + a.shape[1:], lambda i: (i, 0, 0))

    return pl.pallas_call(
        _s5_param_kernel,
        grid=(ng,),
        in_specs=[spec(a) for a in args],
        out_specs=[pl.BlockSpec((1, cw, cw), lambda i: (i, 0, 0)),
                   pl.BlockSpec((1, 2 * p, cw), lambda i: (i, 0, 0)),
                   pl.BlockSpec((1, cw, 2 * p), lambda i: (i, 0, 0)),
                   pl.BlockSpec((1, 2 * p, 1), lambda i: (i, 0, 0))],
        out_shape=[jax.ShapeDtypeStruct((ng, cw, cw), BF16),
                   jax.ShapeDtypeStruct((ng, 2 * p, cw), BF16),
                   jax.ShapeDtypeStruct((ng, cw, 2 * p), BF16),
                   jax.ShapeDtypeStruct((ng, 2 * p, 1), F32)],
        compiler_params=_params("parallel"),
        name="s5_params",
    )(*args)


def _s5_mix_kernel(u_ref, m_ref, et_ref, ft_ref, ac_ref, y_ref, *, nchunk, gsub):
    p, h = S5_STATE, S5_GROUP_CH
    r = u_ref.shape[-1]
    ks = range(gsub)
    u = [u_ref[:, k * h:(k + 1) * h, :].reshape(S5_CW, r) for k in ks]
    s = [jnp.dot(et_ref[k], u[k], preferred_element_type=F32) for k in ks]
    s_re, s_im = [s[k][:p] for k in ks], [s[k][p:] for k in ks]
    a_re, a_im = [ac_ref[k][:p] for k in ks], [ac_ref[k][p:] for k in ks]
    n_idx = lax.broadcasted_iota(jnp.int32, (1, r), 1) % nchunk
    shift = 1
    while shift < nchunk:
        keep = n_idx >= shift
        t_re = [jnp.where(keep, pltpu.roll(s_re[k], shift, axis=1), 0.0) for k in ks]
        t_im = [jnp.where(keep, pltpu.roll(s_im[k], shift, axis=1), 0.0) for k in ks]
        s_re, s_im = ([s_re[k] + a_re[k] * t_re[k] - a_im[k] * t_im[k] for k in ks],
                      [s_im[k] + a_re[k] * t_im[k] + a_im[k] * t_re[k] for k in ks])
        a_re, a_im = ([a_re[k] * a_re[k] - a_im[k] * a_im[k] for k in ks], [2.0 * a_re[k] * a_im[k] for k in ks])
        shift *= 2
    keep = n_idx >= 1
    for k in ks:
        h_prev = jnp.concatenate([jnp.where(keep, pltpu.roll(s_re[k], 1, axis=1), 0.0),
                                  jnp.where(keep, pltpu.roll(s_im[k], 1, axis=1), 0.0)], axis=0)
        y = jnp.dot(m_ref[k], u[k], preferred_element_type=F32)
        y = y + jnp.dot(ft_ref[k], h_prev.astype(BF16), preferred_element_type=F32)
        y_ref[:, k * h:(k + 1) * h, :] = y.reshape(CHUNK, h, r).astype(y_ref.dtype)


def s5_mix(ut, m, et, ft, ac, layer, nchunk, gsub):
    _, s5w, r = ut.shape
    g, h, p, cw = S5_GROUPS, S5_GROUP_CH, S5_STATE, S5_CW
    base = layer * g // gsub
    return pl.pallas_call(
        functools.partial(_s5_mix_kernel, nchunk=nchunk, gsub=gsub),
        grid=(g // gsub,),
        in_specs=[pl.BlockSpec((CHUNK, gsub * h, r), lambda i: (0, i, 0)),
                  pl.BlockSpec((gsub, cw, cw), lambda i: (base + i, 0, 0)),
                  pl.BlockSpec((gsub, 2 * p, cw), lambda i: (base + i, 0, 0)),
                  pl.BlockSpec((gsub, cw, 2 * p), lambda i: (base + i, 0, 0)),
                  pl.BlockSpec((gsub, 2 * p, 1), lambda i: (base + i, 0, 0))],
        out_specs=pl.BlockSpec((CHUNK, gsub * h, r), lambda i: (0, i, 0)),
        out_shape=jax.ShapeDtypeStruct((CHUNK, s5w, r), BF16),
        compiler_params=_params("parallel"),
        name="s5_mix",
    )(ut, m, et, ft, ac)


def _gla_kernel(q_ref, k_ref, v_ref, og_ref, lr_ref, wlr_ref, blr_ref, ng_ref, o_ref, st_ref, *, nc, nsub):
    @pl.when(pl.program_id(1) == 0)
    def _():
        st_ref[...] = jnp.zeros_like(st_ref)

    tc = nc * CHUNK

    def tile(s, carry):
        rows = pl.ds(pl.multiple_of(s * tc, tc), tc)
        _gla_tile(q_ref.at[rows], k_ref.at[rows], v_ref.at[rows], og_ref.at[rows], lr_ref.at[rows],
                  wlr_ref, blr_ref, ng_ref, o_ref.at[rows], st_ref, nc=nc)
        return carry

    lax.fori_loop(0, nsub, tile, 0)


def _gla_tile(q_ref, k_ref, v_ref, og_ref, lr_ref, wlr_ref, blr_ref, ng_ref, o_ref, st_ref, *, nc):
    hd, dk, dv = GLA_HEADS, GLA_DK, GLA_DV
    ri = lax.broadcasted_iota(jnp.int32, (CHUNK, CHUNK), 0)
    ci = lax.broadcasted_iota(jnp.int32, (CHUNK, CHUNK), 1)
    incl = ri >= ci
    ltri = incl.astype(BF16)
    lane_k = lax.broadcasted_iota(jnp.int32, (1, hd * dk), 1)
    srow = lax.broadcasted_iota(jnp.int32, (hd * dv, hd * dk), 0)
    scol = lax.broadcasted_iota(jnp.int32, (hd * dv, hd * dk), 1)
    same_head = (srow // dv) == (scol // dk)
    wlr, blr, ng = wlr_ref[...], blr_ref[...], ng_ref[...]

    cs = range(nc)
    cr = lambda c: slice(c * CHUNK, (c + 1) * CHUNK)
    g = [_log_sigmoid(_mm(lr_ref[cr(c), :], wlr) + blr) * (1.0 / GLA_GATE_NORM) for c in cs]
    bc = [_select_rows(ltri, g[c]) for c in cs]
    bl = [bc[c][CHUNK - 1:CHUNK, :] for c in cs]
    q_e = [q_ref[cr(c), :].astype(F32) * (dk ** -0.5) * jnp.exp(bc[c]) for c in cs]
    k_e = [k_ref[cr(c), :].astype(F32) * jnp.exp(-bc[c]) for c in cs]
    k_d = [k_ref[cr(c), :].astype(F32) * jnp.exp(bl[c] - bc[c]) for c in cs]
    kv = [jnp.where(same_head, _mm_tn(v_ref[cr(c), :], k_d[c]), 0.0) for c in cs]
    sts = []
    st = st_ref[...]
    for c in cs:
        sts.append(st)
        st = jnp.exp(bl[c]) * st + kv[c]
    st_ref[...] = st
    o_inter = [_mm_nt(q_e[c], sts[c]) for c in cs]
    q_heads = [jnp.concatenate([jnp.where((lane_k // dk) == h, q_e[c], 0.0) for h in range(hd)], axis=0)
               for c in cs]
    sc_all = [_mm_nt(q_heads[c], k_e[c]) for c in cs]
    sc = [[jnp.where(incl, sc_all[c][h * CHUNK:(h + 1) * CHUNK], 0.0) for h in range(hd)] for c in cs]
    for c in cs:
        for h in range(hd):
            cols = slice(h * dv, (h + 1) * dv)
            oh = _mm(sc[c][h], v_ref[cr(c), cols]) + o_inter[c][:, cols]
            ms = jnp.mean(oh * oh, axis=-1, keepdims=True)
            o_ref[cr(c), cols] = (oh * lax.rsqrt(ms + EPS) * ng * _silu(og_ref[cr(c), cols].astype(F32))).astype(o_ref.dtype)


def gla_mix(zm, wlr_pad, blr, ng, bsz, seq, tc, nsub):
    t = zm.shape[0]
    blk_rows = tc * nsub
    per_b = seq // blk_rows
    hd, dk, dv = GLA_HEADS, GLA_DK, GLA_DV

    def zspec(width, off):
        blk = off // width
        return pl.BlockSpec((blk_rows, width), lambda b, i: (b * per_b + i, blk))

    def full(a):
        return pl.BlockSpec(a.shape, lambda b, i: (0,) * a.ndim)

    return pl.pallas_call(
        functools.partial(_gla_kernel, nc=tc // CHUNK, nsub=nsub),
        grid=(bsz, per_b),
        in_specs=[zspec(hd * dk, ZM_GQ), zspec(hd * dk, ZM_GK), zspec(hd * dv, ZM_GV),
                  zspec(hd * dv, ZM_GOG), zspec(LANE, ZM_GLR), full(wlr_pad), full(blr), full(ng)],
        out_specs=pl.BlockSpec((blk_rows, hd * dv), lambda b, i: (b * per_b + i, 0)),
        out_shape=jax.ShapeDtypeStruct((t, hd * dv), F32),
        scratch_shapes=[pltpu.VMEM((hd * dv, hd * dk), F32)],
        compiler_params=_params("parallel", "arbitrary"),
        name="gla_mix",
    )(zm, zm, zm, zm, zm, wlr_pad, blr, ng)


GDN_PAD = 8


def _gdn_kernel(qkv_ref, og_ref, ba_ref, cw_ref, shift_ref, alog_ref, dtb_ref, ng_ref, o_ref,
                head_ref, s_ref, *, nc, nsub):
    @pl.when(pl.program_id(1) == 0)
    def _():
        s_ref[...] = jnp.zeros_like(s_ref)
        head_ref[0:GDN_PAD, :] = jnp.zeros((GDN_PAD, head_ref.shape[1]), F32)

    tc = nc * CHUNK

    def tile(s, carry):
        rows = pl.ds(pl.multiple_of(s * tc, tc), tc)
        _gdn_tile(qkv_ref.at[rows], og_ref.at[rows], ba_ref.at[rows], cw_ref, shift_ref, alog_ref, dtb_ref,
                  ng_ref, o_ref.at[rows], head_ref, s_ref, nc=nc)
        return carry

    lax.fori_loop(0, nsub, tile, 0)


def _gdn_tile(qkv_ref, og_ref, ba_ref, cw_ref, shift_ref, alog_ref, dtb_ref, ng_ref, o_ref,
              head_ref, s_ref, *, nc):
    hd, dk, dv, kc = GDN_HEADS, GDN_DK, GDN_DV, GDN_CONV
    tc = nc * CHUNK
    pad = GDN_PAD

    head_ref[pad:2 * pad, :] = qkv_ref[0:2 * pad, :].astype(F32)[0:pad]
    lagged = jnp.dot(shift_ref[...], qkv_ref[...], preferred_element_type=F32)

    ri = lax.broadcasted_iota(jnp.int32, (CHUNK, CHUNK), 0)
    ci = lax.broadcasted_iota(jnp.int32, (CHUNK, CHUNK), 1)
    incl = ri >= ci
    strict = ri > ci
    ltri = incl.astype(BF16)
    eye = (ri == ci).astype(F32)
    cw = [cw_ref[i] for i in range(kc)]
    ng = ng_ref[...]

    ba = ba_ref[...].astype(F32)
    beta_all = _sigmoid(ba)
    g_all = -jnp.exp(alog_ref[...]) * _softplus(ba + dtb_ref[...])

    cr = lambda c: slice(c * CHUNK, (c + 1) * CHUNK)
    gam_c = [_select_rows(ltri, g_all[cr(c), :]) for c in range(nc)]
    gam_tc = [g.T for g in gam_c]
    groups = [list(range(g0, min(g0 + GDN_GROUP_CHUNKS, nc))) for g0 in range(0, nc, GDN_GROUP_CHUNKS)]

    def conv_piece(chunks, col):
        lanes = slice(col * dk, (col + 1) * dk)
        nr = len(chunks) * CHUNK
        assert chunks[0] == 0 and nr == tc
        blocks = lambda a: a.reshape(nr // 8, 8, dk)
        acc = cw[kc - 1][:, lanes] * blocks(qkv_ref[:, lanes].astype(F32))
        head = cw[kc - 1][:, lanes] * head_ref[pad:2 * pad, lanes]
        for i in range(kc - 1):
            lag = kc - 1 - i
            acc = acc + cw[i][:, lanes] * blocks(lagged[(lag - 1) * tc:lag * tc, lanes])
            head = head + cw[i][:, lanes] * head_ref[pad - lag:2 * pad - lag, lanes]
        acc = jnp.concatenate([head, acc.reshape(nr, dk)[pad:]], axis=0)
        y = _silu(acc)
        if col < 2 * hd:
            y = y * lax.rsqrt(jnp.sum(y * y, axis=-1, keepdims=True) + EPS)
        if col < hd:
            y = y * (dk ** -0.5)
        return [y[i * CHUNK:(i + 1) * CHUNK] for i in range(len(chunks))]

    def chain_stages(chunks, cols, res):
        pairs = [(ci, c, h) for ci, c in enumerate(chunks) for h in range(hd)]
        n = len(pairs)
        q = [cols[h][ci] for ci, c, h in pairs]
        k = [cols[hd + h][ci] for ci, c, h in pairs]
        v = [cols[2 * hd + h][ci] for ci, c, h in pairs]
        beta = [beta_all[cr(c), h:h + 1] for ci, c, h in pairs]
        gam = [gam_c[c][:, hd + h:hd + h + 1] for ci, c, h in pairs]
        gam_row = [gam_tc[c][hd + h:hd + h + 1, :] for ci, c, h in pairs]
        dmask = [jnp.where(incl, jnp.exp(gam[i] - gam_row[i]), 0.0) for i in range(n)]
        k_beta = [k[i] * beta[i] for i in range(n)]
        egam = [jnp.exp(gam[i]) for i in range(n)]
        kk = [_mm_nt(jnp.concatenate([k_beta[i], q[i]], axis=0), k[i]) for i in range(n)]
        low = [jnp.where(strict, kk[i][:CHUNK] * dmask[i], 0.0) for i in range(n)]
        res["attn"] = [kk[i][CHUNK:] * dmask[i] for i in range(n)]
        rhs = [jnp.concatenate([v[i] * beta[i], k_beta[i] * egam[i]], axis=1).astype(BF16) for i in range(n)]
        yield
        lowb = [low[i].astype(BF16) for i in range(n)]
        pw = [_mm(lowb[i], lowb[i]).astype(BF16) for i in range(n)]
        tinv = [eye - low[i] for i in range(n)]
        for s in range(5):
            yield
            if s < 4:
                stk = [_mm(jnp.concatenate([tinv[i].astype(BF16), pw[i]], axis=0), pw[i]) for i in range(n)]
                tinv = [tinv[i] + stk[i][:CHUNK] for i in range(n)]
                pw = [stk[i][CHUNK:].astype(BF16) for i in range(n)]
            else:
                tinv = [tinv[i] + _mm(tinv[i], pw[i]) for i in range(n)]
        yield
        sol = [_mm(tinv[i], rhs[i]).astype(BF16) for i in range(n)]
        q_dec = [q[i] * egam[i] for i in range(n)]
        gam_last = [gam[i][CHUNK - 1:CHUNK, :] for i in range(n)]
        k_dec = [k[i] * jnp.exp(gam_last[i] - gam[i]) for i in range(n)]
        yield
        ks = [_mm_tn(k_dec[i], sol[i]) for i in range(n)]
        aw = [_mm(res["attn"][i], sol[i]) for i in range(n)]
        res["lhs"] = [jnp.concatenate([ks[i][:, dv:], q_dec[i] - aw[i][:, dv:]], axis=0).astype(BF16)
                      for i in range(n)]
        res["s_add"] = [ks[i][:, :dv] for i in range(n)]
        res["o_add"] = [aw[i][:, :dv] for i in range(n)]
        res["decay"] = [jnp.exp(gam_last[i]) for i in range(n)]

    ncols = 3 * hd
    cols = [conv_piece(groups[0], col) for col in range(ncols)]
    st = [s_ref[h] for h in range(hd)]
    for gi, chunks in enumerate(groups):
        res = {}
        nxt = groups[gi + 1] if gi + 1 < len(groups) else None
        nxt_cols = []
        for _ in chain_stages(chunks, cols, res):
            if nxt is not None and len(nxt_cols) < ncols:
                nxt_cols.append(conv_piece(nxt, len(nxt_cols)))
        while nxt is not None and len(nxt_cols) < ncols:
            nxt_cols.append(conv_piece(nxt, len(nxt_cols)))
        cols = nxt_cols
        lhs, s_add, o_add, decay = res["lhs"], res["s_add"], res["o_add"], res["decay"]
        for ci, c in enumerate(chunks):
            ids = [ci * hd + h for h in range(hd)]
            prod = [_mm(lhs[i], st[h]) for h, i in enumerate(ids)]
            o = [prod[h][dk:] + o_add[i] for h, i in enumerate(ids)]
            st = [decay[i] * st[h] - prod[h][:dk] + s_add[i] for h, i in enumerate(ids)]
            for h in range(hd):
                ms = jnp.mean(o[h] * o[h], axis=-1, keepdims=True)
                cl = slice(h * dv, (h + 1) * dv)
                o_ref[cr(c), cl] = (o[h] * lax.rsqrt(ms + EPS) * ng * _silu(og_ref[cr(c), cl].astype(F32))).astype(o_ref.dtype)
    for h in range(hd):
        s_ref[h] = st[h]
    head_ref[0:pad, :] = qkv_ref[tc - 2 * pad:tc, :].astype(F32)[pad:]


def gdn_mix(zm, conv_w_pad, shift_m, alog_row, dtb_row, ng, bsz, seq, tc, nsub):
    t = zm.shape[0]
    blk_rows = tc * nsub
    per_b = seq // blk_rows
    hd, dk, dv = GDN_HEADS, GDN_DK, GDN_DV
    qkv_w = 3 * hd * dk

    def zspec(width, off):
        blk = off // width
        return pl.BlockSpec((blk_rows, width), lambda b, i: (b * per_b + i, blk))

    def full(a):
        return pl.BlockSpec(a.shape, lambda b, i: (0,) * a.ndim)

    return pl.pallas_call(
        functools.partial(_gdn_kernel, nc=tc // CHUNK, nsub=nsub),
        grid=(bsz, per_b),
        in_specs=[zspec(qkv_w, ZM_DQKV), zspec(hd * dv, ZM_DOG), zspec(LANE, ZM_DBA),
                  full(conv_w_pad), full(shift_m), full(alog_row), full(dtb_row), full(ng)],
        out_specs=pl.BlockSpec((blk_rows, hd * dv), lambda b, i: (b * per_b + i, 0)),
        out_shape=jax.ShapeDtypeStruct((t, hd * dv), F32),
        scratch_shapes=[pltpu.VMEM((2 * GDN_PAD, qkv_w), F32), pltpu.VMEM((hd, dk, dv), F32)],
        compiler_params=_params("parallel", "arbitrary"),
        name="gdn_mix",
    )(zm, zm, zm, conv_w_pad, shift_m, alog_row, dtb_row, ng)


def _merge_kernel(h3_ref, ygla_ref, ygdn_ref, yst_ref, wglu_ref, wzg_ref, wbg_ref, wbs_ref, wbd_ref, o_ref,
                  s5t_ref, *, jt, bw, tn):
    n_rows = jt * LANE
    d = h3_ref.shape[-1]
    h = h3_ref[...].reshape(n_rows, d).astype(BF16)
    a_gla = ygla_ref[...].reshape(n_rows, bw).astype(BF16)
    a_gdn = ygdn_ref[...].reshape(n_rows, bw).astype(BF16)
    for jj in range(jt):
        ys = _gelu_tanh(yst_ref[jj].astype(F32))
        glu = _mm_tn(ys, wglu_ref[...])
        s5t_ref[jj * LANE:(jj + 1) * LANE, :] = glu[:, :bw] * _sigmoid(glu[:, bw:])
    a_s5 = pltpu.einshape("(jr)d->rjd", s5t_ref[...], j=jt).reshape(n_rows, bw).astype(BF16)

    def branch(b, cols, a, wb_ref):
        gate = _sigmoid(jnp.dot(h, wzg_ref[:, b * d + cols.start:b * d + cols.stop], preferred_element_type=F32))
        return gate * jnp.dot(a, wb_ref[:, cols], preferred_element_type=F32)

    for c0 in range(0, d, tn):
        cols = slice(c0, c0 + tn)
        m = branch(0, cols, a_gla, wbg_ref) + branch(1, cols, a_s5, wbs_ref) + branch(2, cols, a_gdn, wbd_ref)
        o_ref[:, :, cols] = m.reshape(LANE, jt, tn)


def merge_gate(h3, ygla3, ygdn3, yst, wglu, wzg, wbg, wbs, wbd, layer, jt, tn):
    r, _, d = h3.shape
    bw = ygla3.shape[2]

    def view(width):
        return pl.BlockSpec((LANE, jt, width), lambda i, j: (i, j, 0))

    def layer_w(a):
        return pl.BlockSpec((None,) + a.shape[1:], lambda i, j: (layer, 0, 0))

    return pl.pallas_call(
        functools.partial(_merge_kernel, jt=jt, bw=bw, tn=tn),
        grid=(r // LANE, CHUNK // jt),
        in_specs=[view(d), view(bw), view(bw),
                  pl.BlockSpec((jt, yst.shape[1], LANE), lambda i, j: (j, 0, i)),
                  layer_w(wglu), layer_w(wzg), layer_w(wbg), layer_w(wbs), layer_w(wbd)],
        out_specs=view(d),
        out_shape=jax.ShapeDtypeStruct((r, CHUNK, d), F32),
        scratch_shapes=[pltpu.VMEM((jt * LANE, bw), F32)],
        compiler_params=_params("parallel", "parallel"),
        name="merge_gate",
    )(h3, ygla3, ygdn3, yst, wglu, wzg, wbg, wbs, wbd)


def _ffn_kernel(x_ref, m_ref, wout_ref, gt1_ref, g_ref, sc_ref, sh_ref, gt2_ref, win_ref, wo_ref,
                fg_ref, o_ref, *, final_norm, tf):
    dff = wo_ref.shape[0]
    x1 = x_ref[...] + gt1_ref[...] * _mm(m_ref[...], wout_ref[...])
    h = _norm_mod(x1, g_ref[...], sc_ref[...], sh_ref[...]).astype(BF16)
    acc = None
    for f0 in range(0, dff, tf):
        a = jnp.dot(h, win_ref[:, f0:f0 + tf], preferred_element_type=F32)
        b = jnp.dot(h, win_ref[:, dff + f0:dff + f0 + tf], preferred_element_type=F32)
        part = _mm(_silu(a) * b, wo_ref[f0:f0 + tf, :])
        acc = part if acc is None else acc + part
    y = x1 + gt2_ref[...] * acc
    if final_norm:
        ms = jnp.mean(y * y, axis=-1, keepdims=True)
        y = y * lax.rsqrt(ms + EPS) * fg_ref[...]
    o_ref[...] = y


def ffn_block(x2, merged, w_o, gt1, g, sc, sh, gt2, w_in, w_out, final_g, layer, seq, tm, tf, final_norm):
    t, d = x2.shape
    dff = w_out.shape[1]
    per_b = seq // tm

    def bspec():
        return pl.BlockSpec((None, 1, d), lambda i: (i // per_b, 0, 0))

    def const(a):
        return pl.BlockSpec(a.shape, lambda i: (0,) * a.ndim)

    def layer_w(a):
        return pl.BlockSpec((None,) + a.shape[1:], lambda i: (layer, 0, 0))

    row = pl.BlockSpec((tm, d), lambda i: (i, 0))
    return pl.pallas_call(
        functools.partial(_ffn_kernel, final_norm=final_norm, tf=tf),
        grid=(t // tm,),
        in_specs=[row, row, layer_w(w_o), bspec(), const(g), bspec(), bspec(), bspec(),
                  layer_w(w_in), layer_w(w_out), const(final_g)],
        out_specs=row,
        out_shape=jax.ShapeDtypeStruct((t, d), F32),
        compiler_params=_params("parallel"),
        name="ffn_block",
    )(x2, merged, w_o, gt1, g, sc, sh, gt2, w_in, w_out, final_g)


def _wprep_kernel(w_ref, zm_ref, zg_ref, s5t_ref, *, segs, pads, zg_off, s5_off):
    for dst, width in pads:
        zm_ref[0, :, dst:dst + width] = jnp.zeros((zm_ref.shape[1], width), BF16)
    for src, width, dst in segs:
        zm_ref[0, :, dst:dst + width] = w_ref[0, :, src:src + width].astype(BF16)
    zg_ref[0] = w_ref[0, :, zg_off:zg_off + zg_ref.shape[2]].astype(BF16)
    s5t_ref[0] = w_ref[0, :, s5_off:s5_off + s5t_ref.shape[1]].astype(F32).T.astype(BF16)


def prep_in_weights(w_in, segs, pads, zg_off, zg_w, s5_off, s5_w, tr):
    depth, d, d_in = w_in.shape
    return pl.pallas_call(
        functools.partial(_wprep_kernel, segs=segs, pads=pads, zg_off=zg_off, s5_off=s5_off),
        grid=(depth, d // tr),
        in_specs=[pl.BlockSpec((1, tr, d_in), lambda i, j: (i, j, 0))],
        out_specs=[pl.BlockSpec((1, tr, ZM_WIDTH), lambda i, j: (i, j, 0)),
                   pl.BlockSpec((1, tr, zg_w), lambda i, j: (i, j, 0)),
                   pl.BlockSpec((1, s5_w, tr), lambda i, j: (i, 0, j))],
        out_shape=[jax.ShapeDtypeStruct((depth, d, ZM_WIDTH), BF16),
                   jax.ShapeDtypeStruct((depth, d, zg_w), BF16),
                   jax.ShapeDtypeStruct((depth, s5_w, d), BF16)],
        compiler_params=_params("parallel", "parallel"),
        name="prep_in_weights",
    )(w_in)


def _forward(x, c, w_ada, b_ada, norm1_g, w_in, gla_w_lr, gla_b_lr, gla_norm_g,
             s5_lambda_re, s5_lambda_im, s5_log_dt, s5_b_re, s5_b_im, s5_c_re, s5_c_im,
             s5_d, s5_w_glu, gdn_conv_w, gdn_a_log, gdn_dt_bias, gdn_norm_g,
             w_branch_gla, w_branch_s5, w_branch_gdn, w_out, norm2_g, w_ffn_in, w_ffn_out,
             final_g):
    bsz, seq, d = x.shape
    depth = w_ada.shape[0]
    t = bsz * seq
    nchunk = seq // CHUNK
    r = bsz * nchunk
    gqk, gw = GLA_HEADS * GLA_DK, GLA_HEADS * GLA_DV
    s5w = S5_GROUPS * S5_GROUP_CH
    dqkv, dw = 3 * GDN_HEADS * GDN_DK, GDN_HEADS * GDN_DV

    mod = ada_modulation(c, w_ada, b_ada).reshape(depth, bsz, 6, 1, d)
    s5m, s5et, s5ft, s5ac = s5_params(s5_lambda_re, s5_lambda_im, s5_log_dt, s5_b_re, s5_b_im,
                                      s5_c_re, s5_c_im, s5_d)

    o_gq, o_gk, o_gv = 0, gqk, 2 * gqk
    o_glr = o_gv + gw
    o_gog = o_glr + GLA_LOWRANK
    o_s5 = o_gog + gw
    o_dqkv = o_s5 + s5w
    o_dbeta = o_dqkv + dqkv
    o_da = o_dbeta + GDN_HEADS
    o_dog = o_da + GDN_HEADS
    o_zg = o_dog + dw

    segs = ((o_gq, gqk, ZM_GQ), (o_gk, gqk, ZM_GK), (o_gv, gw, ZM_GV), (o_gog, gw, ZM_GOG),
            (o_dqkv, dqkv, ZM_DQKV), (o_dog, dw, ZM_DOG), (o_glr, GLA_LOWRANK, ZM_GLR),
            (o_dbeta, 2 * GDN_HEADS, ZM_DBA))
    pads = ((ZM_GLR, LANE), (ZM_DBA, LANE))
    w_in_b = jnp.pad(w_in.astype(BF16), ((0, 0), (0, 0), (0, -w_in.shape[2] % LANE)))
    w_zm_all, w_zg_all, w_s5t_all = prep_in_weights(w_in_b, segs, pads, o_zg, N_BRANCH * d, o_s5, s5w, tr=256)
    s5_w_glu, w_branch_gla, w_branch_s5, w_branch_gdn, w_out, w_ffn_in, w_ffn_out = [
        a.astype(BF16) for a in (s5_w_glu, w_branch_gla, w_branch_s5, w_branch_gdn, w_out, w_ffn_in, w_ffn_out)]

    gdn_tc = 256
    lag_row = jnp.arange((GDN_CONV - 1) * gdn_tc)
    gdn_shift = (jnp.arange(gdn_tc)[None, :]
                 == (lag_row % gdn_tc - lag_row // gdn_tc - 1)[:, None]).astype(BF16)

    x2 = x.reshape(t, d)
    for i in range(depth):

        sh1, sc1, gt1, sh2, sc2, gt2 = [mod[i, :, k] for k in range(6)]
        g1 = norm1_g[i].reshape(1, d)

        zm, hf = norm_mod_matmul(x2, g1, sc1, sh1, w_zm_all, i, seq, tm=512, tn=3840)
        h3 = hf.reshape(r, CHUNK, d)
        ut = s5_inproj(h3, w_s5t_all, i, jt=8, rt=r)

        wlr_pad = jnp.pad(gla_w_lr[i], ((0, LANE - GLA_LOWRANK), (0, 0)))
        y_gla = gla_mix(zm, wlr_pad, gla_b_lr[i].reshape(1, gqk), gla_norm_g[i].reshape(1, GLA_DV),
                        bsz, seq, tc=256, nsub=4)
        yst = s5_mix(ut, s5m, s5et, s5ft, s5ac, i, nchunk, gsub=4)
        conv_pad = jnp.broadcast_to(gdn_conv_w[i][:, None, :], (GDN_CONV, 8, dqkv))
        alog_row = jnp.pad(gdn_a_log[i], (GDN_HEADS, LANE - 2 * GDN_HEADS)).reshape(1, LANE)
        dtb_row = jnp.pad(gdn_dt_bias[i], (GDN_HEADS, LANE - 2 * GDN_HEADS)).reshape(1, LANE)
        y_gdn = gdn_mix(zm, conv_pad, gdn_shift, alog_row, dtb_row, gdn_norm_g[i].reshape(1, GDN_DV),
                        bsz, seq, tc=gdn_tc, nsub=4)

        merged = merge_gate(h3, y_gla.reshape(r, CHUNK, gw), y_gdn.reshape(r, CHUNK, dw),
                            yst, s5_w_glu, w_zg_all, w_branch_gla, w_branch_s5, w_branch_gdn, i,
                            jt=8, tn=256)
        x2 = ffn_block(x2, merged.reshape(t, d), w_out, gt1,
                       norm2_g[i].reshape(1, d), sc2, sh2, gt2,
                       w_ffn_in, w_ffn_out,
                       final_g.reshape(1, d), i, seq, tm=512, tf=256, final_norm=(i == depth - 1))
    return x2.reshape(bsz, seq, d)


def kernel(x, c, w_ada, b_ada, norm1_g, w_in, gla_w_lr, gla_b_lr, gla_norm_g, s5_lambda_re, s5_lambda_im, s5_log_dt, s5_b_re, s5_b_im, s5_c_re, s5_c_im, s5_d, s5_w_glu, gdn_conv_w, gdn_a_log, gdn_dt_bias, gdn_norm_g, w_branch_gla, w_branch_s5, w_branch_gdn, w_out, norm2_g, w_ffn_in, w_ffn_out, final_g):
    return _forward(x, c, w_ada, b_ada, norm1_g, w_in, gla_w_lr, gla_b_lr, gla_norm_g,
                    s5_lambda_re, s5_lambda_im, s5_log_dt, s5_b_re, s5_b_im, s5_c_re, s5_c_im,
                    s5_d, s5_w_glu, gdn_conv_w, gdn_a_log, gdn_dt_bias, gdn_norm_g,
                    w_branch_gla, w_branch_s5, w_branch_gdn, w_out, norm2_g, w_ffn_in, w_ffn_out,
                    final_g)
```

```python
import functools
import math

import jax
import jax.numpy as jnp
from jax import lax
from jax.experimental import pallas as pl
from jax.experimental.pallas import tpu as pltpu

F32 = jnp.float32
BF16 = jnp.bfloat16
HI = lax.Precision.HIGHEST

EPS = 1e-6
CHUNK = 64
LANE = 128
VMEM_LIMIT = 56 * 1024 * 1024

GLA_HEADS, GLA_DK, GLA_DV, GLA_LOWRANK = 4, 64, 128, 16
GLA_GATE_NORM = 16.0
S5_GROUPS, S5_GROUP_CH, S5_STATE = 32, 16, 64
GDN_HEADS, GDN_DK, GDN_DV, GDN_CONV = 4, 128, 128, 4
N_BRANCH = 3
GDN_GROUP_CHUNKS = 4

ZM_GQ, ZM_GK, ZM_GV, ZM_GOG = 0, 256, 512, 1024
ZM_DQKV, ZM_DOG, ZM_GLR, ZM_DBA = 1536, 3072, 3584, 3712
ZM_WIDTH = 3840


def _mm(a, b):
    return jnp.dot(a.astype(BF16), b.astype(BF16), preferred_element_type=F32)


def _mm_nt(a, b):
    return lax.dot_general(a.astype(BF16), b.astype(BF16), (((1,), (1,)), ((), ())),
                           preferred_element_type=F32)


def _mm_tn(a, b):
    return lax.dot_general(a.astype(BF16), b.astype(BF16), (((0,), (0,)), ((), ())),
                           preferred_element_type=F32)


def _mm_hi(a, b):
    return jnp.dot(a, b, precision=HI, preferred_element_type=F32)


def _split3(x):
    hi = x.astype(BF16)
    r1 = x - hi.astype(F32)
    mid = r1.astype(BF16)
    return hi, mid, (r1 - mid.astype(F32)).astype(BF16)


def _select_rows(m01, x):
    return sum(jnp.dot(m01, t, preferred_element_type=F32) for t in _split3(x))


def _select_cols(x, m01):
    return sum(jnp.dot(t, m01, preferred_element_type=F32) for t in _split3(x))


def _sigmoid(x):
    return 0.5 + 0.5 * jnp.tanh(0.5 * x)


def _silu(x):
    h = 0.5 * x
    return h + h * jnp.tanh(h)


def _softplus(x):
    return jnp.maximum(x, 0.0) + jnp.log(1.0 + jnp.exp(-jnp.abs(x)))


def _log_sigmoid(x):
    return -_softplus(-x)


def _gelu_tanh(x):
    c = math.sqrt(2.0 / math.pi)
    return 0.5 * x * (1.0 + jnp.tanh(c * (x + 0.044715 * (x * x * x))))


def _norm_mod(x, g, sc, sh):
    ms = jnp.mean(x * x, axis=-1, keepdims=True)
    return (x * lax.rsqrt(ms + EPS) * g) * (1.0 + sc) + sh


def _tok_major(ref):
    return pltpu.einshape("rjd->(jr)d", ref[...])


def _params(*sem):
    return pltpu.CompilerParams(dimension_semantics=sem, vmem_limit_bytes=VMEM_LIMIT)


def _ada_kernel(c_ref, w_ref, b_ref, o_ref):
    c = c_ref[...]
    o_ref[0] = _mm(_silu(c), w_ref[0]) + b_ref[0]


def ada_modulation(c, w_ada, b_ada):
    depth, d, d6 = w_ada.shape
    bsz = c.shape[0]
    rows = -(-bsz // 8) * 8
    c_pad = jnp.pad(c, ((0, rows - bsz), (0, 0)))
    out = pl.pallas_call(
        _ada_kernel,
        grid=(depth, d6 // d),
        in_specs=[pl.BlockSpec((rows, d), lambda i, j: (0, 0)),
                  pl.BlockSpec((1, d, d), lambda i, j: (i, 0, j)),
                  pl.BlockSpec((1, 1, d), lambda i, j: (i, 0, j))],
        out_specs=pl.BlockSpec((1, rows, d), lambda i, j: (i, 0, j)),
        out_shape=jax.ShapeDtypeStruct((depth, rows, d6), F32),
        compiler_params=_params("parallel", "parallel"),
        name="ada_modulation",
    )(c_pad, w_ada, b_ada.reshape(depth, 1, d6))
    return out[:, :bsz]


def _inproj_kernel(x_ref, g_ref, sc_ref, sh_ref, w_ref, o_ref, hf_ref, h_ref):
    @pl.when(pl.program_id(1) == 0)
    def _():
        h = _norm_mod(x_ref[...], g_ref[...], sc_ref[...], sh_ref[...])
        hf_ref[...] = h
        h_ref[...] = h.astype(BF16)

    o_ref[...] = jnp.dot(h_ref[...], w_ref[...], preferred_element_type=F32).astype(o_ref.dtype)


def norm_mod_matmul(x2, g, sc, sh, w, layer, seq, tm, tn):
    t, d = x2.shape
    c = w.shape[2]
    per_b = seq // tm
    return pl.pallas_call(
        _inproj_kernel,
        grid=(t // tm, c // tn),
        in_specs=[pl.BlockSpec((tm, d), lambda i, j: (i, 0)),
                  pl.BlockSpec((1, d), lambda i, j: (0, 0)),
                  pl.BlockSpec((None, 1, d), lambda i, j: (i // per_b, 0, 0)),
                  pl.BlockSpec((None, 1, d), lambda i, j: (i // per_b, 0, 0)),
                  pl.BlockSpec((None, d, tn), lambda i, j: (layer, 0, j))],
        out_specs=[pl.BlockSpec((tm, tn), lambda i, j: (i, j)), pl.BlockSpec((tm, d), lambda i, j: (i, 0))],
        out_shape=[jax.ShapeDtypeStruct((t, c), BF16), jax.ShapeDtypeStruct((t, d), F32)],
        scratch_shapes=[pltpu.VMEM((tm, d), BF16)],
        compiler_params=_params("parallel", "arbitrary"),
        name="norm_mod_matmul",
    )(x2, g, sc, sh, w)


def _s5_inproj_kernel(h_ref, wt_ref, o_ref, *, jt):
    rt = h_ref.shape[0]
    h = _tok_major(h_ref).astype(BF16)
    ut = lax.dot_general(wt_ref[...], h, (((1,), (1,)), ((), ())), preferred_element_type=F32)
    for jj in range(jt):
        o_ref[jj] = ut[:, jj * rt:(jj + 1) * rt].astype(BF16)


def s5_inproj(h3, wt, layer, jt, rt):
    r, _, d = h3.shape
    s5w = wt.shape[1]
    return pl.pallas_call(
        functools.partial(_s5_inproj_kernel, jt=jt),
        grid=(r // rt, CHUNK // jt),
        in_specs=[pl.BlockSpec((rt, jt, d), lambda i, j: (i, j, 0)),
                  pl.BlockSpec((None, s5w, d), lambda i, j: (layer, 0, 0))],
        out_specs=pl.BlockSpec((jt, s5w, rt), lambda i, j: (j, 0, i)),
        out_shape=jax.ShapeDtypeStruct((CHUNK, s5w, r), BF16),
        compiler_params=_params("parallel", "parallel"),
        name="s5_inproj",
    )(h3, wt)


S5_CW = CHUNK * S5_GROUP_CH


def _cpow(lr_dt, li_dt, e):
    mag = jnp.exp(lr_dt * e)
    ang = li_dt * e
    return mag * jnp.cos(ang), mag * jnp.sin(ang)


def _s5_param_kernel(ldt_ref, lrc_ref, lic_ref, lrr_ref, lir_ref, bre_ref, bim_ref,
                     c1_ref, c2_ref, dcol_ref, m_ref, et_ref, ft_ref, ac_ref):
    p, cw, h = S5_STATE, S5_CW, S5_GROUP_CH
    dt = jnp.exp(ldt_ref[0])
    lrc, lic = lrc_ref[0], lic_ref[0]
    ab_re, ab_im = _cpow(lrc * dt, lic * dt, 1.0)
    den = lrc * lrc + lic * lic
    nr, ni = ab_re - 1.0, ab_im
    w_re = (nr * lrc + ni * lic) / den
    w_im = (ni * lrc - nr * lic) / den
    trow = lax.broadcasted_iota(jnp.int32, (h, cw), 0)
    tcol = lax.broadcasted_iota(jnp.int32, (h, cw), 1)
    tile_l = (trow == tcol % h).astype(BF16)
    bre, bim = _select_cols(bre_ref[0], tile_l), _select_cols(bim_ref[0], tile_l)
    bb_re = w_re * bre - w_im * bim
    bb_im = w_re * bim + w_im * bre
    lane = lax.broadcasted_iota(jnp.int32, (1, LANE), 1)
    e_m = jnp.where(lane < CHUNK, CHUNK - 1 - lane, 0).astype(F32)
    pd_re, pd_im = _cpow(lrc * dt, lic * dt, e_m)
    xrow = lax.broadcasted_iota(jnp.int32, (LANE, cw), 0)
    xcol = lax.broadcasted_iota(jnp.int32, (LANE, cw), 1)
    expand_l = (xrow == xcol // h).astype(BF16)
    p_re, p_im = _select_cols(pd_re, expand_l), _select_cols(pd_im, expand_l)
    e_re = p_re * bb_re - p_im * bb_im
    e_im = p_re * bb_im + p_im * bb_re
    et = jnp.concatenate([e_re, e_im], axis=0)
    et_ref[0] = et.astype(BF16)
    a_re, a_im = _cpow(lrc * dt, lic * dt, float(CHUNK))
    ac_ref[0] = jnp.concatenate([a_re, a_im], axis=0)
    sgn = jnp.where(lax.broadcasted_iota(jnp.int32, (1, 2 * p), 1) < p, 1.0, -1.0)
    krev = _mm_hi(c1_ref[0] * sgn, et)
    row = lax.broadcasted_iota(jnp.int32, (h, cw), 0)
    col = lax.broadcasted_iota(jnp.int32, (h, cw), 1)
    krev = krev + jnp.where(col == (cw - h) + row, dcol_ref[0], 0.0)
    rrev = jnp.concatenate([krev, jnp.zeros_like(krev)], axis=1)
    per_tile = LANE // h
    rolled = [rrev if r == 0 else pltpu.roll(rrev, 2 * cw - r * h, axis=1) for r in range(per_tile)]
    for i in range(CHUNK):
        s = (CHUNK - 1 - i) * h
        a, r = s // LANE, (s % LANE) // h
        m_ref[0, i * h:(i + 1) * h, :] = rolled[r][:, a * LANE:a * LANE + cw].astype(BF16)
    f_i = (lax.broadcasted_iota(jnp.int32, (CHUNK, 1), 0) + 1).astype(F32)
    qd_re, qd_im = _cpow(lrr_ref[0] * dt, lir_ref[0] * dt, f_i)
    ft = (c1_ref[0] * sgn)[None] * qd_re[:, None, :] - c2_ref[0][None] * qd_im[:, None, :]
    ft_ref[0] = ft.reshape(cw, 2 * p).astype(BF16)


def s5_params(lam_re, lam_im, log_dt, b_re, b_im, c_re, c_im, dpar):
    ng = lam_re.shape[0] * lam_re.shape[1]
    p, h, cw = S5_STATE, S5_GROUP_CH, S5_CW
    lam_re = lam_re.reshape(ng, p)
    lam_im = lam_im.reshape(ng, p)
    c_re = c_re.reshape(ng, h, p)
    c_im = c_im.reshape(ng, h, p)
    c1 = jnp.concatenate([c_re, c_im], axis=-1)
    c2 = jnp.concatenate([c_im, c_re], axis=-1)
    args = (log_dt.reshape(ng, 1, 1),
            lam_re.reshape(ng, p, 1), lam_im.reshape(ng, p, 1),
            jnp.tile(lam_re.reshape(ng, 1, p), (1, 1, 2)), jnp.tile(lam_im.reshape(ng, 1, p), (1, 1, 2)),
            b_re.reshape(ng, p, h), b_im.reshape(ng, p, h), c1, c2,
            dpar.reshape(ng, h, 1))

    def spec(a):
        return pl.BlockSpec((1,) + a.shape[1:], lambda i: (i, 0, 0))

    return pl.pallas_call(
        _s5_param_kernel,
        grid=(ng,),
        in_specs=[spec(a) for a in args],
        out_specs=[pl.BlockSpec((1, cw, cw), lambda i: (i, 0, 0)),
                   pl.BlockSpec((1, 2 * p, cw), lambda i: (i, 0, 0)),
                   pl.BlockSpec((1, cw, 2 * p), lambda i: (i, 0, 0)),
                   pl.BlockSpec((1, 2 * p, 1), lambda i: (i, 0, 0))],
        out_shape=[jax.ShapeDtypeStruct((ng, cw, cw), BF16),
                   jax.ShapeDtypeStruct((ng, 2 * p, cw), BF16),
                   jax.ShapeDtypeStruct((ng, cw, 2 * p), BF16),
                   jax.ShapeDtypeStruct((ng, 2 * p, 1), F32)],
        compiler_params=_params("parallel"),
        name="s5_params",
    )(*args)


S5_MBLOCKS = 4


def _s5_mix_kernel(u_ref, *refs, nchunk, gsub):
    m_refs, (et_ref, ft_ref, ac_ref, y_ref) = refs[:S5_MBLOCKS], refs[S5_MBLOCKS:]
    p, h = S5_STATE, S5_GROUP_CH
    r = u_ref.shape[-1]
    ks = range(gsub)
    u = [u_ref[:, k * h:(k + 1) * h, :].reshape(S5_CW, r) for k in ks]
    s = [jnp.dot(et_ref[k], u[k], preferred_element_type=F32) for k in ks]
    s_re, s_im = [s[k][:p] for k in ks], [s[k][p:] for k in ks]
    a_re, a_im = [ac_ref[k][:p] for k in ks], [ac_ref[k][p:] for k in ks]
    n_idx = lax.broadcasted_iota(jnp.int32, (1, r), 1) % nchunk
    shift = 1
    while shift < nchunk:
        keep = n_idx >= shift
        t_re = [jnp.where(keep, pltpu.roll(s_re[k], shift, axis=1), 0.0) for k in ks]
        t_im = [jnp.where(keep, pltpu.roll(s_im[k], shift, axis=1), 0.0) for k in ks]
        s_re, s_im = ([s_re[k] + a_re[k] * t_re[k] - a_im[k] * t_im[k] for k in ks],
                      [s_im[k] + a_re[k] * t_im[k] + a_im[k] * t_re[k] for k in ks])
        a_re, a_im = ([a_re[k] * a_re[k] - a_im[k] * a_im[k] for k in ks], [2.0 * a_re[k] * a_im[k] for k in ks])
        shift *= 2
    keep = n_idx >= 1
    for k in ks:
        h_prev = jnp.concatenate([jnp.where(keep, pltpu.roll(s_re[k], 1, axis=1), 0.0),
                                  jnp.where(keep, pltpu.roll(s_im[k], 1, axis=1), 0.0)], axis=0)
        rb_rows = S5_CW // S5_MBLOCKS
        y = jnp.concatenate([jnp.dot(m_refs[rb][k], u[k][:(rb + 1) * rb_rows], preferred_element_type=F32)
                             for rb in range(S5_MBLOCKS)], axis=0)
        y = y + jnp.dot(ft_ref[k], h_prev.astype(BF16), preferred_element_type=F32)
        y_ref[:, k * h:(k + 1) * h, :] = y.reshape(CHUNK, h, r).astype(y_ref.dtype)


def s5_mix(ut, m, et, ft, ac, layer, nchunk, gsub):
    _, s5w, r = ut.shape
    g, h, p, cw = S5_GROUPS, S5_GROUP_CH, S5_STATE, S5_CW
    base = layer * g // gsub
    return pl.pallas_call(
        functools.partial(_s5_mix_kernel, nchunk=nchunk, gsub=gsub),
        grid=(g // gsub,),
        in_specs=[pl.BlockSpec((CHUNK, gsub * h, r), lambda i: (0, i, 0)),
                  *[pl.BlockSpec((gsub, cw // S5_MBLOCKS, (rb + 1) * cw // S5_MBLOCKS),
                                 lambda i, rb=rb: (base + i, rb, 0)) for rb in range(S5_MBLOCKS)],
                  pl.BlockSpec((gsub, 2 * p, cw), lambda i: (base + i, 0, 0)),
                  pl.BlockSpec((gsub, cw, 2 * p), lambda i: (base + i, 0, 0)),
                  pl.BlockSpec((gsub, 2 * p, 1), lambda i: (base + i, 0, 0))],
        out_specs=pl.BlockSpec((CHUNK, gsub * h, r), lambda i: (0, i, 0)),
        out_shape=jax.ShapeDtypeStruct((CHUNK, s5w, r), BF16),
        compiler_params=_params("parallel"),
        name="s5_mix",
    )(ut, *([m] * S5_MBLOCKS), et, ft, ac)


def _gla_kernel(q_ref, k_ref, v_ref, og_ref, lr_ref, wlr_ref, blr_ref, ng_ref, o_ref, st_ref, *, nc, nsub):
    @pl.when(pl.program_id(1) == 0)
    def _():
        st_ref[...] = jnp.zeros_like(st_ref)

    tc = nc * CHUNK

    def tile(s, carry):
        rows = pl.ds(pl.multiple_of(s * tc, tc), tc)
        _gla_tile(q_ref.at[rows], k_ref.at[rows], v_ref.at[rows], og_ref.at[rows], lr_ref.at[rows],
                  wlr_ref, blr_ref, ng_ref, o_ref.at[rows], st_ref, nc=nc)
        return carry

    lax.fori_loop(0, nsub, tile, 0)


def _gla_tile(q_ref, k_ref, v_ref, og_ref, lr_ref, wlr_ref, blr_ref, ng_ref, o_ref, st_ref, *, nc):
    hd, dk, dv = GLA_HEADS, GLA_DK, GLA_DV
    ri = lax.broadcasted_iota(jnp.int32, (CHUNK, CHUNK), 0)
    ci = lax.broadcasted_iota(jnp.int32, (CHUNK, CHUNK), 1)
    incl = ri >= ci
    ltri = incl.astype(BF16)
    lane_k = lax.broadcasted_iota(jnp.int32, (1, hd * dk), 1)
    srow = lax.broadcasted_iota(jnp.int32, (hd * dv, hd * dk), 0)
    scol = lax.broadcasted_iota(jnp.int32, (hd * dv, hd * dk), 1)
    same_head = (srow // dv) == (scol // dk)
    wlr, blr, ng = wlr_ref[...], blr_ref[...], ng_ref[...]

    cs = range(nc)
    cr = lambda c: slice(c * CHUNK, (c + 1) * CHUNK)
    g = [_log_sigmoid(_mm(lr_ref[cr(c), :], wlr) + blr) * (1.0 / GLA_GATE_NORM) for c in cs]
    bc = [_select_rows(ltri, g[c]) for c in cs]
    bl = [bc[c][CHUNK - 1:CHUNK, :] for c in cs]
    q_e = [q_ref[cr(c), :].astype(F32) * (dk ** -0.5) * jnp.exp(bc[c]) for c in cs]
    k_e = [k_ref[cr(c), :].astype(F32) * jnp.exp(-bc[c]) for c in cs]
    k_d = [k_ref[cr(c), :].astype(F32) * jnp.exp(bl[c] - bc[c]) for c in cs]
    kv = [jnp.where(same_head, _mm_tn(v_ref[cr(c), :], k_d[c]), 0.0) for c in cs]
    sts = []
    st = st_ref[...]
    for c in cs:
        sts.append(st)
        st = jnp.exp(bl[c]) * st + kv[c]
    st_ref[...] = st
    o_inter = [_mm_nt(q_e[c], sts[c]) for c in cs]
    q_heads = [jnp.concatenate([jnp.where((lane_k // dk) == h, q_e[c], 0.0) for h in range(hd)], axis=0)
               for c in cs]
    sc_all = [_mm_nt(q_heads[c], k_e[c]) for c in cs]
    sc = [[jnp.where(incl, sc_all[c][h * CHUNK:(h + 1) * CHUNK], 0.0) for h in range(hd)] for c in cs]
    for c in cs:
        for h in range(hd):
            cols = slice(h * dv, (h + 1) * dv)
            oh = _mm(sc[c][h], v_ref[cr(c), cols]) + o_inter[c][:, cols]
            ms = jnp.mean(oh * oh, axis=-1, keepdims=True)
            o_ref[cr(c), cols] = (oh * lax.rsqrt(ms + EPS) * ng * _silu(og_ref[cr(c), cols].astype(F32))).astype(o_ref.dtype)


def gla_mix(zm, wlr_pad, blr, ng, bsz, seq, tc, nsub):
    t = zm.shape[0]
    blk_rows = tc * nsub
    per_b = seq // blk_rows
    hd, dk, dv = GLA_HEADS, GLA_DK, GLA_DV

    def zspec(width, off):
        blk = off // width
        return pl.BlockSpec((blk_rows, width), lambda b, i: (b * per_b + i, blk))

    def full(a):
        return pl.BlockSpec(a.shape, lambda b, i: (0,) * a.ndim)

    return pl.pallas_call(
        functools.partial(_gla_kernel, nc=tc // CHUNK, nsub=nsub),
        grid=(bsz, per_b),
        in_specs=[zspec(hd * dk, ZM_GQ), zspec(hd * dk, ZM_GK), zspec(hd * dv, ZM_GV),
                  zspec(hd * dv, ZM_GOG), zspec(LANE, ZM_GLR), full(wlr_pad), full(blr), full(ng)],
        out_specs=pl.BlockSpec((blk_rows, hd * dv), lambda b, i: (b * per_b + i, 0)),
        out_shape=jax.ShapeDtypeStruct((t, hd * dv), F32),
        scratch_shapes=[pltpu.VMEM((hd * dv, hd * dk), F32)],
        compiler_params=_params("parallel", "arbitrary"),
        name="gla_mix",
    )(zm, zm, zm, zm, zm, wlr_pad, blr, ng)


GDN_PAD = 8


def _gdn_kernel(qkv_ref, og_ref, ba_ref, cw_ref, shift_ref, alog_ref, dtb_ref, ng_ref, o_ref,
                head_ref, s_ref, *, nc, nsub):
    @pl.when(pl.program_id(1) == 0)
    def _():
        s_ref[...] = jnp.zeros_like(s_ref)
        head_ref[0:GDN_PAD, :] = jnp.zeros((GDN_PAD, head_ref.shape[1]), F32)

    tc = nc * CHUNK

    def tile(s, carry):
        rows = pl.ds(pl.multiple_of(s * tc, tc), tc)
        _gdn_tile(qkv_ref.at[rows], og_ref.at[rows], ba_ref.at[rows], cw_ref, shift_ref, alog_ref, dtb_ref,
                  ng_ref, o_ref.at[rows], head_ref, s_ref, nc=nc)
        return carry

    lax.fori_loop(0, nsub, tile, 0)


def _gdn_tile(qkv_ref, og_ref, ba_ref, cw_ref, shift_ref, alog_ref, dtb_ref, ng_ref, o_ref,
              head_ref, s_ref, *, nc):
    hd, dk, dv, kc = GDN_HEADS, GDN_DK, GDN_DV, GDN_CONV
    tc = nc * CHUNK
    pad = GDN_PAD

    head_ref[pad:2 * pad, :] = qkv_ref[0:2 * pad, :].astype(F32)[0:pad]
    lagged = jnp.dot(shift_ref[...], qkv_ref[...], preferred_element_type=F32)

    ri = lax.broadcasted_iota(jnp.int32, (CHUNK, CHUNK), 0)
    ci = lax.broadcasted_iota(jnp.int32, (CHUNK, CHUNK), 1)
    incl = ri >= ci
    strict = ri > ci
    ltri = incl.astype(BF16)
    eye = (ri == ci).astype(F32)
    cw = [cw_ref[i] for i in range(kc)]
    ng = ng_ref[...]

    ba = ba_ref[...].astype(F32)
    beta_all = _sigmoid(ba)
    g_all = -jnp.exp(alog_ref[...]) * _softplus(ba + dtb_ref[...])

    cr = lambda c: slice(c * CHUNK, (c + 1) * CHUNK)
    gam_c = [_select_rows(ltri, g_all[cr(c), :]) for c in range(nc)]
    gam_tc = [g.T for g in gam_c]
    groups = [list(range(g0, min(g0 + GDN_GROUP_CHUNKS, nc))) for g0 in range(0, nc, GDN_GROUP_CHUNKS)]

    def conv_piece(chunks, col):
        lanes = slice(col * dk, (col + 1) * dk)
        nr = len(chunks) * CHUNK
        assert chunks[0] == 0 and nr == tc
        blocks = lambda a: a.reshape(nr // 8, 8, dk)
        acc = cw[kc - 1][:, lanes] * blocks(qkv_ref[:, lanes].astype(F32))
        head = cw[kc - 1][:, lanes] * head_ref[pad:2 * pad, lanes]
        for i in range(kc - 1):
            lag = kc - 1 - i
            acc = acc + cw[i][:, lanes] * blocks(lagged[(lag - 1) * tc:lag * tc, lanes])
            head = head + cw[i][:, lanes] * head_ref[pad - lag:2 * pad - lag, lanes]
        acc = jnp.concatenate([head, acc.reshape(nr, dk)[pad:]], axis=0)
        y = _silu(acc)
        if col < 2 * hd:
            y = y * lax.rsqrt(jnp.sum(y * y, axis=-1, keepdims=True) + EPS)
        if col < hd:
            y = y * (dk ** -0.5)
        return [y[i * CHUNK:(i + 1) * CHUNK] for i in range(len(chunks))]

    def chain_stages(chunks, cols, res):
        pairs = [(ci, c, h) for ci, c in enumerate(chunks) for h in range(hd)]
        n = len(pairs)
        q = [cols[h][ci] for ci, c, h in pairs]
        k = [cols[hd + h][ci] for ci, c, h in pairs]
        v = [cols[2 * hd + h][ci] for ci, c, h in pairs]
        beta = [beta_all[cr(c), h:h + 1] for ci, c, h in pairs]
        gam = [gam_c[c][:, hd + h:hd + h + 1] for ci, c, h in pairs]
        gam_row = [gam_tc[c][hd + h:hd + h + 1, :] for ci, c, h in pairs]
        dmask = [jnp.where(incl, jnp.exp(gam[i] - gam_row[i]), 0.0) for i in range(n)]
        k_beta = [k[i] * beta[i] for i in range(n)]
        egam = [jnp.exp(gam[i]) for i in range(n)]
        kk = [_mm_nt(jnp.concatenate([k_beta[i], q[i]], axis=0), k[i]) for i in range(n)]
        low = [jnp.where(strict, kk[i][:CHUNK] * dmask[i], 0.0) for i in range(n)]
        res["attn"] = [kk[i][CHUNK:] * dmask[i] for i in range(n)]
        rhs = [jnp.concatenate([v[i] * beta[i], k_beta[i] * egam[i]], axis=1).astype(BF16) for i in range(n)]
        yield
        lowb = [low[i].astype(BF16) for i in range(n)]
        pw = [_mm(lowb[i], lowb[i]).astype(BF16) for i in range(n)]
        tinv = [eye - low[i] for i in range(n)]
        for s in range(5):
            yield
            if s < 4:
                stk = [_mm(jnp.concatenate([tinv[i].astype(BF16), pw[i]], axis=0), pw[i]) for i in range(n)]
                tinv = [tinv[i] + stk[i][:CHUNK] for i in range(n)]
                pw = [stk[i][CHUNK:].astype(BF16) for i in range(n)]
            else:
                tinv = [tinv[i] + _mm(tinv[i], pw[i]) for i in range(n)]
        yield
        sol = [_mm(tinv[i], rhs[i]).astype(BF16) for i in range(n)]
        q_dec = [q[i] * egam[i] for i in range(n)]
        gam_last = [gam[i][CHUNK - 1:CHUNK, :] for i in range(n)]
        k_dec = [k[i] * jnp.exp(gam_last[i] - gam[i]) for i in range(n)]
        yield
        ks = [_mm_tn(k_dec[i], sol[i]) for i in range(n)]
        aw = [_mm(res["attn"][i], sol[i]) for i in range(n)]
        res["lhs"] = [jnp.concatenate([ks[i][:, dv:], q_dec[i] - aw[i][:, dv:]], axis=0).astype(BF16)
                      for i in range(n)]
        res["s_add"] = [ks[i][:, :dv] for i in range(n)]
        res["o_add"] = [aw[i][:, :dv] for i in range(n)]
        res["decay"] = [jnp.exp(gam_last[i]) for i in range(n)]

    ncols = 3 * hd
    cols = [conv_piece(groups[0], col) for col in range(ncols)]
    st = [s_ref[h] for h in range(hd)]
    for gi, chunks in enumerate(groups):
        res = {}
        nxt = groups[gi + 1] if gi + 1 < len(groups) else None
        nxt_cols = []
        for _ in chain_stages(chunks, cols, res):
            if nxt is not None and len(nxt_cols) < ncols:
                nxt_cols.append(conv_piece(nxt, len(nxt_cols)))
        while nxt is not None and len(nxt_cols) < ncols:
            nxt_cols.append(conv_piece(nxt, len(nxt_cols)))
        cols = nxt_cols
        lhs, s_add, o_add, decay = res["lhs"], res["s_add"], res["o_add"], res["decay"]
        for ci, c in enumerate(chunks):
            ids = [ci * hd + h for h in range(hd)]
            prod = [_mm(lhs[i], st[h]) for h, i in enumerate(ids)]
            o = [prod[h][dk:] + o_add[i] for h, i in enumerate(ids)]
            st = [decay[i] * st[h] - prod[h][:dk] + s_add[i] for h, i in enumerate(ids)]
            for h in range(hd):
                ms = jnp.mean(o[h] * o[h], axis=-1, keepdims=True)
                cl = slice(h * dv, (h + 1) * dv)
                o_ref[cr(c), cl] = (o[h] * lax.rsqrt(ms + EPS) * ng * _silu(og_ref[cr(c), cl].astype(F32))).astype(o_ref.dtype)
    for h in range(hd):
        s_ref[h] = st[h]
    head_ref[0:pad, :] = qkv_ref[tc - 2 * pad:tc, :].astype(F32)[pad:]


def gdn_mix(zm, conv_w_pad, shift_m, alog_row, dtb_row, ng, bsz, seq, tc, nsub):
    t = zm.shape[0]
    blk_rows = tc * nsub
    per_b = seq // blk_rows
    hd, dk, dv = GDN_HEADS, GDN_DK, GDN_DV
    qkv_w = 3 * hd * dk

    def zspec(width, off):
        blk = off // width
        return pl.BlockSpec((blk_rows, width), lambda b, i: (b * per_b + i, blk))

    def full(a):
        return pl.BlockSpec(a.shape, lambda b, i: (0,) * a.ndim)

    return pl.pallas_call(
        functools.partial(_gdn_kernel, nc=tc // CHUNK, nsub=nsub),
        grid=(bsz, per_b),
        in_specs=[zspec(qkv_w, ZM_DQKV), zspec(hd * dv, ZM_DOG), zspec(LANE, ZM_DBA),
                  full(conv_w_pad), full(shift_m), full(alog_row), full(dtb_row), full(ng)],
        out_specs=pl.BlockSpec((blk_rows, hd * dv), lambda b, i: (b * per_b + i, 0)),
        out_shape=jax.ShapeDtypeStruct((t, hd * dv), F32),
        scratch_shapes=[pltpu.VMEM((2 * GDN_PAD, qkv_w), F32), pltpu.VMEM((hd, dk, dv), F32)],
        compiler_params=_params("parallel", "arbitrary"),
        name="gdn_mix",
    )(zm, zm, zm, conv_w_pad, shift_m, alog_row, dtb_row, ng)


def _merge_kernel(h3_ref, ygla_ref, ygdn_ref, yst_ref, wglu_ref, wzg_ref, wbg_ref, wbs_ref, wbd_ref, o_ref,
                  s5t_ref, *, jt, bw, tn):
    n_rows = jt * LANE
    d = h3_ref.shape[-1]
    h = h3_ref[...].reshape(n_rows, d).astype(BF16)
    a_gla = ygla_ref[...].reshape(n_rows, bw).astype(BF16)
    a_gdn = ygdn_ref[...].reshape(n_rows, bw).astype(BF16)
    for jj in range(jt):
        ys = _gelu_tanh(yst_ref[jj].astype(F32))
        glu = _mm_tn(ys, wglu_ref[...])
        s5t_ref[jj * LANE:(jj + 1) * LANE, :] = glu[:, :bw] * _sigmoid(glu[:, bw:])
    a_s5 = pltpu.einshape("(jr)d->rjd", s5t_ref[...], j=jt).reshape(n_rows, bw).astype(BF16)

    def branch(b, cols, a, wb_ref):
        gate = _sigmoid(jnp.dot(h, wzg_ref[:, b * d + cols.start:b * d + cols.stop], preferred_element_type=F32))
        return gate * jnp.dot(a, wb_ref[:, cols], preferred_element_type=F32)

    for c0 in range(0, d, tn):
        cols = slice(c0, c0 + tn)
        m = branch(0, cols, a_gla, wbg_ref) + branch(1, cols, a_s5, wbs_ref) + branch(2, cols, a_gdn, wbd_ref)
        o_ref[:, :, cols] = m.reshape(LANE, jt, tn)


def merge_gate(h3, ygla3, ygdn3, yst, wglu, wzg, wbg, wbs, wbd, layer, jt, tn):
    r, _, d = h3.shape
    bw = ygla3.shape[2]

    def view(width):
        return pl.BlockSpec((LANE, jt, width), lambda i, j: (i, j, 0))

    def layer_w(a):
        return pl.BlockSpec((None,) + a.shape[1:], lambda i, j: (layer, 0, 0))

    return pl.pallas_call(
        functools.partial(_merge_kernel, jt=jt, bw=bw, tn=tn),
        grid=(r // LANE, CHUNK // jt),
        in_specs=[view(d), view(bw), view(bw),
                  pl.BlockSpec((jt, yst.shape[1], LANE), lambda i, j: (j, 0, i)),
                  layer_w(wglu), layer_w(wzg), layer_w(wbg), layer_w(wbs), layer_w(wbd)],
        out_specs=view(d),
        out_shape=jax.ShapeDtypeStruct((r, CHUNK, d), F32),
        scratch_shapes=[pltpu.VMEM((jt * LANE, bw), F32)],
        compiler_params=_params("parallel", "parallel"),
        name="merge_gate",
    )(h3, ygla3, ygdn3, yst, wglu, wzg, wbg, wbs, wbd)


def _ffn_kernel(x_ref, m_ref, wout_ref, gt1_ref, g_ref, sc_ref, sh_ref, gt2_ref, win_ref, wo_ref,
                fg_ref, o_ref, *, final_norm, tf):
    dff = wo_ref.shape[0]
    x1 = x_ref[...] + gt1_ref[...] * _mm(m_ref[...], wout_ref[...])
    h = _norm_mod(x1, g_ref[...], sc_ref[...], sh_ref[...]).astype(BF16)
    acc = None
    for f0 in range(0, dff, tf):
        a = jnp.dot(h, win_ref[:, f0:f0 + tf], preferred_element_type=F32)
        b = jnp.dot(h, win_ref[:, dff + f0:dff + f0 + tf], preferred_element_type=F32)
        part = _mm(_silu(a) * b, wo_ref[f0:f0 + tf, :])
        acc = part if acc is None else acc + part
    y = x1 + gt2_ref[...] * acc
    if final_norm:
        ms = jnp.mean(y * y, axis=-1, keepdims=True)
        y = y * lax.rsqrt(ms + EPS) * fg_ref[...]
    o_ref[...] = y


def ffn_block(x2, merged, w_o, gt1, g, sc, sh, gt2, w_in, w_out, final_g, layer, seq, tm, tf, final_norm):
    t, d = x2.shape
    dff = w_out.shape[1]
    per_b = seq // tm

    def bspec():
        return pl.BlockSpec((None, 1, d), lambda i: (i // per_b, 0, 0))

    def const(a):
        return pl.BlockSpec(a.shape, lambda i: (0,) * a.ndim)

    def layer_w(a):
        return pl.BlockSpec((None,) + a.shape[1:], lambda i: (layer, 0, 0))

    row = pl.BlockSpec((tm, d), lambda i: (i, 0))
    return pl.pallas_call(
        functools.partial(_ffn_kernel, final_norm=final_norm, tf=tf),
        grid=(t // tm,),
        in_specs=[row, row, layer_w(w_o), bspec(), const(g), bspec(), bspec(), bspec(),
                  layer_w(w_in), layer_w(w_out), const(final_g)],
        out_specs=row,
        out_shape=jax.ShapeDtypeStruct((t, d), F32),
        compiler_params=_params("parallel"),
        name="ffn_block",
    )(x2, merged, w_o, gt1, g, sc, sh, gt2, w_in, w_out, final_g)


def _wprep_kernel(w_ref, zm_ref, zg_ref, s5t_ref, *, segs, pads, zg_off, s5_off):
    for dst, width in pads:
        zm_ref[0, :, dst:dst + width] = jnp.zeros((zm_ref.shape[1], width), BF16)
    for src, width, dst in segs:
        zm_ref[0, :, dst:dst + width] = w_ref[0, :, src:src + width].astype(BF16)
    zg_ref[0] = w_ref[0, :, zg_off:zg_off + zg_ref.shape[2]].astype(BF16)
    s5t_ref[0] = w_ref[0, :, s5_off:s5_off + s5t_ref.shape[1]].astype(F32).T.astype(BF16)


def prep_in_weights(w_in, segs, pads, zg_off, zg_w, s5_off, s5_w, tr):
    depth, d, d_in = w_in.shape
    return pl.pallas_call(
        functools.partial(_wprep_kernel, segs=segs, pads=pads, zg_off=zg_off, s5_off=s5_off),
        grid=(depth, d // tr),
        in_specs=[pl.BlockSpec((1, tr, d_in), lambda i, j: (i, j, 0))],
        out_specs=[pl.BlockSpec((1, tr, ZM_WIDTH), lambda i, j: (i, j, 0)),
                   pl.BlockSpec((1, tr, zg_w), lambda i, j: (i, j, 0)),
                   pl.BlockSpec((1, s5_w, tr), lambda i, j: (i, 0, j))],
        out_shape=[jax.ShapeDtypeStruct((depth, d, ZM_WIDTH), BF16),
                   jax.ShapeDtypeStruct((depth, d, zg_w), BF16),
                   jax.ShapeDtypeStruct((depth, s5_w, d), BF16)],
        compiler_params=_params("parallel", "parallel"),
        name="prep_in_weights",
    )(w_in)


def _forward(x, c, w_ada, b_ada, norm1_g, w_in, gla_w_lr, gla_b_lr, gla_norm_g,
             s5_lambda_re, s5_lambda_im, s5_log_dt, s5_b_re, s5_b_im, s5_c_re, s5_c_im,
             s5_d, s5_w_glu, gdn_conv_w, gdn_a_log, gdn_dt_bias, gdn_norm_g,
             w_branch_gla, w_branch_s5, w_branch_gdn, w_out, norm2_g, w_ffn_in, w_ffn_out,
             final_g):
    bsz, seq, d = x.shape
    depth = w_ada.shape[0]
    t = bsz * seq
    nchunk = seq // CHUNK
    r = bsz * nchunk
    gqk, gw = GLA_HEADS * GLA_DK, GLA_HEADS * GLA_DV
    s5w = S5_GROUPS * S5_GROUP_CH
    dqkv, dw = 3 * GDN_HEADS * GDN_DK, GDN_HEADS * GDN_DV

    mod = ada_modulation(c, w_ada, b_ada).reshape(depth, bsz, 6, 1, d)
    s5m, s5et, s5ft, s5ac = s5_params(s5_lambda_re, s5_lambda_im, s5_log_dt, s5_b_re, s5_b_im,
                                      s5_c_re, s5_c_im, s5_d)

    o_gq, o_gk, o_gv = 0, gqk, 2 * gqk
    o_glr = o_gv + gw
    o_gog = o_glr + GLA_LOWRANK
    o_s5 = o_gog + gw
    o_dqkv = o_s5 + s5w
    o_dbeta = o_dqkv + dqkv
    o_da = o_dbeta + GDN_HEADS
    o_dog = o_da + GDN_HEADS
    o_zg = o_dog + dw

    segs = ((o_gq, gqk, ZM_GQ), (o_gk, gqk, ZM_GK), (o_gv, gw, ZM_GV), (o_gog, gw, ZM_GOG),
            (o_dqkv, dqkv, ZM_DQKV), (o_dog, dw, ZM_DOG), (o_glr, GLA_LOWRANK, ZM_GLR),
            (o_dbeta, 2 * GDN_HEADS, ZM_DBA))
    pads = ((ZM_GLR, LANE), (ZM_DBA, LANE))
    w_in_b = jnp.pad(w_in.astype(BF16), ((0, 0), (0, 0), (0, -w_in.shape[2] % LANE)))
    w_zm_all, w_zg_all, w_s5t_all = prep_in_weights(w_in_b, segs, pads, o_zg, N_BRANCH * d, o_s5, s5w, tr=256)
    s5_w_glu, w_branch_gla, w_branch_s5, w_branch_gdn, w_out, w_ffn_in, w_ffn_out = [
        a.astype(BF16) for a in (s5_w_glu, w_branch_gla, w_branch_s5, w_branch_gdn, w_out, w_ffn_in, w_ffn_out)]

    gdn_tc = 256
    lag_row = jnp.arange((GDN_CONV - 1) * gdn_tc)
    gdn_shift = (jnp.arange(gdn_tc)[None, :]
                 == (lag_row % gdn_tc - lag_row // gdn_tc - 1)[:, None]).astype(BF16)

    x2 = x.reshape(t, d)
    for i in range(depth):

        sh1, sc1, gt1, sh2, sc2, gt2 = [mod[i, :, k] for k in range(6)]
        g1 = norm1_g[i].reshape(1, d)

        zm, hf = norm_mod_matmul(x2, g1, sc1, sh1, w_zm_all, i, seq, tm=512, tn=3840)
        h3 = hf.reshape(r, CHUNK, d)
        ut = s5_inproj(h3, w_s5t_all, i, jt=8, rt=r)

        wlr_pad = jnp.pad(gla_w_lr[i], ((0, LANE - GLA_LOWRANK), (0, 0)))
        y_gla = gla_mix(zm, wlr_pad, gla_b_lr[i].reshape(1, gqk), gla_norm_g[i].reshape(1, GLA_DV),
                        bsz, seq, tc=256, nsub=4)
        yst = s5_mix(ut, s5m, s5et, s5ft, s5ac, i, nchunk, gsub=4)
        conv_pad = jnp.broadcast_to(gdn_conv_w[i][:, None, :], (GDN_CONV, 8, dqkv))
        alog_row = jnp.pad(gdn_a_log[i], (GDN_HEADS, LANE - 2 * GDN_HEADS)).reshape(1, LANE)
        dtb_row = jnp.pad(gdn_dt_bias[i], (GDN_HEADS, LANE - 2 * GDN_HEADS)).reshape(1, LANE)
        y_gdn = gdn_mix(zm, conv_pad, gdn_shift, alog_row, dtb_row, gdn_norm_g[i].reshape(1, GDN_DV),
                        bsz, seq, tc=gdn_tc, nsub=4)

        merged = merge_gate(h3, y_gla.reshape(r, CHUNK, gw), y_gdn.reshape(r, CHUNK, dw),
                            yst, s5_w_glu, w_zg_all, w_branch_gla, w_branch_s5, w_branch_gdn, i,
                            jt=8, tn=256)
        x2 = ffn_block(x2, merged.reshape(t, d), w_out, gt1,
                       norm2_g[i].reshape(1, d), sc2, sh2, gt2,
                       w_ffn_in, w_ffn_out,
                       final_g.reshape(1, d), i, seq, tm=512, tf=256, final_norm=(i == depth - 1))
    return x2.reshape(bsz, seq, d)


def kernel(x, c, w_ada, b_ada, norm1_g, w_in, gla_w_lr, gla_b_lr, gla_norm_g, s5_lambda_re, s5_lambda_im, s5_log_dt, s5_b_re, s5_b_im, s5_c_re, s5_c_im, s5_d, s5_w_glu, gdn_conv_w, gdn_a_log, gdn_dt_bias, gdn_norm_g, w_branch_gla, w_branch_s5, w_branch_gdn, w_out, norm2_g, w_ffn_in, w_ffn_out, final_g):
    return _forward(x, c, w_ada, b_ada, norm1_g, w_in, gla_w_lr, gla_b_lr, gla_norm_g,
                    s5_lambda_re, s5_lambda_im, s5_log_dt, s5_b_re, s5_b_im, s5_c_re, s5_c_im,
                    s5_d, s5_w_glu, gdn_conv_w, gdn_a_log, gdn_dt_bias, gdn_norm_g,
                    w_branch_gla, w_branch_s5, w_branch_gdn, w_out, norm2_g, w_ffn_in, w_ffn_out,
                    final_g)
```

```python
import functools
import math

import jax
import jax.numpy as jnp
from jax import lax
from jax.experimental import pallas as pl
from jax.experimental.pallas import tpu as pltpu

F32 = jnp.float32
BF16 = jnp.bfloat16
HI = lax.Precision.HIGHEST

EPS = 1e-6
CHUNK = 64
LANE = 128
VMEM_LIMIT = 56 * 1024 * 1024

GLA_HEADS, GLA_DK, GLA_DV, GLA_LOWRANK = 4, 64, 128, 16
GLA_GATE_NORM = 16.0
S5_GROUPS, S5_GROUP_CH, S5_STATE = 32, 16, 64
GDN_HEADS, GDN_DK, GDN_DV, GDN_CONV = 4, 128, 128, 4
N_BRANCH = 3
GDN_GROUP_CHUNKS = 4

ZM_GQ, ZM_GK, ZM_GV, ZM_GOG = 0, 256, 512, 1024
ZM_DQKV, ZM_DOG, ZM_GLR, ZM_DBA = 1536, 3072, 3584, 3712
ZM_WIDTH = 3840


def _mm(a, b):
    return jnp.dot(a.astype(BF16), b.astype(BF16), preferred_element_type=F32)


def _mm_nt(a, b):
    return lax.dot_general(a.astype(BF16), b.astype(BF16), (((1,), (1,)), ((), ())),
                           preferred_element_type=F32)


def _mm_tn(a, b):
    return lax.dot_general(a.astype(BF16), b.astype(BF16), (((0,), (0,)), ((), ())),
                           preferred_element_type=F32)


def _mm_hi(a, b):
    return jnp.dot(a, b, precision=HI, preferred_element_type=F32)


def _split3(x):
    hi = x.astype(BF16)
    r1 = x - hi.astype(F32)
    mid = r1.astype(BF16)
    return hi, mid, (r1 - mid.astype(F32)).astype(BF16)


def _select_rows(m01, x):
    return sum(jnp.dot(m01, t, preferred_element_type=F32) for t in _split3(x))


def _select_cols(x, m01):
    return sum(jnp.dot(t, m01, preferred_element_type=F32) for t in _split3(x))


def _sigmoid(x):
    return 0.5 + 0.5 * jnp.tanh(0.5 * x)


def _silu(x):
    h = 0.5 * x
    return h + h * jnp.tanh(h)


def _softplus(x):
    return jnp.maximum(x, 0.0) + jnp.log(1.0 + jnp.exp(-jnp.abs(x)))


def _log_sigmoid(x):
    return -_softplus(-x)


def _gelu_tanh(x):
    c = math.sqrt(2.0 / math.pi)
    return 0.5 * x * (1.0 + jnp.tanh(c * (x + 0.044715 * (x * x * x))))


def _norm_mod(x, g, sc, sh):
    ms = jnp.mean(x * x, axis=-1, keepdims=True)
    return (x * lax.rsqrt(ms + EPS) * g) * (1.0 + sc) + sh


def _tok_major(ref):
    return pltpu.einshape("rjd->(jr)d", ref[...])


def _params(*sem):
    return pltpu.CompilerParams(dimension_semantics=sem, vmem_limit_bytes=VMEM_LIMIT)


def _ada_kernel(c_ref, w_ref, b_ref, o_ref):
    c = c_ref[...]
    o_ref[0] = jnp.dot(_silu(c), w_ref[0], preferred_element_type=F32) + b_ref[0]


def ada_modulation(c, w_ada, b_ada):
    depth, d, d6 = w_ada.shape
    bsz = c.shape[0]
    rows = -(-bsz // 8) * 8
    c_pad = jnp.pad(c, ((0, rows - bsz), (0, 0)))
    out = pl.pallas_call(
        _ada_kernel,
        grid=(depth, d6 // d),
        in_specs=[pl.BlockSpec((rows, d), lambda i, j: (0, 0)),
                  pl.BlockSpec((1, d, d), lambda i, j: (i, 0, j)),
                  pl.BlockSpec((1, 1, d), lambda i, j: (i, 0, j))],
        out_specs=pl.BlockSpec((1, rows, d), lambda i, j: (i, 0, j)),
        out_shape=jax.ShapeDtypeStruct((depth, rows, d6), F32),
        compiler_params=_params("parallel", "parallel"),
        name="ada_modulation",
    )(c_pad, w_ada, b_ada.reshape(depth, 1, d6))
    return out[:, :bsz]


def _inproj_kernel(x_ref, g_ref, sc_ref, sh_ref, w_ref, o_ref, hf_ref, h_ref):
    @pl.when(pl.program_id(1) == 0)
    def _():
        h = _norm_mod(x_ref[...], g_ref[...], sc_ref[...], sh_ref[...])
        hf_ref[...] = h
        h_ref[...] = h.astype(BF16)

    o_ref[...] = jnp.dot(h_ref[...], w_ref[...], preferred_element_type=F32).astype(o_ref.dtype)


def norm_mod_matmul(x2, g, sc, sh, w, layer, seq, tm, tn):
    t, d = x2.shape
    c = w.shape[2]
    per_b = seq // tm
    return pl.pallas_call(
        _inproj_kernel,
        grid=(t // tm, c // tn),
        in_specs=[pl.BlockSpec((tm, d), lambda i, j: (i, 0)),
                  pl.BlockSpec((1, d), lambda i, j: (0, 0)),
                  pl.BlockSpec((None, 1, d), lambda i, j: (i // per_b, 0, 0)),
                  pl.BlockSpec((None, 1, d), lambda i, j: (i // per_b, 0, 0)),
                  pl.BlockSpec((None, d, tn), lambda i, j: (layer, 0, j))],
        out_specs=[pl.BlockSpec((tm, tn), lambda i, j: (i, j)), pl.BlockSpec((tm, d), lambda i, j: (i, 0))],
        out_shape=[jax.ShapeDtypeStruct((t, c), BF16), jax.ShapeDtypeStruct((t, d), F32)],
        scratch_shapes=[pltpu.VMEM((tm, d), BF16)],
        compiler_params=_params("parallel", "arbitrary"),
        name="norm_mod_matmul",
    )(x2, g, sc, sh, w)


def _s5_inproj_kernel(h_ref, wt_ref, o_ref, *, jt):
    rt = h_ref.shape[0]
    h = _tok_major(h_ref).astype(BF16)
    ut = lax.dot_general(wt_ref[...], h, (((1,), (1,)), ((), ())), preferred_element_type=F32)
    for jj in range(jt):
        o_ref[jj] = ut[:, jj * rt:(jj + 1) * rt].astype(BF16)


def s5_inproj(h3, wt, layer, jt, rt):
    r, _, d = h3.shape
    s5w = wt.shape[1]
    return pl.pallas_call(
        functools.partial(_s5_inproj_kernel, jt=jt),
        grid=(r // rt, CHUNK // jt),
        in_specs=[pl.BlockSpec((rt, jt, d), lambda i, j: (i, j, 0)),
                  pl.BlockSpec((None, s5w, d), lambda i, j: (layer, 0, 0))],
        out_specs=pl.BlockSpec((jt, s5w, rt), lambda i, j: (j, 0, i)),
        out_shape=jax.ShapeDtypeStruct((CHUNK, s5w, r), BF16),
        compiler_params=_params("parallel", "parallel"),
        name="s5_inproj",
    )(h3, wt)


S5_CW = CHUNK * S5_GROUP_CH


def _cpow(lr_dt, li_dt, e):
    mag = jnp.exp(lr_dt * e)
    ang = li_dt * e
    return mag * jnp.cos(ang), mag * jnp.sin(ang)


def _s5_param_kernel(ldt_ref, lrc_ref, lic_ref, lrr_ref, lir_ref, bre_ref, bim_ref,
                     c1_ref, c2_ref, dcol_ref, m_ref, et_ref, ft_ref, ac_ref):
    p, cw, h = S5_STATE, S5_CW, S5_GROUP_CH
    dt = jnp.exp(ldt_ref[0])
    lrc, lic = lrc_ref[0], lic_ref[0]
    ab_re, ab_im = _cpow(lrc * dt, lic * dt, 1.0)
    den = lrc * lrc + lic * lic
    nr, ni = ab_re - 1.0, ab_im
    w_re = (nr * lrc + ni * lic) / den
    w_im = (ni * lrc - nr * lic) / den
    trow = lax.broadcasted_iota(jnp.int32, (h, cw), 0)
    tcol = lax.broadcasted_iota(jnp.int32, (h, cw), 1)
    tile_l = (trow == tcol % h).astype(BF16)
    bre, bim = _select_cols(bre_ref[0], tile_l), _select_cols(bim_ref[0], tile_l)
    bb_re = w_re * bre - w_im * bim
    bb_im = w_re * bim + w_im * bre
    lane = lax.broadcasted_iota(jnp.int32, (1, LANE), 1)
    e_m = jnp.where(lane < CHUNK, CHUNK - 1 - lane, 0).astype(F32)
    pd_re, pd_im = _cpow(lrc * dt, lic * dt, e_m)
    xrow = lax.broadcasted_iota(jnp.int32, (LANE, cw), 0)
    xcol = lax.broadcasted_iota(jnp.int32, (LANE, cw), 1)
    expand_l = (xrow == xcol // h).astype(BF16)
    p_re, p_im = _select_cols(pd_re, expand_l), _select_cols(pd_im, expand_l)
    e_re = p_re * bb_re - p_im * bb_im
    e_im = p_re * bb_im + p_im * bb_re
    et = jnp.concatenate([e_re, e_im], axis=0)
    et_ref[0] = et.astype(BF16)
    a_re, a_im = _cpow(lrc * dt, lic * dt, float(CHUNK))
    ac_ref[0] = jnp.concatenate([a_re, a_im], axis=0)
    sgn = jnp.where(lax.broadcasted_iota(jnp.int32, (1, 2 * p), 1) < p, 1.0, -1.0)
    krev = _mm_hi(c1_ref[0] * sgn, et)
    row = lax.broadcasted_iota(jnp.int32, (h, cw), 0)
    col = lax.broadcasted_iota(jnp.int32, (h, cw), 1)
    krev = krev + jnp.where(col == (cw - h) + row, dcol_ref[0], 0.0)
    rrev = jnp.concatenate([krev, jnp.zeros_like(krev)], axis=1)
    per_tile = LANE // h
    rolled = [rrev if r == 0 else pltpu.roll(rrev, 2 * cw - r * h, axis=1) for r in range(per_tile)]
    for i in range(CHUNK):
        s = (CHUNK - 1 - i) * h
        a, r = s // LANE, (s % LANE) // h
        m_ref[0, i * h:(i + 1) * h, :] = rolled[r][:, a * LANE:a * LANE + cw].astype(BF16)
    f_i = (lax.broadcasted_iota(jnp.int32, (CHUNK, 1), 0) + 1).astype(F32)
    qd_re, qd_im = _cpow(lrr_ref[0] * dt, lir_ref[0] * dt, f_i)
    ft = (c1_ref[0] * sgn)[None] * qd_re[:, None, :] - c2_ref[0][None] * qd_im[:, None, :]
    ft_ref[0] = ft.reshape(cw, 2 * p).astype(BF16)


def s5_params(lam_re, lam_im, log_dt, b_re, b_im, c_re, c_im, dpar):
    ng = lam_re.shape[0] * lam_re.shape[1]
    p, h, cw = S5_STATE, S5_GROUP_CH, S5_CW
    lam_re = lam_re.reshape(ng, p)
    lam_im = lam_im.reshape(ng, p)
    c_re = c_re.reshape(ng, h, p)
    c_im = c_im.reshape(ng, h, p)
    c1 = jnp.concatenate([c_re, c_im], axis=-1)
    c2 = jnp.concatenate([c_im, c_re], axis=-1)
    args = (log_dt.reshape(ng, 1, 1),
            lam_re.reshape(ng, p, 1), lam_im.reshape(ng, p, 1),
            jnp.tile(lam_re.reshape(ng, 1, p), (1, 1, 2)), jnp.tile(lam_im.reshape(ng, 1, p), (1, 1, 2)),
            b_re.reshape(ng, p, h), b_im.reshape(ng, p, h), c1, c2,
            dpar.reshape(ng, h, 1))

    def spec(a):
        return pl.BlockSpec((1,) + a.shape[1:], lambda i: (i, 0, 0))

    return pl.pallas_call(
        _s5_param_kernel,
        grid=(ng,),
        in_specs=[spec(a) for a in args],
        out_specs=[pl.BlockSpec((1, cw, cw), lambda i: (i, 0, 0)),
                   pl.BlockSpec((1, 2 * p, cw), lambda i: (i, 0, 0)),
                   pl.BlockSpec((1, cw, 2 * p), lambda i: (i, 0, 0)),
                   pl.BlockSpec((1, 2 * p, 1), lambda i: (i, 0, 0))],
        out_shape=[jax.ShapeDtypeStruct((ng, cw, cw), BF16),
                   jax.ShapeDtypeStruct((ng, 2 * p, cw), BF16),
                   jax.ShapeDtypeStruct((ng, cw, 2 * p), BF16),
                   jax.ShapeDtypeStruct((ng, 2 * p, 1), F32)],
        compiler_params=_params("parallel"),
        name="s5_params",
    )(*args)


S5_MBLOCKS = 4


def _s5_mix_kernel(u_ref, *refs, nchunk, gsub):
    m_refs, (et_ref, ft_ref, ac_ref, y_ref) = refs[:S5_MBLOCKS], refs[S5_MBLOCKS:]
    p, h = S5_STATE, S5_GROUP_CH
    r = u_ref.shape[-1]
    ks = range(gsub)
    u = [u_ref[:, k * h:(k + 1) * h, :].reshape(S5_CW, r) for k in ks]
    s = [jnp.dot(et_ref[k], u[k], preferred_element_type=F32) for k in ks]
    s_re, s_im = [s[k][:p] for k in ks], [s[k][p:] for k in ks]
    a_re, a_im = [ac_ref[k][:p] for k in ks], [ac_ref[k][p:] for k in ks]
    n_idx = lax.broadcasted_iota(jnp.int32, (1, r), 1) % nchunk
    shift = 1
    while shift < nchunk:
        keep = n_idx >= shift
        t_re = [jnp.where(keep, pltpu.roll(s_re[k], shift, axis=1), 0.0) for k in ks]
        t_im = [jnp.where(keep, pltpu.roll(s_im[k], shift, axis=1), 0.0) for k in ks]
        s_re, s_im = ([s_re[k] + a_re[k] * t_re[k] - a_im[k] * t_im[k] for k in ks],
                      [s_im[k] + a_re[k] * t_im[k] + a_im[k] * t_re[k] for k in ks])
        a_re, a_im = ([a_re[k] * a_re[k] - a_im[k] * a_im[k] for k in ks], [2.0 * a_re[k] * a_im[k] for k in ks])
        shift *= 2
    keep = n_idx >= 1
    for k in ks:
        h_prev = jnp.concatenate([jnp.where(keep, pltpu.roll(s_re[k], 1, axis=1), 0.0),
                                  jnp.where(keep, pltpu.roll(s_im[k], 1, axis=1), 0.0)], axis=0)
        rb_rows = S5_CW // S5_MBLOCKS
        y = jnp.concatenate([jnp.dot(m_refs[rb][k], u[k][:(rb + 1) * rb_rows], preferred_element_type=F32)
                             for rb in range(S5_MBLOCKS)], axis=0)
        y = y + jnp.dot(ft_ref[k], h_prev.astype(BF16), preferred_element_type=F32)
        y_ref[:, k * h:(k + 1) * h, :] = y.reshape(CHUNK, h, r).astype(y_ref.dtype)


def s5_mix(ut, m, et, ft, ac, layer, nchunk, gsub):
    _, s5w, r = ut.shape
    g, h, p, cw = S5_GROUPS, S5_GROUP_CH, S5_STATE, S5_CW
    base = layer * g // gsub
    return pl.pallas_call(
        functools.partial(_s5_mix_kernel, nchunk=nchunk, gsub=gsub),
        grid=(g // gsub,),
        in_specs=[pl.BlockSpec((CHUNK, gsub * h, r), lambda i: (0, i, 0)),
                  *[pl.BlockSpec((gsub, cw // S5_MBLOCKS, (rb + 1) * cw // S5_MBLOCKS),
                                 lambda i, rb=rb: (base + i, rb, 0)) for rb in range(S5_MBLOCKS)],
                  pl.BlockSpec((gsub, 2 * p, cw), lambda i: (base + i, 0, 0)),
                  pl.BlockSpec((gsub, cw, 2 * p), lambda i: (base + i, 0, 0)),
                  pl.BlockSpec((gsub, 2 * p, 1), lambda i: (base + i, 0, 0))],
        out_specs=pl.BlockSpec((CHUNK, gsub * h, r), lambda i: (0, i, 0)),
        out_shape=jax.ShapeDtypeStruct((CHUNK, s5w, r), BF16),
        compiler_params=_params("parallel"),
        name="s5_mix",
    )(ut, *([m] * S5_MBLOCKS), et, ft, ac)


def _gla_kernel(q_ref, k_ref, v_ref, og_ref, lr_ref, wlr_ref, blr_ref, ng_ref, o_ref, st_ref, *, nc, nsub):
    @pl.when(pl.program_id(1) == 0)
    def _():
        st_ref[...] = jnp.zeros_like(st_ref)

    tc = nc * CHUNK

    def tile(s, carry):
        rows = pl.ds(pl.multiple_of(s * tc, tc), tc)
        _gla_tile(q_ref.at[rows], k_ref.at[rows], v_ref.at[rows], og_ref.at[rows], lr_ref.at[rows],
                  wlr_ref, blr_ref, ng_ref, o_ref.at[rows], st_ref, nc=nc)
        return carry

    lax.fori_loop(0, nsub, tile, 0)


def _gla_tile(q_ref, k_ref, v_ref, og_ref, lr_ref, wlr_ref, blr_ref, ng_ref, o_ref, st_ref, *, nc):
    hd, dk, dv = GLA_HEADS, GLA_DK, GLA_DV
    ri = lax.broadcasted_iota(jnp.int32, (CHUNK, CHUNK), 0)
    ci = lax.broadcasted_iota(jnp.int32, (CHUNK, CHUNK), 1)
    incl = ri >= ci
    ltri = incl.astype(BF16)
    lane_k = lax.broadcasted_iota(jnp.int32, (1, hd * dk), 1)
    srow = lax.broadcasted_iota(jnp.int32, (hd * dv, hd * dk), 0)
    scol = lax.broadcasted_iota(jnp.int32, (hd * dv, hd * dk), 1)
    same_head = (srow // dv) == (scol // dk)
    wlr, blr, ng = wlr_ref[...], blr_ref[...], ng_ref[...]

    cs = range(nc)
    cr = lambda c: slice(c * CHUNK, (c + 1) * CHUNK)
    g = [_log_sigmoid(_mm(lr_ref[cr(c), :], wlr) + blr) * (1.0 / GLA_GATE_NORM) for c in cs]
    bc = [_select_rows(ltri, g[c]) for c in cs]
    bl = [bc[c][CHUNK - 1:CHUNK, :] for c in cs]
    q_e = [q_ref[cr(c), :].astype(F32) * (dk ** -0.5) * jnp.exp(bc[c]) for c in cs]
    k_e = [k_ref[cr(c), :].astype(F32) * jnp.exp(-bc[c]) for c in cs]
    k_d = [k_ref[cr(c), :].astype(F32) * jnp.exp(bl[c] - bc[c]) for c in cs]
    kv = [jnp.where(same_head, _mm_tn(v_ref[cr(c), :], k_d[c]), 0.0) for c in cs]
    sts = []
    st = st_ref[...]
    for c in cs:
        sts.append(st)
        st = jnp.exp(bl[c]) * st + kv[c]
    st_ref[...] = st
    o_inter = [_mm_nt(q_e[c], sts[c]) for c in cs]
    q_heads = [jnp.concatenate([jnp.where((lane_k // dk) == h, q_e[c], 0.0) for h in range(hd)], axis=0)
               for c in cs]
    sc_all = [_mm_nt(q_heads[c], k_e[c]) for c in cs]
    sc = [[jnp.where(incl, sc_all[c][h * CHUNK:(h + 1) * CHUNK], 0.0) for h in range(hd)] for c in cs]
    for c in cs:
        for h in range(hd):
            cols = slice(h * dv, (h + 1) * dv)
            oh = _mm(sc[c][h], v_ref[cr(c), cols]) + o_inter[c][:, cols]
            ms = jnp.mean(oh * oh, axis=-1, keepdims=True)
            o_ref[cr(c), cols] = (oh * lax.rsqrt(ms + EPS) * ng * _silu(og_ref[cr(c), cols].astype(F32))).astype(o_ref.dtype)


def gla_mix(zm, wlr_pad, blr, ng, bsz, seq, tc, nsub):
    t = zm.shape[0]
    blk_rows = tc * nsub
    per_b = seq // blk_rows
    hd, dk, dv = GLA_HEADS, GLA_DK, GLA_DV

    def zspec(width, off):
        blk = off // width
        return pl.BlockSpec((blk_rows, width), lambda b, i: (b * per_b + i, blk))

    def full(a):
        return pl.BlockSpec(a.shape, lambda b, i: (0,) * a.ndim)

    return pl.pallas_call(
        functools.partial(_gla_kernel, nc=tc // CHUNK, nsub=nsub),
        grid=(bsz, per_b),
        in_specs=[zspec(hd * dk, ZM_GQ), zspec(hd * dk, ZM_GK), zspec(hd * dv, ZM_GV),
                  zspec(hd * dv, ZM_GOG), zspec(LANE, ZM_GLR), full(wlr_pad), full(blr), full(ng)],
        out_specs=pl.BlockSpec((blk_rows, hd * dv), lambda b, i: (b * per_b + i, 0)),
        out_shape=jax.ShapeDtypeStruct((t, hd * dv), F32),
        scratch_shapes=[pltpu.VMEM((hd * dv, hd * dk), F32)],
        compiler_params=_params("parallel", "arbitrary"),
        name="gla_mix",
    )(zm, zm, zm, zm, zm, wlr_pad, blr, ng)


GDN_PAD = 8


def _gdn_kernel(qkv_ref, og_ref, ba_ref, cw_ref, shift_ref, alog_ref, dtb_ref, ng_ref, o_ref,
                head_ref, s_ref, *, nc, nsub):
    @pl.when(pl.program_id(1) == 0)
    def _():
        s_ref[...] = jnp.zeros_like(s_ref)
        head_ref[0:GDN_PAD, :] = jnp.zeros((GDN_PAD, head_ref.shape[1]), F32)

    tc = nc * CHUNK

    def tile(s, carry):
        rows = pl.ds(pl.multiple_of(s * tc, tc), tc)
        _gdn_tile(qkv_ref.at[rows], og_ref.at[rows], ba_ref.at[rows], cw_ref, shift_ref, alog_ref, dtb_ref,
                  ng_ref, o_ref.at[rows], head_ref, s_ref, nc=nc)
        return carry

    lax.fori_loop(0, nsub, tile, 0)


def _gdn_tile(qkv_ref, og_ref, ba_ref, cw_ref, shift_ref, alog_ref, dtb_ref, ng_ref, o_ref,
              head_ref, s_ref, *, nc):
    hd, dk, dv, kc = GDN_HEADS, GDN_DK, GDN_DV, GDN_CONV
    tc = nc * CHUNK
    pad = GDN_PAD

    head_ref[pad:2 * pad, :] = qkv_ref[0:2 * pad, :].astype(F32)[0:pad]
    lagged = jnp.dot(shift_ref[...], qkv_ref[...], preferred_element_type=F32)

    ri = lax.broadcasted_iota(jnp.int32, (CHUNK, CHUNK), 0)
    ci = lax.broadcasted_iota(jnp.int32, (CHUNK, CHUNK), 1)
    incl = ri >= ci
    strict = ri > ci
    ltri = incl.astype(BF16)
    eye = (ri == ci).astype(F32)
    cw = [cw_ref[i] for i in range(kc)]
    ng = ng_ref[...]

    ba = ba_ref[...].astype(F32)
    beta_all = _sigmoid(ba)
    g_all = -jnp.exp(alog_ref[...]) * _softplus(ba + dtb_ref[...])

    cr = lambda c: slice(c * CHUNK, (c + 1) * CHUNK)
    gam_c = [_select_rows(ltri, g_all[cr(c), :]) for c in range(nc)]
    gam_tc = [g.T for g in gam_c]
    groups = [list(range(g0, min(g0 + GDN_GROUP_CHUNKS, nc))) for g0 in range(0, nc, GDN_GROUP_CHUNKS)]

    def conv_piece(chunks, col):
        lanes = slice(col * dk, (col + 1) * dk)
        r0, nr = chunks[0] * CHUNK, len(chunks) * CHUNK
        blocks = lambda a: a.reshape(nr // 8, 8, dk)
        acc = cw[kc - 1][:, lanes] * blocks(qkv_ref[r0:r0 + nr, lanes].astype(F32))
        for i in range(kc - 1):
            lag = kc - 1 - i
            acc = acc + cw[i][:, lanes] * blocks(lagged[(lag - 1) * tc + r0:(lag - 1) * tc + r0 + nr, lanes])
        acc = acc.reshape(nr, dk)
        if r0 == 0:
            head = cw[kc - 1][:, lanes] * head_ref[pad:2 * pad, lanes]
            for i in range(kc - 1):
                lag = kc - 1 - i
                head = head + cw[i][:, lanes] * head_ref[pad - lag:2 * pad - lag, lanes]
            acc = jnp.concatenate([head, acc[pad:]], axis=0)
        y = _silu(acc)
        if col < 2 * hd:
            y = y * lax.rsqrt(jnp.sum(y * y, axis=-1, keepdims=True) + EPS)
        if col < hd:
            y = y * (dk ** -0.5)
        return [y[i * CHUNK:(i + 1) * CHUNK] for i in range(len(chunks))]

    def chain_stages(chunks, cols, res):
        pairs = [(ci, c, h) for ci, c in enumerate(chunks) for h in range(hd)]
        n = len(pairs)
        q = [cols[h][ci] for ci, c, h in pairs]
        k = [cols[hd + h][ci] for ci, c, h in pairs]
        v = [cols[2 * hd + h][ci] for ci, c, h in pairs]
        beta = [beta_all[cr(c), h:h + 1] for ci, c, h in pairs]
        gam = [gam_c[c][:, hd + h:hd + h + 1] for ci, c, h in pairs]
        gam_row = [gam_tc[c][hd + h:hd + h + 1, :] for ci, c, h in pairs]
        dmask = [jnp.where(incl, jnp.exp(gam[i] - gam_row[i]), 0.0) for i in range(n)]
        k_beta = [k[i] * beta[i] for i in range(n)]
        egam = [jnp.exp(gam[i]) for i in range(n)]
        kk = [_mm_nt(jnp.concatenate([k_beta[i], q[i]], axis=0), k[i]) for i in range(n)]
        low = [jnp.where(strict, kk[i][:CHUNK] * dmask[i], 0.0) for i in range(n)]
        res["attn"] = [kk[i][CHUNK:] * dmask[i] for i in range(n)]
        rhs = [jnp.concatenate([v[i] * beta[i], k_beta[i] * egam[i]], axis=1).astype(BF16) for i in range(n)]
        yield
        lowb = [low[i].astype(BF16) for i in range(n)]
        pw = [_mm(lowb[i], lowb[i]).astype(BF16) for i in range(n)]
        tinv = [eye - low[i] for i in range(n)]
        for s in range(5):
            yield
            if s < 4:
                stk = [_mm(jnp.concatenate([tinv[i].astype(BF16), pw[i]], axis=0), pw[i]) for i in range(n)]
                tinv = [tinv[i] + stk[i][:CHUNK] for i in range(n)]
                pw = [stk[i][CHUNK:].astype(BF16) for i in range(n)]
            else:
                tinv = [tinv[i] + _mm(tinv[i], pw[i]) for i in range(n)]
        yield
        sol = [_mm(tinv[i], rhs[i]).astype(BF16) for i in range(n)]
        q_dec = [q[i] * egam[i] for i in range(n)]
        gam_last = [gam[i][CHUNK - 1:CHUNK, :] for i in range(n)]
        k_dec = [k[i] * jnp.exp(gam_last[i] - gam[i]) for i in range(n)]
        yield
        ks = [_mm_tn(k_dec[i], sol[i]) for i in range(n)]
        aw = [_mm(res["attn"][i], sol[i]) for i in range(n)]
        res["lhs"] = [jnp.concatenate([ks[i][:, dv:], q_dec[i] - aw[i][:, dv:]], axis=0).astype(BF16)
                      for i in range(n)]
        res["s_add"] = [ks[i][:, :dv] for i in range(n)]
        res["o_add"] = [aw[i][:, :dv] for i in range(n)]
        res["decay"] = [jnp.exp(gam_last[i]) for i in range(n)]

    ncols = 3 * hd
    cols = [conv_piece(groups[0], col) for col in range(ncols)]
    st = [s_ref[h] for h in range(hd)]
    for gi, chunks in enumerate(groups):
        res = {}
        nxt = groups[gi + 1] if gi + 1 < len(groups) else None
        nxt_cols = []
        for _ in chain_stages(chunks, cols, res):
            if nxt is not None and len(nxt_cols) < ncols:
                nxt_cols.append(conv_piece(nxt, len(nxt_cols)))
        while nxt is not None and len(nxt_cols) < ncols:
            nxt_cols.append(conv_piece(nxt, len(nxt_cols)))
        cols = nxt_cols
        lhs, s_add, o_add, decay = res["lhs"], res["s_add"], res["o_add"], res["decay"]
        for ci, c in enumerate(chunks):
            ids = [ci * hd + h for h in range(hd)]
            prod = [_mm(lhs[i], st[h]) for h, i in enumerate(ids)]
            o = [prod[h][dk:] + o_add[i] for h, i in enumerate(ids)]
            st = [decay[i] * st[h] - prod[h][:dk] + s_add[i] for h, i in enumerate(ids)]
            for h in range(hd):
                ms = jnp.mean(o[h] * o[h], axis=-1, keepdims=True)
                cl = slice(h * dv, (h + 1) * dv)
                o_ref[cr(c), cl] = (o[h] * lax.rsqrt(ms + EPS) * ng * _silu(og_ref[cr(c), cl].astype(F32))).astype(o_ref.dtype)
    for h in range(hd):
        s_ref[h] = st[h]
    head_ref[0:pad, :] = qkv_ref[tc - 2 * pad:tc, :].astype(F32)[pad:]


def gdn_mix(zm, conv_w_pad, shift_m, alog_row, dtb_row, ng, bsz, seq, tc, nsub):
    t = zm.shape[0]
    blk_rows = tc * nsub
    per_b = seq // blk_rows
    hd, dk, dv = GDN_HEADS, GDN_DK, GDN_DV
    qkv_w = 3 * hd * dk

    def zspec(width, off):
        blk = off // width
        return pl.BlockSpec((blk_rows, width), lambda b, i: (b * per_b + i, blk))

    def full(a):
        return pl.BlockSpec(a.shape, lambda b, i: (0,) * a.ndim)

    return pl.pallas_call(
        functools.partial(_gdn_kernel, nc=tc // CHUNK, nsub=nsub),
        grid=(bsz, per_b),
        in_specs=[zspec(qkv_w, ZM_DQKV), zspec(hd * dv, ZM_DOG), zspec(LANE, ZM_DBA),
                  full(conv_w_pad), full(shift_m), full(alog_row), full(dtb_row), full(ng)],
        out_specs=pl.BlockSpec((blk_rows, hd * dv), lambda b, i: (b * per_b + i, 0)),
        out_shape=jax.ShapeDtypeStruct((t, hd * dv), F32),
        scratch_shapes=[pltpu.VMEM((2 * GDN_PAD, qkv_w), F32), pltpu.VMEM((hd, dk, dv), F32)],
        compiler_params=_params("parallel", "arbitrary"),
        name="gdn_mix",
    )(zm, zm, zm, conv_w_pad, shift_m, alog_row, dtb_row, ng)


def _merge_kernel(h3_ref, ygla_ref, ygdn_ref, yst_ref, wglu_ref, wzg_ref, wbg_ref, wbs_ref, wbd_ref, o_ref,
                  s5t_ref, *, jt, bw, tn):
    n_rows = jt * LANE
    d = h3_ref.shape[-1]
    h = h3_ref[...].reshape(n_rows, d).astype(BF16)
    a_gla = ygla_ref[...].reshape(n_rows, bw).astype(BF16)
    a_gdn = ygdn_ref[...].reshape(n_rows, bw).astype(BF16)
    for jj in range(jt):
        ys = _gelu_tanh(yst_ref[jj].astype(F32))
        glu = _mm_tn(ys, wglu_ref[...])
        s5t_ref[jj * LANE:(jj + 1) * LANE, :] = glu[:, :bw] * _sigmoid(glu[:, bw:])
    a_s5 = pltpu.einshape("(jr)d->rjd", s5t_ref[...], j=jt).reshape(n_rows, bw).astype(BF16)

    def branch(b, cols, a, wb_ref):
        gate = _sigmoid(jnp.dot(h, wzg_ref[:, b * d + cols.start:b * d + cols.stop], preferred_element_type=F32))
        return gate * jnp.dot(a, wb_ref[:, cols], preferred_element_type=F32)

    for c0 in range(0, d, tn):
        cols = slice(c0, c0 + tn)
        m = branch(0, cols, a_gla, wbg_ref) + branch(1, cols, a_s5, wbs_ref) + branch(2, cols, a_gdn, wbd_ref)
        o_ref[:, :, cols] = m.reshape(LANE, jt, tn)


def merge_gate(h3, ygla3, ygdn3, yst, wglu, wzg, wbg, wbs, wbd, layer, jt, tn):
    r, _, d = h3.shape
    bw = ygla3.shape[2]

    def view(width):
        return pl.BlockSpec((LANE, jt, width), lambda i, j: (i, j, 0))

    def layer_w(a):
        return pl.BlockSpec((None,) + a.shape[1:], lambda i, j: (layer, 0, 0))

    return pl.pallas_call(
        functools.partial(_merge_kernel, jt=jt, bw=bw, tn=tn),
        grid=(r // LANE, CHUNK // jt),
        in_specs=[view(d), view(bw), view(bw),
                  pl.BlockSpec((jt, yst.shape[1], LANE), lambda i, j: (j, 0, i)),
                  layer_w(wglu), layer_w(wzg), layer_w(wbg), layer_w(wbs), layer_w(wbd)],
        out_specs=view(d),
        out_shape=jax.ShapeDtypeStruct((r, CHUNK, d), F32),
        scratch_shapes=[pltpu.VMEM((jt * LANE, bw), F32)],
        compiler_params=_params("parallel", "parallel"),
        name="merge_gate",
    )(h3, ygla3, ygdn3, yst, wglu, wzg, wbg, wbs, wbd)


def _ffn_kernel(x_ref, m_ref, wout_ref, gt1_ref, g_ref, sc_ref, sh_ref, gt2_ref, win_ref, wo_ref,
                fg_ref, o_ref, *, final_norm, tf):
    dff = wo_ref.shape[0]
    x1 = x_ref[...] + gt1_ref[...] * _mm(m_ref[...], wout_ref[...])
    h = _norm_mod(x1, g_ref[...], sc_ref[...], sh_ref[...]).astype(BF16)
    acc = None
    for f0 in range(0, dff, tf):
        a = jnp.dot(h, win_ref[:, f0:f0 + tf], preferred_element_type=F32)
        b = jnp.dot(h, win_ref[:, dff + f0:dff + f0 + tf], preferred_element_type=F32)
        part = _mm(_silu(a) * b, wo_ref[f0:f0 + tf, :])
        acc = part if acc is None else acc + part
    y = x1 + gt2_ref[...] * acc
    if final_norm:
        ms = jnp.mean(y * y, axis=-1, keepdims=True)
        y = y * lax.rsqrt(ms + EPS) * fg_ref[...]
    o_ref[...] = y


def ffn_block(x2, merged, w_o, gt1, g, sc, sh, gt2, w_in, w_out, final_g, layer, seq, tm, tf, final_norm):
    t, d = x2.shape
    dff = w_out.shape[1]
    per_b = seq // tm

    def bspec():
        return pl.BlockSpec((None, 1, d), lambda i: (i // per_b, 0, 0))

    def const(a):
        return pl.BlockSpec(a.shape, lambda i: (0,) * a.ndim)

    def layer_w(a):
        return pl.BlockSpec((None,) + a.shape[1:], lambda i: (layer, 0, 0))

    row = pl.BlockSpec((tm, d), lambda i: (i, 0))
    return pl.pallas_call(
        functools.partial(_ffn_kernel, final_norm=final_norm, tf=tf),
        grid=(t // tm,),
        in_specs=[row, row, layer_w(w_o), bspec(), const(g), bspec(), bspec(), bspec(),
                  layer_w(w_in), layer_w(w_out), const(final_g)],
        out_specs=row,
        out_shape=jax.ShapeDtypeStruct((t, d), F32),
        compiler_params=_params("parallel"),
        name="ffn_block",
    )(x2, merged, w_o, gt1, g, sc, sh, gt2, w_in, w_out, final_g)


def _wprep_kernel(w_ref, zm_ref, zg_ref, s5t_ref, *, segs, pads, zg_off, s5_off):
    for dst, width in pads:
        zm_ref[0, :, dst:dst + width] = jnp.zeros((zm_ref.shape[1], width), BF16)
    for src, width, dst in segs:
        zm_ref[0, :, dst:dst + width] = w_ref[0, :, src:src + width].astype(BF16)
    zg_ref[0] = w_ref[0, :, zg_off:zg_off + zg_ref.shape[2]].astype(BF16)
    s5t_ref[0] = w_ref[0, :, s5_off:s5_off + s5t_ref.shape[1]].astype(F32).T.astype(BF16)


def prep_in_weights(w_in, segs, pads, zg_off, zg_w, s5_off, s5_w, tr):
    depth, d, d_in = w_in.shape
    return pl.pallas_call(
        functools.partial(_wprep_kernel, segs=segs, pads=pads, zg_off=zg_off, s5_off=s5_off),
        grid=(depth, d // tr),
        in_specs=[pl.BlockSpec((1, tr, d_in), lambda i, j: (i, j, 0))],
        out_specs=[pl.BlockSpec((1, tr, ZM_WIDTH), lambda i, j: (i, j, 0)),
                   pl.BlockSpec((1, tr, zg_w), lambda i, j: (i, j, 0)),
                   pl.BlockSpec((1, s5_w, tr), lambda i, j: (i, 0, j))],
        out_shape=[jax.ShapeDtypeStruct((depth, d, ZM_WIDTH), BF16),
                   jax.ShapeDtypeStruct((depth, d, zg_w), BF16),
                   jax.ShapeDtypeStruct((depth, s5_w, d), BF16)],
        compiler_params=_params("parallel", "parallel"),
        name="prep_in_weights",
    )(w_in)


def _forward(x, c, w_ada, b_ada, norm1_g, w_in, gla_w_lr, gla_b_lr, gla_norm_g,
             s5_lambda_re, s5_lambda_im, s5_log_dt, s5_b_re, s5_b_im, s5_c_re, s5_c_im,
             s5_d, s5_w_glu, gdn_conv_w, gdn_a_log, gdn_dt_bias, gdn_norm_g,
             w_branch_gla, w_branch_s5, w_branch_gdn, w_out, norm2_g, w_ffn_in, w_ffn_out,
             final_g):
    bsz, seq, d = x.shape
    depth = w_ada.shape[0]
    t = bsz * seq
    nchunk = seq // CHUNK
    r = bsz * nchunk
    gqk, gw = GLA_HEADS * GLA_DK, GLA_HEADS * GLA_DV
    s5w = S5_GROUPS * S5_GROUP_CH
    dqkv, dw = 3 * GDN_HEADS * GDN_DK, GDN_HEADS * GDN_DV

    mod = ada_modulation(c, w_ada, b_ada).reshape(depth, bsz, 6, 1, d)
    s5m, s5et, s5ft, s5ac = s5_params(s5_lambda_re, s5_lambda_im, s5_log_dt, s5_b_re, s5_b_im,
                                      s5_c_re, s5_c_im, s5_d)

    o_gq, o_gk, o_gv = 0, gqk, 2 * gqk
    o_glr = o_gv + gw
    o_gog = o_glr + GLA_LOWRANK
    o_s5 = o_gog + gw
    o_dqkv = o_s5 + s5w
    o_dbeta = o_dqkv + dqkv
    o_da = o_dbeta + GDN_HEADS
    o_dog = o_da + GDN_HEADS
    o_zg = o_dog + dw

    segs = ((o_gq, gqk, ZM_GQ), (o_gk, gqk, ZM_GK), (o_gv, gw, ZM_GV), (o_gog, gw, ZM_GOG),
            (o_dqkv, dqkv, ZM_DQKV), (o_dog, dw, ZM_DOG), (o_glr, GLA_LOWRANK, ZM_GLR),
            (o_dbeta, 2 * GDN_HEADS, ZM_DBA))
    pads = ((ZM_GLR, LANE), (ZM_DBA, LANE))
    w_in_b = jnp.pad(w_in.astype(BF16), ((0, 0), (0, 0), (0, -w_in.shape[2] % LANE)))
    w_zm_all, w_zg_all, w_s5t_all = prep_in_weights(w_in_b, segs, pads, o_zg, N_BRANCH * d, o_s5, s5w, tr=256)
    s5_w_glu, w_branch_gla, w_branch_s5, w_branch_gdn, w_out, w_ffn_in, w_ffn_out = [
        a.astype(BF16) for a in (s5_w_glu, w_branch_gla, w_branch_s5, w_branch_gdn, w_out, w_ffn_in, w_ffn_out)]

    gdn_tc = 256
    lag_row = jnp.arange((GDN_CONV - 1) * gdn_tc)
    gdn_shift = (jnp.arange(gdn_tc)[None, :]
                 == (lag_row % gdn_tc - lag_row // gdn_tc - 1)[:, None]).astype(BF16)

    x2 = x.reshape(t, d)
    for i in range(depth):

        sh1, sc1, gt1, sh2, sc2, gt2 = [mod[i, :, k] for k in range(6)]
        g1 = norm1_g[i].reshape(1, d)

        zm, hf = norm_mod_matmul(x2, g1, sc1, sh1, w_zm_all, i, seq, tm=512, tn=3840)
        h3 = hf.reshape(r, CHUNK, d)
        ut = s5_inproj(h3, w_s5t_all, i, jt=8, rt=r)

        wlr_pad = jnp.pad(gla_w_lr[i], ((0, LANE - GLA_LOWRANK), (0, 0)))
        y_gla = gla_mix(zm, wlr_pad, gla_b_lr[i].reshape(1, gqk), gla_norm_g[i].reshape(1, GLA_DV),
                        bsz, seq, tc=512, nsub=2)
        yst = s5_mix(ut, s5m, s5et, s5ft, s5ac, i, nchunk, gsub=4)
        conv_pad = jnp.broadcast_to(gdn_conv_w[i][:, None, :], (GDN_CONV, 8, dqkv))
        alog_row = jnp.pad(gdn_a_log[i], (GDN_HEADS, LANE - 2 * GDN_HEADS)).reshape(1, LANE)
        dtb_row = jnp.pad(gdn_dt_bias[i], (GDN_HEADS, LANE - 2 * GDN_HEADS)).reshape(1, LANE)
        y_gdn = gdn_mix(zm, conv_pad, gdn_shift, alog_row, dtb_row, gdn_norm_g[i].reshape(1, GDN_DV),
                        bsz, seq, tc=gdn_tc, nsub=4)

        merged = merge_gate(h3, y_gla.reshape(r, CHUNK, gw), y_gdn.reshape(r, CHUNK, dw),
                            yst, s5_w_glu, w_zg_all, w_branch_gla, w_branch_s5, w_branch_gdn, i,
                            jt=8, tn=256)
        x2 = ffn_block(x2, merged.reshape(t, d), w_out, gt1,
                       norm2_g[i].reshape(1, d), sc2, sh2, gt2,
                       w_ffn_in, w_ffn_out,
                       final_g.reshape(1, d), i, seq, tm=512, tf=256, final_norm=(i == depth - 1))
    return x2.reshape(bsz, seq, d)


def kernel(x, c, w_ada, b_ada, norm1_g, w_in, gla_w_lr, gla_b_lr, gla_norm_g, s5_lambda_re, s5_lambda_im, s5_log_dt, s5_b_re, s5_b_im, s5_c_re, s5_c_im, s5_d, s5_w_glu, gdn_conv_w, gdn_a_log, gdn_dt_bias, gdn_norm_g, w_branch_gla, w_branch_s5, w_branch_gdn, w_out, norm2_g, w_ffn_in, w_ffn_out, final_g):
    return _forward(x, c, w_ada, b_ada, norm1_g, w_in, gla_w_lr, gla_b_lr, gla_norm_g,
                    s5_lambda_re, s5_lambda_im, s5_log_dt, s5_b_re, s5_b_im, s5_c_re, s5_c_im,
                    s5_d, s5_w_glu, gdn_conv_w, gdn_a_log, gdn_dt_bias, gdn_norm_g,
                    w_branch_gla, w_branch_s5, w_branch_gdn, w_out, norm2_g, w_ffn_in, w_ffn_out,
                    final_g)
```

```python
import functools
import math

import jax
import jax.numpy as jnp
from jax import lax
from jax.experimental import pallas as pl
from jax.experimental.pallas import tpu as pltpu

F32 = jnp.float32
BF16 = jnp.bfloat16
HI = lax.Precision.HIGHEST

EPS = 1e-6
CHUNK = 64
LANE = 128
VMEM_LIMIT = 56 * 1024 * 1024

GLA_HEADS, GLA_DK, GLA_DV, GLA_LOWRANK = 4, 64, 128, 16
GLA_GATE_NORM = 16.0
S5_GROUPS, S5_GROUP_CH, S5_STATE = 32, 16, 64
GDN_HEADS, GDN_DK, GDN_DV, GDN_CONV = 4, 128, 128, 4
N_BRANCH = 3
GDN_GROUP_CHUNKS = 4

INPROJ_TM = 512
TOK_TILE = 8
GLA_TC, GLA_NSUB = 512, 2
GDN_TC, GDN_NSUB = 256, 4
S5_GSUB = 4
MERGE_TN = 256
FFN_TM, FFN_TF = 512, 256
WPREP_ROWS = 256

ZM_GQ, ZM_GK, ZM_GV, ZM_GOG = 0, 256, 512, 1024
ZM_DQKV, ZM_DOG, ZM_GLR, ZM_DBA = 1536, 3072, 3584, 3712
ZM_WIDTH = 3840


def _mm(a, b):
    return jnp.dot(a.astype(BF16), b.astype(BF16), preferred_element_type=F32)


def _mm_nt(a, b):
    return lax.dot_general(a.astype(BF16), b.astype(BF16), (((1,), (1,)), ((), ())),
                           preferred_element_type=F32)


def _mm_tn(a, b):
    return lax.dot_general(a.astype(BF16), b.astype(BF16), (((0,), (0,)), ((), ())),
                           preferred_element_type=F32)


def _mm_hi(a, b):
    return jnp.dot(a, b, precision=HI, preferred_element_type=F32)


def _split3(x):
    hi = x.astype(BF16)
    r1 = x - hi.astype(F32)
    mid = r1.astype(BF16)
    return hi, mid, (r1 - mid.astype(F32)).astype(BF16)


def _select_rows(m01, x):
    return sum(jnp.dot(m01, t, preferred_element_type=F32) for t in _split3(x))


def _select_cols(x, m01):
    return sum(jnp.dot(t, m01, preferred_element_type=F32) for t in _split3(x))


def _sigmoid(x):
    return 0.5 + 0.5 * jnp.tanh(0.5 * x)


def _silu(x):
    h = 0.5 * x
    return h + h * jnp.tanh(h)


def _softplus(x):
    return jnp.maximum(x, 0.0) + jnp.log(1.0 + jnp.exp(-jnp.abs(x)))


def _log_sigmoid(x):
    return -_softplus(-x)


def _gelu_tanh(x):
    c = math.sqrt(2.0 / math.pi)
    return 0.5 * x * (1.0 + jnp.tanh(c * (x + 0.044715 * (x * x * x))))


def _norm_mod(x, g, sc, sh):
    ms = jnp.mean(x * x, axis=-1, keepdims=True)
    return (x * lax.rsqrt(ms + EPS) * g) * (1.0 + sc) + sh


def _tok_major(ref):
    return pltpu.einshape("rjd->(jr)d", ref[...])


def _params(*sem):
    return pltpu.CompilerParams(dimension_semantics=sem, vmem_limit_bytes=VMEM_LIMIT)


def _ada_kernel(c_ref, w_ref, b_ref, o_ref):
    c = c_ref[...]
    o_ref[0] = jnp.dot(_silu(c), w_ref[0], preferred_element_type=F32) + b_ref[0]


def ada_modulation(c, w_ada, b_ada):
    depth, d, d6 = w_ada.shape
    bsz = c.shape[0]
    rows = -(-bsz // 8) * 8
    tw = d6 // 2
    c_pad = jnp.pad(c, ((0, rows - bsz), (0, 0)))
    out = pl.pallas_call(
        _ada_kernel,
        grid=(depth, d6 // tw),
        in_specs=[pl.BlockSpec((rows, d), lambda i, j: (0, 0)),
                  pl.BlockSpec((1, d, tw), lambda i, j: (i, 0, j)),
                  pl.BlockSpec((1, 1, tw), lambda i, j: (i, 0, j))],
        out_specs=pl.BlockSpec((1, rows, tw), lambda i, j: (i, 0, j)),
        out_shape=jax.ShapeDtypeStruct((depth, rows, d6), F32),
        compiler_params=_params("parallel", "parallel"),
        name="ada_modulation",
    )(c_pad, w_ada, b_ada.reshape(depth, 1, d6))
    return out[:, :bsz]


def _inproj_kernel(x_ref, g_ref, sc_ref, sh_ref, w_ref, o_ref, hf_ref, h_ref):
    @pl.when(pl.program_id(1) == 0)
    def _():
        h = _norm_mod(x_ref[...], g_ref[...], sc_ref[...], sh_ref[...])
        hf_ref[...] = h
        h_ref[...] = h.astype(BF16)

    o_ref[...] = jnp.dot(h_ref[...], w_ref[...], preferred_element_type=F32).astype(o_ref.dtype)


def norm_mod_matmul(x2, g, sc, sh, w, layer, seq, tm, tn):
    t, d = x2.shape
    c = w.shape[2]
    per_b = seq // tm
    return pl.pallas_call(
        _inproj_kernel,
        grid=(t // tm, c // tn),
        in_specs=[pl.BlockSpec((tm, d), lambda i, j: (i, 0)),
                  pl.BlockSpec((1, d), lambda i, j: (0, 0)),
                  pl.BlockSpec((None, 1, d), lambda i, j: (i // per_b, 0, 0)),
                  pl.BlockSpec((None, 1, d), lambda i, j: (i // per_b, 0, 0)),
                  pl.BlockSpec((None, d, tn), lambda i, j: (layer, 0, j))],
        out_specs=[pl.BlockSpec((tm, tn), lambda i, j: (i, j)), pl.BlockSpec((tm, d), lambda i, j: (i, 0))],
        out_shape=[jax.ShapeDtypeStruct((t, c), BF16), jax.ShapeDtypeStruct((t, d), F32)],
        scratch_shapes=[pltpu.VMEM((tm, d), BF16)],
        compiler_params=_params("parallel", "arbitrary"),
        name="norm_mod_matmul",
    )(x2, g, sc, sh, w)


def _s5_inproj_kernel(h_ref, wt_ref, o_ref, *, jt):
    rt = h_ref.shape[0]
    h = _tok_major(h_ref).astype(BF16)
    ut = lax.dot_general(wt_ref[...], h, (((1,), (1,)), ((), ())), preferred_element_type=F32)
    for jj in range(jt):
        o_ref[jj] = ut[:, jj * rt:(jj + 1) * rt].astype(BF16)


def s5_inproj(h3, wt, layer, jt, rt):
    r, _, d = h3.shape
    s5w = wt.shape[1]
    return pl.pallas_call(
        functools.partial(_s5_inproj_kernel, jt=jt),
        grid=(r // rt, CHUNK // jt),
        in_specs=[pl.BlockSpec((rt, jt, d), lambda i, j: (i, j, 0)),
                  pl.BlockSpec((None, s5w, d), lambda i, j: (layer, 0, 0))],
        out_specs=pl.BlockSpec((jt, s5w, rt), lambda i, j: (j, 0, i)),
        out_shape=jax.ShapeDtypeStruct((CHUNK, s5w, r), BF16),
        compiler_params=_params("parallel", "parallel"),
        name="s5_inproj",
    )(h3, wt)


S5_CW = CHUNK * S5_GROUP_CH


def _cpow(lr_dt, li_dt, e):
    mag = jnp.exp(lr_dt * e)
    ang = li_dt * e
    return mag * jnp.cos(ang), mag * jnp.sin(ang)


def _s5_param_kernel(ldt_ref, lrc_ref, lic_ref, lrr_ref, lir_ref, bre_ref, bim_ref,
                     c1_ref, c2_ref, dcol_ref, m_ref, et_ref, ft_ref, ac_ref):
    p, cw, h = S5_STATE, S5_CW, S5_GROUP_CH
    dt = jnp.exp(ldt_ref[0])
    lrc, lic = lrc_ref[0], lic_ref[0]
    ab_re, ab_im = _cpow(lrc * dt, lic * dt, 1.0)
    den = lrc * lrc + lic * lic
    nr, ni = ab_re - 1.0, ab_im
    w_re = (nr * lrc + ni * lic) / den
    w_im = (ni * lrc - nr * lic) / den
    trow = lax.broadcasted_iota(jnp.int32, (h, cw), 0)
    tcol = lax.broadcasted_iota(jnp.int32, (h, cw), 1)
    tile_l = (trow == tcol % h).astype(BF16)
    bre, bim = _select_cols(bre_ref[0], tile_l), _select_cols(bim_ref[0], tile_l)
    bb_re = w_re * bre - w_im * bim
    bb_im = w_re * bim + w_im * bre
    lane = lax.broadcasted_iota(jnp.int32, (1, LANE), 1)
    e_m = jnp.where(lane < CHUNK, CHUNK - 1 - lane, 0).astype(F32)
    pd_re, pd_im = _cpow(lrc * dt, lic * dt, e_m)
    xrow = lax.broadcasted_iota(jnp.int32, (LANE, cw), 0)
    xcol = lax.broadcasted_iota(jnp.int32, (LANE, cw), 1)
    expand_l = (xrow == xcol // h).astype(BF16)
    p_re, p_im = _select_cols(pd_re, expand_l), _select_cols(pd_im, expand_l)
    e_re = p_re * bb_re - p_im * bb_im
    e_im = p_re * bb_im + p_im * bb_re
    et = jnp.concatenate([e_re, e_im], axis=0)
    et_ref[0] = et.astype(BF16)
    a_re, a_im = _cpow(lrc * dt, lic * dt, float(CHUNK))
    ac_ref[0] = jnp.concatenate([a_re, a_im], axis=0)
    sgn = jnp.where(lax.broadcasted_iota(jnp.int32, (1, 2 * p), 1) < p, 1.0, -1.0)
    krev = _mm_hi(c1_ref[0] * sgn, et)
    row = lax.broadcasted_iota(jnp.int32, (h, cw), 0)
    col = lax.broadcasted_iota(jnp.int32, (h, cw), 1)
    krev = krev + jnp.where(col == (cw - h) + row, dcol_ref[0], 0.0)
    rrev = jnp.concatenate([krev, jnp.zeros_like(krev)], axis=1)
    per_tile = LANE // h
    rolled = [rrev if r == 0 else pltpu.roll(rrev, 2 * cw - r * h, axis=1) for r in range(per_tile)]
    for i in range(CHUNK):
        s = (CHUNK - 1 - i) * h
        a, r = s // LANE, (s % LANE) // h
        m_ref[0, i * h:(i + 1) * h, :] = rolled[r][:, a * LANE:a * LANE + cw].astype(BF16)
    f_i = (lax.broadcasted_iota(jnp.int32, (CHUNK, 1), 0) + 1).astype(F32)
    qd_re, qd_im = _cpow(lrr_ref[0] * dt, lir_ref[0] * dt, f_i)
    ft = (c1_ref[0] * sgn)[None] * qd_re[:, None, :] - c2_ref[0][None] * qd_im[:, None, :]
    ft_ref[0] = ft.reshape(cw, 2 * p).astype(BF16)


def s5_params(lam_re, lam_im, log_dt, b_re, b_im, c_re, c_im, dpar):
    ng = lam_re.shape[0] * lam_re.shape[1]
    p, h, cw = S5_STATE, S5_GROUP_CH, S5_CW
    lam_re = lam_re.reshape(ng, p)
    lam_im = lam_im.reshape(ng, p)
    c_re = c_re.reshape(ng, h, p)
    c_im = c_im.reshape(ng, h, p)
    c1 = jnp.concatenate([c_re, c_im], axis=-1)
    c2 = jnp.concatenate([c_im, c_re], axis=-1)
    args = (log_dt.reshape(ng, 1, 1),
            lam_re.reshape(ng, p, 1), lam_im.reshape(ng, p, 1),
            jnp.tile(lam_re.reshape(ng, 1, p), (1, 1, 2)), jnp.tile(lam_im.reshape(ng, 1, p), (1, 1, 2)),
            b_re.reshape(ng, p, h), b_im.reshape(ng, p, h), c1, c2,
            dpar.reshape(ng, h, 1))

    def spec(a):
        return pl.BlockSpec((1,) + a.shape[1:], lambda i: (i, 0, 0))

    return pl.pallas_call(
        _s5_param_kernel,
        grid=(ng,),
        in_specs=[spec(a) for a in args],
        out_specs=[pl.BlockSpec((1, cw, cw), lambda i: (i, 0, 0)),
                   pl.BlockSpec((1, 2 * p, cw), lambda i: (i, 0, 0)),
                   pl.BlockSpec((1, cw, 2 * p), lambda i: (i, 0, 0)),
                   pl.BlockSpec((1, 2 * p, 1), lambda i: (i, 0, 0))],
        out_shape=[jax.ShapeDtypeStruct((ng, cw, cw), BF16),
                   jax.ShapeDtypeStruct((ng, 2 * p, cw), BF16),
                   jax.ShapeDtypeStruct((ng, cw, 2 * p), BF16),
                   jax.ShapeDtypeStruct((ng, 2 * p, 1), F32)],
        compiler_params=_params("parallel"),
        name="s5_params",
    )(*args)


S5_MBLOCKS = 4


def _s5_mix_kernel(u_ref, *refs, nchunk, gsub):
    m_refs, (et_ref, ft_ref, ac_ref, y_ref) = refs[:S5_MBLOCKS], refs[S5_MBLOCKS:]
    p, h = S5_STATE, S5_GROUP_CH
    r = u_ref.shape[-1]
    ks = range(gsub)
    u = [u_ref[:, k * h:(k + 1) * h, :].reshape(S5_CW, r) for k in ks]
    s = [jnp.dot(et_ref[k], u[k], preferred_element_type=F32) for k in ks]
    s_re, s_im = [s[k][:p] for k in ks], [s[k][p:] for k in ks]
    a_re, a_im = [ac_ref[k][:p] for k in ks], [ac_ref[k][p:] for k in ks]
    n_idx = lax.broadcasted_iota(jnp.int32, (1, r), 1) % nchunk
    shift = 1
    while shift < nchunk:
        keep = n_idx >= shift
        t_re = [jnp.where(keep, pltpu.roll(s_re[k], shift, axis=1), 0.0) for k in ks]
        t_im = [jnp.where(keep, pltpu.roll(s_im[k], shift, axis=1), 0.0) for k in ks]
        s_re, s_im = ([s_re[k] + a_re[k] * t_re[k] - a_im[k] * t_im[k] for k in ks],
                      [s_im[k] + a_re[k] * t_im[k] + a_im[k] * t_re[k] for k in ks])
        a_re, a_im = ([a_re[k] * a_re[k] - a_im[k] * a_im[k] for k in ks], [2.0 * a_re[k] * a_im[k] for k in ks])
        shift *= 2
    keep = n_idx >= 1
    for k in ks:
        h_prev = jnp.concatenate([jnp.where(keep, pltpu.roll(s_re[k], 1, axis=1), 0.0),
                                  jnp.where(keep, pltpu.roll(s_im[k], 1, axis=1), 0.0)], axis=0)
        rb_rows = S5_CW // S5_MBLOCKS
        y = jnp.concatenate([jnp.dot(m_refs[rb][k], u[k][:(rb + 1) * rb_rows], preferred_element_type=F32)
                             for rb in range(S5_MBLOCKS)], axis=0)
        y = y + jnp.dot(ft_ref[k], h_prev.astype(BF16), preferred_element_type=F32)
        y_ref[:, k * h:(k + 1) * h, :] = y.reshape(CHUNK, h, r).astype(y_ref.dtype)


def s5_mix(ut, m, et, ft, ac, layer, nchunk, gsub):
    _, s5w, r = ut.shape
    g, h, p, cw = S5_GROUPS, S5_GROUP_CH, S5_STATE, S5_CW
    base = layer * g // gsub
    return pl.pallas_call(
        functools.partial(_s5_mix_kernel, nchunk=nchunk, gsub=gsub),
        grid=(g // gsub,),
        in_specs=[pl.BlockSpec((CHUNK, gsub * h, r), lambda i: (0, i, 0)),
                  *[pl.BlockSpec((gsub, cw // S5_MBLOCKS, (rb + 1) * cw // S5_MBLOCKS),
                                 lambda i, rb=rb: (base + i, rb, 0)) for rb in range(S5_MBLOCKS)],
                  pl.BlockSpec((gsub, 2 * p, cw), lambda i: (base + i, 0, 0)),
                  pl.BlockSpec((gsub, cw, 2 * p), lambda i: (base + i, 0, 0)),
                  pl.BlockSpec((gsub, 2 * p, 1), lambda i: (base + i, 0, 0))],
        out_specs=pl.BlockSpec((CHUNK, gsub * h, r), lambda i: (0, i, 0)),
        out_shape=jax.ShapeDtypeStruct((CHUNK, s5w, r), BF16),
        compiler_params=_params("parallel"),
        name="s5_mix",
    )(ut, *([m] * S5_MBLOCKS), et, ft, ac)


def _gla_kernel(q_ref, k_ref, v_ref, og_ref, lr_ref, wlr_ref, blr_ref, ng_ref, o_ref, st_ref, *, nc, nsub):
    @pl.when(pl.program_id(1) == 0)
    def _():
        st_ref[...] = jnp.zeros_like(st_ref)

    tc = nc * CHUNK

    def tile(s, carry):
        rows = pl.ds(pl.multiple_of(s * tc, tc), tc)
        _gla_tile(q_ref.at[rows], k_ref.at[rows], v_ref.at[rows], og_ref.at[rows], lr_ref.at[rows],
                  wlr_ref, blr_ref, ng_ref, o_ref.at[rows], st_ref, nc=nc)
        return carry

    lax.fori_loop(0, nsub, tile, 0)


def _gla_tile(q_ref, k_ref, v_ref, og_ref, lr_ref, wlr_ref, blr_ref, ng_ref, o_ref, st_ref, *, nc):
    hd, dk, dv = GLA_HEADS, GLA_DK, GLA_DV
    ri = lax.broadcasted_iota(jnp.int32, (CHUNK, CHUNK), 0)
    ci = lax.broadcasted_iota(jnp.int32, (CHUNK, CHUNK), 1)
    incl = ri >= ci
    ltri = incl.astype(BF16)
    lane_k = lax.broadcasted_iota(jnp.int32, (1, hd * dk), 1)
    srow = lax.broadcasted_iota(jnp.int32, (hd * dv, hd * dk), 0)
    scol = lax.broadcasted_iota(jnp.int32, (hd * dv, hd * dk), 1)
    same_head = (srow // dv) == (scol // dk)
    wlr, blr, ng = wlr_ref[...], blr_ref[...], ng_ref[...]

    cs = range(nc)
    cr = lambda c: slice(c * CHUNK, (c + 1) * CHUNK)
    g = [_log_sigmoid(_mm(lr_ref[cr(c), :], wlr) + blr) * (1.0 / GLA_GATE_NORM) for c in cs]
    bc = [_select_rows(ltri, g[c]) for c in cs]
    bl = [bc[c][CHUNK - 1:CHUNK, :] for c in cs]
    q_e = [q_ref[cr(c), :].astype(F32) * (dk ** -0.5) * jnp.exp(bc[c]) for c in cs]
    k_e = [k_ref[cr(c), :].astype(F32) * jnp.exp(-bc[c]) for c in cs]
    k_d = [k_ref[cr(c), :].astype(F32) * jnp.exp(bl[c] - bc[c]) for c in cs]
    kv = [jnp.where(same_head, _mm_tn(v_ref[cr(c), :], k_d[c]), 0.0) for c in cs]
    sts = []
    st = st_ref[...]
    for c in cs:
        sts.append(st)
        st = jnp.exp(bl[c]) * st + kv[c]
    st_ref[...] = st
    o_inter = [_mm_nt(q_e[c], sts[c]) for c in cs]
    q_heads = [jnp.concatenate([jnp.where((lane_k // dk) == h, q_e[c], 0.0) for h in range(hd)], axis=0)
               for c in cs]
    sc_all = [_mm_nt(q_heads[c], k_e[c]) for c in cs]
    sc = [[jnp.where(incl, sc_all[c][h * CHUNK:(h + 1) * CHUNK], 0.0) for h in range(hd)] for c in cs]
    for c in cs:
        for h in range(hd):
            cols = slice(h * dv, (h + 1) * dv)
            oh = _mm(sc[c][h], v_ref[cr(c), cols]) + o_inter[c][:, cols]
            ms = jnp.mean(oh * oh, axis=-1, keepdims=True)
            o_ref[cr(c), cols] = (oh * lax.rsqrt(ms + EPS) * ng * _silu(og_ref[cr(c), cols].astype(F32))).astype(o_ref.dtype)


def gla_mix(zm, wlr_pad, blr, ng, bsz, seq, tc, nsub):
    t = zm.shape[0]
    blk_rows = tc * nsub
    per_b = seq // blk_rows
    hd, dk, dv = GLA_HEADS, GLA_DK, GLA_DV

    def zspec(width, off):
        blk = off // width
        return pl.BlockSpec((blk_rows, width), lambda b, i: (b * per_b + i, blk))

    def full(a):
        return pl.BlockSpec(a.shape, lambda b, i: (0,) * a.ndim)

    return pl.pallas_call(
        functools.partial(_gla_kernel, nc=tc // CHUNK, nsub=nsub),
        grid=(bsz, per_b),
        in_specs=[zspec(hd * dk, ZM_GQ), zspec(hd * dk, ZM_GK), zspec(hd * dv, ZM_GV),
                  zspec(hd * dv, ZM_GOG), zspec(LANE, ZM_GLR), full(wlr_pad), full(blr), full(ng)],
        out_specs=pl.BlockSpec((blk_rows, hd * dv), lambda b, i: (b * per_b + i, 0)),
        out_shape=jax.ShapeDtypeStruct((t, hd * dv), F32),
        scratch_shapes=[pltpu.VMEM((hd * dv, hd * dk), F32)],
        compiler_params=_params("parallel", "arbitrary"),
        name="gla_mix",
    )(zm, zm, zm, zm, zm, wlr_pad, blr, ng)


GDN_PAD = 8


def _gdn_kernel(qkv_ref, og_ref, ba_ref, cw_ref, shift_ref, alog_ref, dtb_ref, ng_ref, o_ref,
                head_ref, s_ref, *, nc, nsub):
    @pl.when(pl.program_id(1) == 0)
    def _():
        s_ref[...] = jnp.zeros_like(s_ref)
        head_ref[0:GDN_PAD, :] = jnp.zeros((GDN_PAD, head_ref.shape[1]), F32)

    tc = nc * CHUNK

    def tile(s, carry):
        rows = pl.ds(pl.multiple_of(s * tc, tc), tc)
        _gdn_tile(qkv_ref.at[rows], og_ref.at[rows], ba_ref.at[rows], cw_ref, shift_ref, alog_ref, dtb_ref,
                  ng_ref, o_ref.at[rows], head_ref, s_ref, nc=nc)
        return carry

    lax.fori_loop(0, nsub, tile, 0)


def _gdn_tile(qkv_ref, og_ref, ba_ref, cw_ref, shift_ref, alog_ref, dtb_ref, ng_ref, o_ref,
              head_ref, s_ref, *, nc):
    hd, dk, dv, kc = GDN_HEADS, GDN_DK, GDN_DV, GDN_CONV
    tc = nc * CHUNK
    pad = GDN_PAD

    head_ref[pad:2 * pad, :] = qkv_ref[0:2 * pad, :].astype(F32)[0:pad]
    lagged = jnp.dot(shift_ref[...], qkv_ref[...], preferred_element_type=F32)

    ri = lax.broadcasted_iota(jnp.int32, (CHUNK, CHUNK), 0)
    ci = lax.broadcasted_iota(jnp.int32, (CHUNK, CHUNK), 1)
    incl = ri >= ci
    strict = ri > ci
    ltri = incl.astype(BF16)
    eye = (ri == ci).astype(F32)
    cw = [cw_ref[i] for i in range(kc)]
    ng = ng_ref[...]

    ba = ba_ref[...].astype(F32)
    beta_all = _sigmoid(ba)
    g_all = -jnp.exp(alog_ref[...]) * _softplus(ba + dtb_ref[...])

    cr = lambda c: slice(c * CHUNK, (c + 1) * CHUNK)
    gam_c = [_select_rows(ltri, g_all[cr(c), :]) for c in range(nc)]
    gam_tc = [g.T for g in gam_c]
    groups = [list(range(g0, min(g0 + GDN_GROUP_CHUNKS, nc))) for g0 in range(0, nc, GDN_GROUP_CHUNKS)]

    def conv_piece(chunks, col):
        lanes = slice(col * dk, (col + 1) * dk)
        r0, nr = chunks[0] * CHUNK, len(chunks) * CHUNK
        blocks = lambda a: a.reshape(nr // 8, 8, dk)
        acc = cw[kc - 1][:, lanes] * blocks(qkv_ref[r0:r0 + nr, lanes].astype(F32))
        for i in range(kc - 1):
            lag = kc - 1 - i
            acc = acc + cw[i][:, lanes] * blocks(lagged[(lag - 1) * tc + r0:(lag - 1) * tc + r0 + nr, lanes])
        acc = acc.reshape(nr, dk)
        if r0 == 0:
            head = cw[kc - 1][:, lanes] * head_ref[pad:2 * pad, lanes]
            for i in range(kc - 1):
                lag = kc - 1 - i
                head = head + cw[i][:, lanes] * head_ref[pad - lag:2 * pad - lag, lanes]
            acc = jnp.concatenate([head, acc[pad:]], axis=0)
        y = _silu(acc)
        if col < 2 * hd:
            y = y * lax.rsqrt(jnp.sum(y * y, axis=-1, keepdims=True) + EPS)
        if col < hd:
            y = y * (dk ** -0.5)
        return [y[i * CHUNK:(i + 1) * CHUNK] for i in range(len(chunks))]

    def chain_stages(chunks, cols, res):
        pairs = [(ci, c, h) for ci, c in enumerate(chunks) for h in range(hd)]
        n = len(pairs)
        q = [cols[h][ci] for ci, c, h in pairs]
        k = [cols[hd + h][ci] for ci, c, h in pairs]
        v = [cols[2 * hd + h][ci] for ci, c, h in pairs]
        beta = [beta_all[cr(c), h:h + 1] for ci, c, h in pairs]
        gam = [gam_c[c][:, hd + h:hd + h + 1] for ci, c, h in pairs]
        gam_row = [gam_tc[c][hd + h:hd + h + 1, :] for ci, c, h in pairs]
        dmask = [jnp.where(incl, jnp.exp(gam[i] - gam_row[i]), 0.0) for i in range(n)]
        k_beta = [k[i] * beta[i] for i in range(n)]
        egam = [jnp.exp(gam[i]) for i in range(n)]
        kk = [_mm_nt(jnp.concatenate([k_beta[i], q[i]], axis=0), k[i]) for i in range(n)]
        low = [jnp.where(strict, kk[i][:CHUNK] * dmask[i], 0.0) for i in range(n)]
        res["attn"] = [kk[i][CHUNK:] * dmask[i] for i in range(n)]
        rhs = [jnp.concatenate([v[i] * beta[i], k_beta[i] * egam[i]], axis=1).astype(BF16) for i in range(n)]
        yield
        lowb = [low[i].astype(BF16) for i in range(n)]
        pw = [_mm(lowb[i], lowb[i]).astype(BF16) for i in range(n)]
        tinv = [eye - low[i] for i in range(n)]
        for s in range(5):
            yield
            if s < 4:
                stk = [_mm(jnp.concatenate([tinv[i].astype(BF16), pw[i]], axis=0), pw[i]) for i in range(n)]
                tinv = [tinv[i] + stk[i][:CHUNK] for i in range(n)]
                pw = [stk[i][CHUNK:].astype(BF16) for i in range(n)]
            else:
                tinv = [tinv[i] + _mm(tinv[i], pw[i]) for i in range(n)]
        yield
        sol = [_mm(tinv[i], rhs[i]).astype(BF16) for i in range(n)]
        q_dec = [q[i] * egam[i] for i in range(n)]
        gam_last = [gam[i][CHUNK - 1:CHUNK, :] for i in range(n)]
        k_dec = [k[i] * jnp.exp(gam_last[i] - gam[i]) for i in range(n)]
        yield
        ks = [_mm_tn(k_dec[i], sol[i]) for i in range(n)]
        aw = [_mm(res["attn"][i], sol[i]) for i in range(n)]
        res["lhs"] = [jnp.concatenate([ks[i][:, dv:], q_dec[i] - aw[i][:, dv:]], axis=0).astype(BF16)
                      for i in range(n)]
        res["s_add"] = [ks[i][:, :dv] for i in range(n)]
        res["o_add"] = [aw[i][:, :dv] for i in range(n)]
        res["decay"] = [jnp.exp(gam_last[i]) for i in range(n)]

    ncols = 3 * hd
    cols = [conv_piece(groups[0], col) for col in range(ncols)]
    st = [s_ref[h] for h in range(hd)]
    for gi, chunks in enumerate(groups):
        res = {}
        nxt = groups[gi + 1] if gi + 1 < len(groups) else None
        nxt_cols = []
        for _ in chain_stages(chunks, cols, res):
            if nxt is not None and len(nxt_cols) < ncols:
                nxt_cols.append(conv_piece(nxt, len(nxt_cols)))
        while nxt is not None and len(nxt_cols) < ncols:
            nxt_cols.append(conv_piece(nxt, len(nxt_cols)))
        cols = nxt_cols
        lhs, s_add, o_add, decay = res["lhs"], res["s_add"], res["o_add"], res["decay"]
        for ci, c in enumerate(chunks):
            ids = [ci * hd + h for h in range(hd)]
            prod = [_mm(lhs[i], st[h]) for h, i in enumerate(ids)]
            o = [prod[h][dk:] + o_add[i] for h, i in enumerate(ids)]
            st = [decay[i] * st[h] - prod[h][:dk] + s_add[i] for h, i in enumerate(ids)]
            for h in range(hd):
                ms = jnp.mean(o[h] * o[h], axis=-1, keepdims=True)
                cl = slice(h * dv, (h + 1) * dv)
                o_ref[cr(c), cl] = (o[h] * lax.rsqrt(ms + EPS) * ng * _silu(og_ref[cr(c), cl].astype(F32))).astype(o_ref.dtype)
    for h in range(hd):
        s_ref[h] = st[h]
    head_ref[0:pad, :] = qkv_ref[tc - 2 * pad:tc, :].astype(F32)[pad:]


def gdn_mix(zm, conv_w_pad, shift_m, alog_row, dtb_row, ng, bsz, seq, tc, nsub):
    t = zm.shape[0]
    blk_rows = tc * nsub
    per_b = seq // blk_rows
    hd, dk, dv = GDN_HEADS, GDN_DK, GDN_DV
    qkv_w = 3 * hd * dk

    def zspec(width, off):
        blk = off // width
        return pl.BlockSpec((blk_rows, width), lambda b, i: (b * per_b + i, blk))

    def full(a):
        return pl.BlockSpec(a.shape, lambda b, i: (0,) * a.ndim)

    return pl.pallas_call(
        functools.partial(_gdn_kernel, nc=tc // CHUNK, nsub=nsub),
        grid=(bsz, per_b),
        in_specs=[zspec(qkv_w, ZM_DQKV), zspec(hd * dv, ZM_DOG), zspec(LANE, ZM_DBA),
                  full(conv_w_pad), full(shift_m), full(alog_row), full(dtb_row), full(ng)],
        out_specs=pl.BlockSpec((blk_rows, hd * dv), lambda b, i: (b * per_b + i, 0)),
        out_shape=jax.ShapeDtypeStruct((t, hd * dv), F32),
        scratch_shapes=[pltpu.VMEM((2 * GDN_PAD, qkv_w), F32), pltpu.VMEM((hd, dk, dv), F32)],
        compiler_params=_params("parallel", "arbitrary"),
        name="gdn_mix",
    )(zm, zm, zm, conv_w_pad, shift_m, alog_row, dtb_row, ng)


def _merge_kernel(h3_ref, ygla_ref, ygdn_ref, yst_ref, wglu_ref, wzg_ref, wbg_ref, wbs_ref, wbd_ref, o_ref,
                  s5t_ref, *, jt, bw, tn):
    n_rows = jt * LANE
    d = h3_ref.shape[-1]
    h = h3_ref[...].reshape(n_rows, d).astype(BF16)
    a_gla = ygla_ref[...].reshape(n_rows, bw).astype(BF16)
    a_gdn = ygdn_ref[...].reshape(n_rows, bw).astype(BF16)
    for jj in range(jt):
        ys = _gelu_tanh(yst_ref[jj].astype(F32))
        glu = _mm_tn(ys, wglu_ref[...])
        s5t_ref[jj * LANE:(jj + 1) * LANE, :] = glu[:, :bw] * _sigmoid(glu[:, bw:])
    a_s5 = pltpu.einshape("(jr)d->rjd", s5t_ref[...], j=jt).reshape(n_rows, bw).astype(BF16)

    def branch(b, cols, a, wb_ref):
        gate = _sigmoid(jnp.dot(h, wzg_ref[:, b * d + cols.start:b * d + cols.stop], preferred_element_type=F32))
        return gate * jnp.dot(a, wb_ref[:, cols], preferred_element_type=F32)

    for c0 in range(0, d, tn):
        cols = slice(c0, c0 + tn)
        m = branch(0, cols, a_gla, wbg_ref) + branch(1, cols, a_s5, wbs_ref) + branch(2, cols, a_gdn, wbd_ref)
        o_ref[:, :, cols] = m.reshape(LANE, jt, tn)


def merge_gate(h3, ygla3, ygdn3, yst, wglu, wzg, wbg, wbs, wbd, layer, jt, tn):
    r, _, d = h3.shape
    bw = ygla3.shape[2]

    def view(width):
        return pl.BlockSpec((LANE, jt, width), lambda i, j: (i, j, 0))

    def layer_w(a):
        return pl.BlockSpec((None,) + a.shape[1:], lambda i, j: (layer, 0, 0))

    return pl.pallas_call(
        functools.partial(_merge_kernel, jt=jt, bw=bw, tn=tn),
        grid=(r // LANE, CHUNK // jt),
        in_specs=[view(d), view(bw), view(bw),
                  pl.BlockSpec((jt, yst.shape[1], LANE), lambda i, j: (j, 0, i)),
                  layer_w(wglu), layer_w(wzg), layer_w(wbg), layer_w(wbs), layer_w(wbd)],
        out_specs=view(d),
        out_shape=jax.ShapeDtypeStruct((r, CHUNK, d), F32),
        scratch_shapes=[pltpu.VMEM((jt * LANE, bw), F32)],
        compiler_params=_params("parallel", "parallel"),
        name="merge_gate",
    )(h3, ygla3, ygdn3, yst, wglu, wzg, wbg, wbs, wbd)


def _ffn_kernel(x_ref, m_ref, wout_ref, gt1_ref, g_ref, sc_ref, sh_ref, gt2_ref, win_ref, wo_ref,
                fg_ref, o_ref, *, final_norm, tf):
    dff = wo_ref.shape[0]
    x1 = x_ref[...] + gt1_ref[...] * _mm(m_ref[...], wout_ref[...])
    h = _norm_mod(x1, g_ref[...], sc_ref[...], sh_ref[...]).astype(BF16)
    acc = None
    for f0 in range(0, dff, tf):
        a = jnp.dot(h, win_ref[:, f0:f0 + tf], preferred_element_type=F32)
        b = jnp.dot(h, win_ref[:, dff + f0:dff + f0 + tf], preferred_element_type=F32)
        part = _mm(_silu(a) * b, wo_ref[f0:f0 + tf, :])
        acc = part if acc is None else acc + part
    y = x1 + gt2_ref[...] * acc
    if final_norm:
        ms = jnp.mean(y * y, axis=-1, keepdims=True)
        y = y * lax.rsqrt(ms + EPS) * fg_ref[...]
    o_ref[...] = y


def ffn_block(x2, merged, w_o, gt1, g, sc, sh, gt2, w_in, w_out, final_g, layer, seq, tm, tf, final_norm):
    t, d = x2.shape
    dff = w_out.shape[1]
    per_b = seq // tm

    def bspec():
        return pl.BlockSpec((None, 1, d), lambda i: (i // per_b, 0, 0))

    def const(a):
        return pl.BlockSpec(a.shape, lambda i: (0,) * a.ndim)

    def layer_w(a):
        return pl.BlockSpec((None,) + a.shape[1:], lambda i: (layer, 0, 0))

    row = pl.BlockSpec((tm, d), lambda i: (i, 0))
    return pl.pallas_call(
        functools.partial(_ffn_kernel, final_norm=final_norm, tf=tf),
        grid=(t // tm,),
        in_specs=[row, row, layer_w(w_o), bspec(), const(g), bspec(), bspec(), bspec(),
                  layer_w(w_in), layer_w(w_out), const(final_g)],
        out_specs=row,
        out_shape=jax.ShapeDtypeStruct((t, d), F32),
        compiler_params=_params("parallel"),
        name="ffn_block",
    )(x2, merged, w_o, gt1, g, sc, sh, gt2, w_in, w_out, final_g)


def _wprep_kernel(w_ref, zm_ref, zg_ref, s5t_ref, *, segs, pads, zg_off, s5_off):
    for dst, width in pads:
        zm_ref[0, :, dst:dst + width] = jnp.zeros((zm_ref.shape[1], width), BF16)
    for src, width, dst in segs:
        zm_ref[0, :, dst:dst + width] = w_ref[0, :, src:src + width].astype(BF16)
    zg_ref[0] = w_ref[0, :, zg_off:zg_off + zg_ref.shape[2]].astype(BF16)
    s5t_ref[0] = w_ref[0, :, s5_off:s5_off + s5t_ref.shape[1]].astype(F32).T.astype(BF16)


def prep_in_weights(w_in, segs, pads, zg_off, zg_w, s5_off, s5_w, tr):
    depth, d, d_in = w_in.shape
    return pl.pallas_call(
        functools.partial(_wprep_kernel, segs=segs, pads=pads, zg_off=zg_off, s5_off=s5_off),
        grid=(depth, d // tr),
        in_specs=[pl.BlockSpec((1, tr, d_in), lambda i, j: (i, j, 0))],
        out_specs=[pl.BlockSpec((1, tr, ZM_WIDTH), lambda i, j: (i, j, 0)),
                   pl.BlockSpec((1, tr, zg_w), lambda i, j: (i, j, 0)),
                   pl.BlockSpec((1, s5_w, tr), lambda i, j: (i, 0, j))],
        out_shape=[jax.ShapeDtypeStruct((depth, d, ZM_WIDTH), BF16),
                   jax.ShapeDtypeStruct((depth, d, zg_w), BF16),
                   jax.ShapeDtypeStruct((depth, s5_w, d), BF16)],
        compiler_params=_params("parallel", "parallel"),
        name="prep_in_weights",
    )(w_in)


def _forward(x, c, w_ada, b_ada, norm1_g, w_in, gla_w_lr, gla_b_lr, gla_norm_g,
             s5_lambda_re, s5_lambda_im, s5_log_dt, s5_b_re, s5_b_im, s5_c_re, s5_c_im,
             s5_d, s5_w_glu, gdn_conv_w, gdn_a_log, gdn_dt_bias, gdn_norm_g,
             w_branch_gla, w_branch_s5, w_branch_gdn, w_out, norm2_g, w_ffn_in, w_ffn_out,
             final_g):
    bsz, seq, d = x.shape
    depth = w_ada.shape[0]
    t = bsz * seq
    nchunk = seq // CHUNK
    r = bsz * nchunk
    gqk, gw = GLA_HEADS * GLA_DK, GLA_HEADS * GLA_DV
    s5w = S5_GROUPS * S5_GROUP_CH
    dqkv, dw = 3 * GDN_HEADS * GDN_DK, GDN_HEADS * GDN_DV

    mod = ada_modulation(c, w_ada, b_ada).reshape(depth, bsz, 6, 1, d)
    s5m, s5et, s5ft, s5ac = s5_params(s5_lambda_re, s5_lambda_im, s5_log_dt, s5_b_re, s5_b_im,
                                      s5_c_re, s5_c_im, s5_d)

    o_gq, o_gk, o_gv = 0, gqk, 2 * gqk
    o_glr = o_gv + gw
    o_gog = o_glr + GLA_LOWRANK
    o_s5 = o_gog + gw
    o_dqkv = o_s5 + s5w
    o_dbeta = o_dqkv + dqkv
    o_da = o_dbeta + GDN_HEADS
    o_dog = o_da + GDN_HEADS
    o_zg = o_dog + dw

    segs = ((o_gq, gqk, ZM_GQ), (o_gk, gqk, ZM_GK), (o_gv, gw, ZM_GV), (o_gog, gw, ZM_GOG),
            (o_dqkv, dqkv, ZM_DQKV), (o_dog, dw, ZM_DOG), (o_glr, GLA_LOWRANK, ZM_GLR),
            (o_dbeta, 2 * GDN_HEADS, ZM_DBA))
    pads = ((ZM_GLR, LANE), (ZM_DBA, LANE))
    w_in_b = jnp.pad(w_in.astype(BF16), ((0, 0), (0, 0), (0, -w_in.shape[2] % LANE)))
    w_zm_all, w_zg_all, w_s5t_all = prep_in_weights(w_in_b, segs, pads, o_zg, N_BRANCH * d, o_s5, s5w, tr=WPREP_ROWS)
    s5_w_glu, w_branch_gla, w_branch_s5, w_branch_gdn, w_out, w_ffn_in, w_ffn_out = [
        a.astype(BF16) for a in (s5_w_glu, w_branch_gla, w_branch_s5, w_branch_gdn, w_out, w_ffn_in, w_ffn_out)]

    lag_row = jnp.arange((GDN_CONV - 1) * GDN_TC)
    gdn_shift = (jnp.arange(GDN_TC)[None, :]
                 == (lag_row % GDN_TC - lag_row // GDN_TC - 1)[:, None]).astype(BF16)

    x2 = x.reshape(t, d)
    for i in range(depth):

        sh1, sc1, gt1, sh2, sc2, gt2 = [mod[i, :, k] for k in range(6)]
        g1 = norm1_g[i].reshape(1, d)

        zm, hf = norm_mod_matmul(x2, g1, sc1, sh1, w_zm_all, i, seq, tm=INPROJ_TM, tn=ZM_WIDTH)
        h3 = hf.reshape(r, CHUNK, d)
        ut = s5_inproj(h3, w_s5t_all, i, jt=TOK_TILE, rt=r)

        wlr_pad = jnp.pad(gla_w_lr[i], ((0, LANE - GLA_LOWRANK), (0, 0)))
        y_gla = gla_mix(zm, wlr_pad, gla_b_lr[i].reshape(1, gqk), gla_norm_g[i].reshape(1, GLA_DV),
                        bsz, seq, tc=GLA_TC, nsub=GLA_NSUB)
        yst = s5_mix(ut, s5m, s5et, s5ft, s5ac, i, nchunk, gsub=S5_GSUB)
        conv_pad = jnp.broadcast_to(gdn_conv_w[i][:, None, :], (GDN_CONV, 8, dqkv))
        alog_row = jnp.pad(gdn_a_log[i], (GDN_HEADS, LANE - 2 * GDN_HEADS)).reshape(1, LANE)
        dtb_row = jnp.pad(gdn_dt_bias[i], (GDN_HEADS, LANE - 2 * GDN_HEADS)).reshape(1, LANE)
        y_gdn = gdn_mix(zm, conv_pad, gdn_shift, alog_row, dtb_row, gdn_norm_g[i].reshape(1, GDN_DV),
                        bsz, seq, tc=GDN_TC, nsub=GDN_NSUB)

        merged = merge_gate(h3, y_gla.reshape(r, CHUNK, gw), y_gdn.reshape(r, CHUNK, dw),
                            yst, s5_w_glu, w_zg_all, w_branch_gla, w_branch_s5, w_branch_gdn, i,
                            jt=TOK_TILE, tn=MERGE_TN)
        x2 = ffn_block(x2, merged.reshape(t, d), w_out, gt1,
                       norm2_g[i].reshape(1, d), sc2, sh2, gt2,
                       w_ffn_in, w_ffn_out,
                       final_g.reshape(1, d), i, seq, tm=FFN_TM, tf=FFN_TF, final_norm=(i == depth - 1))
    return x2.reshape(bsz, seq, d)


def kernel(x, c, w_ada, b_ada, norm1_g, w_in, gla_w_lr, gla_b_lr, gla_norm_g, s5_lambda_re, s5_lambda_im, s5_log_dt, s5_b_re, s5_b_im, s5_c_re, s5_c_im, s5_d, s5_w_glu, gdn_conv_w, gdn_a_log, gdn_dt_bias, gdn_norm_g, w_branch_gla, w_branch_s5, w_branch_gdn, w_out, norm2_g, w_ffn_in, w_ffn_out, final_g):
    return _forward(x, c, w_ada, b_ada, norm1_g, w_in, gla_w_lr, gla_b_lr, gla_norm_g,
                    s5_lambda_re, s5_lambda_im, s5_log_dt, s5_b_re, s5_b_im, s5_c_re, s5_c_im,
                    s5_d, s5_w_glu, gdn_conv_w, gdn_a_log, gdn_dt_bias, gdn_norm_g,
                    w_branch_gla, w_branch_s5, w_branch_gdn, w_out, norm2_g, w_ffn_in, w_ffn_out,
                    final_g)
```

```python
import functools
import math

import jax
import jax.numpy as jnp
from jax import lax
from jax.experimental import pallas as pl
from jax.experimental.pallas import tpu as pltpu

F32 = jnp.float32
BF16 = jnp.bfloat16
HI = lax.Precision.HIGHEST

EPS = 1e-6
CHUNK = 64
LANE = 128
VMEM_LIMIT = 56 * 1024 * 1024

GLA_HEADS, GLA_DK, GLA_DV, GLA_LOWRANK = 4, 64, 128, 16
GLA_GATE_NORM = 16.0
S5_GROUPS, S5_GROUP_CH, S5_STATE = 32, 16, 64
GDN_HEADS, GDN_DK, GDN_DV, GDN_CONV = 4, 128, 128, 4
N_BRANCH = 3
GDN_GROUP_CHUNKS = 4

INPROJ_TM = 512
TOK_TILE = 8
GLA_TC, GLA_NSUB = 512, 2
GDN_TC, GDN_NSUB = 256, 4
S5_GSUB = 4
MERGE_TN = 256
FFN_TM, FFN_TF = 512, 256
WPREP_ROWS = 256

ZM_GQ, ZM_GK, ZM_GV, ZM_GOG = 0, 256, 512, 1024
ZM_DQKV, ZM_DOG, ZM_GLR, ZM_DBA = 1536, 3072, 3584, 3712
ZM_WIDTH = 3840


def _mm(a, b):
    return jnp.dot(a.astype(BF16), b.astype(BF16), preferred_element_type=F32)


def _mm_nt(a, b):
    return lax.dot_general(a.astype(BF16), b.astype(BF16), (((1,), (1,)), ((), ())),
                           preferred_element_type=F32)


def _mm_tn(a, b):
    return lax.dot_general(a.astype(BF16), b.astype(BF16), (((0,), (0,)), ((), ())),
                           preferred_element_type=F32)


def _mm_hi(a, b):
    return jnp.dot(a, b, precision=HI, preferred_element_type=F32)


def _split3(x):
    hi = x.astype(BF16)
    r1 = x - hi.astype(F32)
    mid = r1.astype(BF16)
    return hi, mid, (r1 - mid.astype(F32)).astype(BF16)


def _select_rows(m01, x):
    return sum(jnp.dot(m01, t, preferred_element_type=F32) for t in _split3(x))


def _select_cols(x, m01):
    return sum(jnp.dot(t, m01, preferred_element_type=F32) for t in _split3(x))


def _sigmoid(x):
    return 0.5 + 0.5 * jnp.tanh(0.5 * x)


def _silu(x):
    h = 0.5 * x
    return h + h * jnp.tanh(h)


def _softplus(x):
    return jnp.maximum(x, 0.0) + jnp.log(1.0 + jnp.exp(-jnp.abs(x)))


def _log_sigmoid(x):
    return -_softplus(-x)


def _gelu_tanh(x):
    c = math.sqrt(2.0 / math.pi)
    return 0.5 * x * (1.0 + jnp.tanh(c * (x + 0.044715 * (x * x * x))))


def _norm_mod(x, g, sc, sh):
    ms = jnp.mean(x * x, axis=-1, keepdims=True)
    return (x * lax.rsqrt(ms + EPS) * g) * (1.0 + sc) + sh


def _tok_major(ref):
    return pltpu.einshape("rjd->(jr)d", ref[...])


def _params(*sem):
    return pltpu.CompilerParams(dimension_semantics=sem, vmem_limit_bytes=VMEM_LIMIT)


def _ada_kernel(c_ref, w_ref, b_ref, o_ref):
    @pl.when(pl.program_id(1) == 0)
    def _():
        o_ref[0] = jnp.broadcast_to(b_ref[0], o_ref.shape[1:])

    o_ref[0] += jnp.dot(_silu(c_ref[...]), w_ref[0], preferred_element_type=F32)


def ada_modulation(c, w_ada, b_ada):
    depth, d, d6 = w_ada.shape
    bsz = c.shape[0]
    rows = -(-bsz // 8) * 8
    tk = d // 4
    c_pad = jnp.pad(c, ((0, rows - bsz), (0, 0)))
    out = pl.pallas_call(
        _ada_kernel,
        grid=(depth, d // tk),
        in_specs=[pl.BlockSpec((rows, tk), lambda i, k: (0, k)),
                  pl.BlockSpec((1, tk, d6), lambda i, k: (i, k, 0)),
                  pl.BlockSpec((1, 1, d6), lambda i, k: (i, 0, 0))],
        out_specs=pl.BlockSpec((1, rows, d6), lambda i, k: (i, 0, 0)),
        out_shape=jax.ShapeDtypeStruct((depth, rows, d6), F32),
        compiler_params=_params("parallel", "arbitrary"),
        name="ada_modulation",
    )(c_pad, w_ada, b_ada.reshape(depth, 1, d6))
    return out[:, :bsz]


def _inproj_kernel(x_ref, g_ref, sc_ref, sh_ref, w_ref, o_ref, hf_ref, h_ref):
    @pl.when(pl.program_id(1) == 0)
    def _():
        h = _norm_mod(x_ref[...], g_ref[...], sc_ref[...], sh_ref[...])
        hf_ref[...] = h
        h_ref[...] = h.astype(BF16)

    o_ref[...] = jnp.dot(h_ref[...], w_ref[...], preferred_element_type=F32).astype(o_ref.dtype)


def norm_mod_matmul(x2, g, sc, sh, w, layer, seq, tm, tn):
    t, d = x2.shape
    c = w.shape[2]
    per_b = seq // tm
    return pl.pallas_call(
        _inproj_kernel,
        grid=(t // tm, c // tn),
        in_specs=[pl.BlockSpec((tm, d), lambda i, j: (i, 0)),
                  pl.BlockSpec((1, d), lambda i, j: (0, 0)),
                  pl.BlockSpec((None, 1, d), lambda i, j: (i // per_b, 0, 0)),
                  pl.BlockSpec((None, 1, d), lambda i, j: (i // per_b, 0, 0)),
                  pl.BlockSpec((None, d, tn), lambda i, j: (layer, 0, j))],
        out_specs=[pl.BlockSpec((tm, tn), lambda i, j: (i, j)), pl.BlockSpec((tm, d), lambda i, j: (i, 0))],
        out_shape=[jax.ShapeDtypeStruct((t, c), BF16), jax.ShapeDtypeStruct((t, d), F32)],
        scratch_shapes=[pltpu.VMEM((tm, d), BF16)],
        compiler_params=_params("parallel", "arbitrary"),
        name="norm_mod_matmul",
    )(x2, g, sc, sh, w)


def _s5_inproj_kernel(h_ref, wt_ref, o_ref, *, jt):
    rt = h_ref.shape[0]
    h = _tok_major(h_ref).astype(BF16)
    ut = lax.dot_general(wt_ref[...], h, (((1,), (1,)), ((), ())), preferred_element_type=F32)
    for jj in range(jt):
        o_ref[jj] = ut[:, jj * rt:(jj + 1) * rt].astype(BF16)


def s5_inproj(h3, wt, layer, jt, rt):
    r, _, d = h3.shape
    s5w = wt.shape[1]
    return pl.pallas_call(
        functools.partial(_s5_inproj_kernel, jt=jt),
        grid=(r // rt, CHUNK // jt),
        in_specs=[pl.BlockSpec((rt, jt, d), lambda i, j: (i, j, 0)),
                  pl.BlockSpec((None, s5w, d), lambda i, j: (layer, 0, 0))],
        out_specs=pl.BlockSpec((jt, s5w, rt), lambda i, j: (j, 0, i)),
        out_shape=jax.ShapeDtypeStruct((CHUNK, s5w, r), BF16),
        compiler_params=_params("parallel", "parallel"),
        name="s5_inproj",
    )(h3, wt)


S5_CW = CHUNK * S5_GROUP_CH


def _cpow(lr_dt, li_dt, e):
    mag = jnp.exp(lr_dt * e)
    ang = li_dt * e
    return mag * jnp.cos(ang), mag * jnp.sin(ang)


def _s5_param_kernel(ldt_ref, lrc_ref, lic_ref, lrr_ref, lir_ref, bre_ref, bim_ref,
                     c1_ref, c2_ref, dcol_ref, m_ref, et_ref, ft_ref, ac_ref):
    p, cw, h = S5_STATE, S5_CW, S5_GROUP_CH
    dt = jnp.exp(ldt_ref[0])
    lrc, lic = lrc_ref[0], lic_ref[0]
    ab_re, ab_im = _cpow(lrc * dt, lic * dt, 1.0)
    den = lrc * lrc + lic * lic
    nr, ni = ab_re - 1.0, ab_im
    w_re = (nr * lrc + ni * lic) / den
    w_im = (ni * lrc - nr * lic) / den
    trow = lax.broadcasted_iota(jnp.int32, (h, cw), 0)
    tcol = lax.broadcasted_iota(jnp.int32, (h, cw), 1)
    tile_l = (trow == tcol % h).astype(BF16)
    bre, bim = _select_cols(bre_ref[0], tile_l), _select_cols(bim_ref[0], tile_l)
    bb_re = w_re * bre - w_im * bim
    bb_im = w_re * bim + w_im * bre
    lane = lax.broadcasted_iota(jnp.int32, (1, LANE), 1)
    e_m = jnp.where(lane < CHUNK, CHUNK - 1 - lane, 0).astype(F32)
    pd_re, pd_im = _cpow(lrc * dt, lic * dt, e_m)
    xrow = lax.broadcasted_iota(jnp.int32, (LANE, cw), 0)
    xcol = lax.broadcasted_iota(jnp.int32, (LANE, cw), 1)
    expand_l = (xrow == xcol // h).astype(BF16)
    p_re, p_im = _select_cols(pd_re, expand_l), _select_cols(pd_im, expand_l)
    e_re = p_re * bb_re - p_im * bb_im
    e_im = p_re * bb_im + p_im * bb_re
    et = jnp.concatenate([e_re, e_im], axis=0)
    et_ref[0] = et.astype(BF16)
    a_re, a_im = _cpow(lrc * dt, lic * dt, float(CHUNK))
    ac_ref[0] = jnp.concatenate([a_re, a_im], axis=0)
    sgn = jnp.where(lax.broadcasted_iota(jnp.int32, (1, 2 * p), 1) < p, 1.0, -1.0)
    krev = _mm_hi(c1_ref[0] * sgn, et)
    row = lax.broadcasted_iota(jnp.int32, (h, cw), 0)
    col = lax.broadcasted_iota(jnp.int32, (h, cw), 1)
    krev = krev + jnp.where(col == (cw - h) + row, dcol_ref[0], 0.0)
    rrev = jnp.concatenate([krev, jnp.zeros_like(krev)], axis=1)
    per_tile = LANE // h
    rolled = [rrev if r == 0 else pltpu.roll(rrev, 2 * cw - r * h, axis=1) for r in range(per_tile)]
    for i in range(CHUNK):
        s = (CHUNK - 1 - i) * h
        a, r = s // LANE, (s % LANE) // h
        m_ref[0, i * h:(i + 1) * h, :] = rolled[r][:, a * LANE:a * LANE + cw].astype(BF16)
    f_i = (lax.broadcasted_iota(jnp.int32, (CHUNK, 1), 0) + 1).astype(F32)
    qd_re, qd_im = _cpow(lrr_ref[0] * dt, lir_ref[0] * dt, f_i)
    ft = (c1_ref[0] * sgn)[None] * qd_re[:, None, :] - c2_ref[0][None] * qd_im[:, None, :]
    ft_ref[0] = ft.reshape(cw, 2 * p).astype(BF16)


def s5_params(lam_re, lam_im, log_dt, b_re, b_im, c_re, c_im, dpar):
    ng = lam_re.shape[0] * lam_re.shape[1]
    p, h, cw = S5_STATE, S5_GROUP_CH, S5_CW
    lam_re = lam_re.reshape(ng, p)
    lam_im = lam_im.reshape(ng, p)
    c_re = c_re.reshape(ng, h, p)
    c_im = c_im.reshape(ng, h, p)
    c1 = jnp.concatenate([c_re, c_im], axis=-1)
    c2 = jnp.concatenate([c_im, c_re], axis=-1)
    args = (log_dt.reshape(ng, 1, 1),
            lam_re.reshape(ng, p, 1), lam_im.reshape(ng, p, 1),
            jnp.tile(lam_re.reshape(ng, 1, p), (1, 1, 2)), jnp.tile(lam_im.reshape(ng, 1, p), (1, 1, 2)),
            b_re.reshape(ng, p, h), b_im.reshape(ng, p, h), c1, c2,
            dpar.reshape(ng, h, 1))

    def spec(a):
        return pl.BlockSpec((1,) + a.shape[1:], lambda i: (i, 0, 0))

    return pl.pallas_call(
        _s5_param_kernel,
        grid=(ng,),
        in_specs=[spec(a) for a in args],
        out_specs=[pl.BlockSpec((1, cw, cw), lambda i: (i, 0, 0)),
                   pl.BlockSpec((1, 2 * p, cw), lambda i: (i, 0, 0)),
                   pl.BlockSpec((1, cw, 2 * p), lambda i: (i, 0, 0)),
                   pl.BlockSpec((1, 2 * p, 1), lambda i: (i, 0, 0))],
        out_shape=[jax.ShapeDtypeStruct((ng, cw, cw), BF16),
                   jax.ShapeDtypeStruct((ng, 2 * p, cw), BF16),
                   jax.ShapeDtypeStruct((ng, cw, 2 * p), BF16),
                   jax.ShapeDtypeStruct((ng, 2 * p, 1), F32)],
        compiler_params=_params("parallel"),
        name="s5_params",
    )(*args)


S5_MBLOCKS = 4


def _s5_mix_kernel(u_ref, *refs, nchunk, gsub):
    m_refs, (et_ref, ft_ref, ac_ref, y_ref) = refs[:S5_MBLOCKS], refs[S5_MBLOCKS:]
    p, h = S5_STATE, S5_GROUP_CH
    r = u_ref.shape[-1]
    ks = range(gsub)
    u = [u_ref[:, k * h:(k + 1) * h, :].reshape(S5_CW, r) for k in ks]
    s = [jnp.dot(et_ref[k], u[k], preferred_element_type=F32) for k in ks]
    s_re, s_im = [s[k][:p] for k in ks], [s[k][p:] for k in ks]
    a_re, a_im = [ac_ref[k][:p] for k in ks], [ac_ref[k][p:] for k in ks]
    n_idx = lax.broadcasted_iota(jnp.int32, (1, r), 1) % nchunk
    shift = 1
    while shift < nchunk:
        keep = n_idx >= shift
        t_re = [jnp.where(keep, pltpu.roll(s_re[k], shift, axis=1), 0.0) for k in ks]
        t_im = [jnp.where(keep, pltpu.roll(s_im[k], shift, axis=1), 0.0) for k in ks]
        s_re, s_im = ([s_re[k] + a_re[k] * t_re[k] - a_im[k] * t_im[k] for k in ks],
                      [s_im[k] + a_re[k] * t_im[k] + a_im[k] * t_re[k] for k in ks])
        a_re, a_im = ([a_re[k] * a_re[k] - a_im[k] * a_im[k] for k in ks], [2.0 * a_re[k] * a_im[k] for k in ks])
        shift *= 2
    keep = n_idx >= 1
    for k in ks:
        h_prev = jnp.concatenate([jnp.where(keep, pltpu.roll(s_re[k], 1, axis=1), 0.0),
                                  jnp.where(keep, pltpu.roll(s_im[k], 1, axis=1), 0.0)], axis=0)
        rb_rows = S5_CW // S5_MBLOCKS
        y = jnp.concatenate([jnp.dot(m_refs[rb][k], u[k][:(rb + 1) * rb_rows], preferred_element_type=F32)
                             for rb in range(S5_MBLOCKS)], axis=0)
        y = y + jnp.dot(ft_ref[k], h_prev.astype(BF16), preferred_element_type=F32)
        y_ref[:, k * h:(k + 1) * h, :] = y.reshape(CHUNK, h, r).astype(y_ref.dtype)


def s5_mix(ut, m, et, ft, ac, layer, nchunk, gsub):
    _, s5w, r = ut.shape
    g, h, p, cw = S5_GROUPS, S5_GROUP_CH, S5_STATE, S5_CW
    base = layer * g // gsub
    return pl.pallas_call(
        functools.partial(_s5_mix_kernel, nchunk=nchunk, gsub=gsub),
        grid=(g // gsub,),
        in_specs=[pl.BlockSpec((CHUNK, gsub * h, r), lambda i: (0, i, 0)),
                  *[pl.BlockSpec((gsub, cw // S5_MBLOCKS, (rb + 1) * cw // S5_MBLOCKS),
                                 lambda i, rb=rb: (base + i, rb, 0)) for rb in range(S5_MBLOCKS)],
                  pl.BlockSpec((gsub, 2 * p, cw), lambda i: (base + i, 0, 0)),
                  pl.BlockSpec((gsub, cw, 2 * p), lambda i: (base + i, 0, 0)),
                  pl.BlockSpec((gsub, 2 * p, 1), lambda i: (base + i, 0, 0))],
        out_specs=pl.BlockSpec((CHUNK, gsub * h, r), lambda i: (0, i, 0)),
        out_shape=jax.ShapeDtypeStruct((CHUNK, s5w, r), BF16),
        compiler_params=_params("parallel"),
        name="s5_mix",
    )(ut, *([m] * S5_MBLOCKS), et, ft, ac)


def _gla_kernel(q_ref, k_ref, v_ref, og_ref, lr_ref, wlr_ref, blr_ref, ng_ref, o_ref, st_ref, *, nc, nsub):
    @pl.when(pl.program_id(1) == 0)
    def _():
        st_ref[...] = jnp.zeros_like(st_ref)

    tc = nc * CHUNK

    def tile(s, carry):
        rows = pl.ds(pl.multiple_of(s * tc, tc), tc)
        _gla_tile(q_ref.at[rows], k_ref.at[rows], v_ref.at[rows], og_ref.at[rows], lr_ref.at[rows],
                  wlr_ref, blr_ref, ng_ref, o_ref.at[rows], st_ref, nc=nc)
        return carry

    lax.fori_loop(0, nsub, tile, 0, unroll=2)


def _gla_tile(q_ref, k_ref, v_ref, og_ref, lr_ref, wlr_ref, blr_ref, ng_ref, o_ref, st_ref, *, nc):
    hd, dk, dv = GLA_HEADS, GLA_DK, GLA_DV
    ri = lax.broadcasted_iota(jnp.int32, (CHUNK, CHUNK), 0)
    ci = lax.broadcasted_iota(jnp.int32, (CHUNK, CHUNK), 1)
    incl = ri >= ci
    ltri = incl.astype(BF16)
    lane_k = lax.broadcasted_iota(jnp.int32, (1, hd * dk), 1)
    srow = lax.broadcasted_iota(jnp.int32, (hd * dv, hd * dk), 0)
    scol = lax.broadcasted_iota(jnp.int32, (hd * dv, hd * dk), 1)
    same_head = (srow // dv) == (scol // dk)
    wlr, blr, ng = wlr_ref[...], blr_ref[...], ng_ref[...]

    cs = range(nc)
    cr = lambda c: slice(c * CHUNK, (c + 1) * CHUNK)
    g = [_log_sigmoid(_mm(lr_ref[cr(c), :], wlr) + blr) * (1.0 / GLA_GATE_NORM) for c in cs]
    bc = [_select_rows(ltri, g[c]) for c in cs]
    bl = [bc[c][CHUNK - 1:CHUNK, :] for c in cs]
    q_e = [q_ref[cr(c), :].astype(F32) * (dk ** -0.5) * jnp.exp(bc[c]) for c in cs]
    k_e = [k_ref[cr(c), :].astype(F32) * jnp.exp(-bc[c]) for c in cs]
    k_d = [k_ref[cr(c), :].astype(F32) * jnp.exp(bl[c] - bc[c]) for c in cs]
    kv = [jnp.where(same_head, _mm_tn(v_ref[cr(c), :], k_d[c]), 0.0) for c in cs]
    sts = []
    st = st_ref[...]
    for c in cs:
        sts.append(st)
        st = jnp.exp(bl[c]) * st + kv[c]
    st_ref[...] = st
    o_inter = [_mm_nt(q_e[c], sts[c]) for c in cs]
    q_heads = [jnp.concatenate([jnp.where((lane_k // dk) == h, q_e[c], 0.0) for h in range(hd)], axis=0)
               for c in cs]
    sc_all = [_mm_nt(q_heads[c], k_e[c]) for c in cs]
    sc = [[jnp.where(incl, sc_all[c][h * CHUNK:(h + 1) * CHUNK], 0.0) for h in range(hd)] for c in cs]
    for c in cs:
        for h in range(hd):
            cols = slice(h * dv, (h + 1) * dv)
            oh = _mm(sc[c][h], v_ref[cr(c), cols]) + o_inter[c][:, cols]
            ms = jnp.mean(oh * oh, axis=-1, keepdims=True)
            o_ref[cr(c), cols] = (oh * lax.rsqrt(ms + EPS) * ng * _silu(og_ref[cr(c), cols].astype(F32))).astype(o_ref.dtype)


def gla_mix(zm, wlr_pad, blr, ng, bsz, seq, tc, nsub):
    t = zm.shape[0]
    blk_rows = tc * nsub
    per_b = seq // blk_rows
    hd, dk, dv = GLA_HEADS, GLA_DK, GLA_DV

    def zspec(width, off):
        blk = off // width
        return pl.BlockSpec((blk_rows, width), lambda b, i: (b * per_b + i, blk))

    def full(a):
        return pl.BlockSpec(a.shape, lambda b, i: (0,) * a.ndim)

    return pl.pallas_call(
        functools.partial(_gla_kernel, nc=tc // CHUNK, nsub=nsub),
        grid=(bsz, per_b),
        in_specs=[zspec(hd * dk, ZM_GQ), zspec(hd * dk, ZM_GK), zspec(hd * dv, ZM_GV),
                  zspec(hd * dv, ZM_GOG), zspec(LANE, ZM_GLR), full(wlr_pad), full(blr), full(ng)],
        out_specs=pl.BlockSpec((blk_rows, hd * dv), lambda b, i: (b * per_b + i, 0)),
        out_shape=jax.ShapeDtypeStruct((t, hd * dv), F32),
        scratch_shapes=[pltpu.VMEM((hd * dv, hd * dk), F32)],
        compiler_params=_params("parallel", "arbitrary"),
        name="gla_mix",
    )(zm, zm, zm, zm, zm, wlr_pad, blr, ng)


GDN_PAD = 8


def _gdn_kernel(qkv_ref, og_ref, ba_ref, cw_ref, shift_ref, alog_ref, dtb_ref, ng_ref, o_ref,
                head_ref, s_ref, *, nc, nsub):
    @pl.when(pl.program_id(1) == 0)
    def _():
        s_ref[...] = jnp.zeros_like(s_ref)
        head_ref[0:GDN_PAD, :] = jnp.zeros((GDN_PAD, head_ref.shape[1]), F32)

    tc = nc * CHUNK

    def tile(s, carry):
        rows = pl.ds(pl.multiple_of(s * tc, tc), tc)
        _gdn_tile(qkv_ref.at[rows], og_ref.at[rows], ba_ref.at[rows], cw_ref, shift_ref, alog_ref, dtb_ref,
                  ng_ref, o_ref.at[rows], head_ref, s_ref, nc=nc)
        return carry

    lax.fori_loop(0, nsub, tile, 0, unroll=4)


def _gdn_tile(qkv_ref, og_ref, ba_ref, cw_ref, shift_ref, alog_ref, dtb_ref, ng_ref, o_ref,
              head_ref, s_ref, *, nc):
    hd, dk, dv, kc = GDN_HEADS, GDN_DK, GDN_DV, GDN_CONV
    tc = nc * CHUNK
    pad = GDN_PAD

    head_ref[pad:2 * pad, :] = qkv_ref[0:2 * pad, :].astype(F32)[0:pad]
    lagged = jnp.dot(shift_ref[...], qkv_ref[...], preferred_element_type=F32)

    ri = lax.broadcasted_iota(jnp.int32, (CHUNK, CHUNK), 0)
    ci = lax.broadcasted_iota(jnp.int32, (CHUNK, CHUNK), 1)
    incl = ri >= ci
    strict = ri > ci
    ltri = incl.astype(BF16)
    eye = (ri == ci).astype(F32)
    cw = [cw_ref[i] for i in range(kc)]
    ng = ng_ref[...]

    ba = ba_ref[...].astype(F32)
    beta_all = _sigmoid(ba)
    g_all = -jnp.exp(alog_ref[...]) * _softplus(ba + dtb_ref[...])

    cr = lambda c: slice(c * CHUNK, (c + 1) * CHUNK)
    gam_c = [_select_rows(ltri, g_all[cr(c), :]) for c in range(nc)]
    gam_tc = [g.T for g in gam_c]
    groups = [list(range(g0, min(g0 + GDN_GROUP_CHUNKS, nc))) for g0 in range(0, nc, GDN_GROUP_CHUNKS)]

    def conv_piece(chunks, col):
        lanes = slice(col * dk, (col + 1) * dk)
        r0, nr = chunks[0] * CHUNK, len(chunks) * CHUNK
        blocks = lambda a: a.reshape(nr // 8, 8, dk)
        acc = cw[kc - 1][:, lanes] * blocks(qkv_ref[r0:r0 + nr, lanes].astype(F32))
        for i in range(kc - 1):
            lag = kc - 1 - i
            acc = acc + cw[i][:, lanes] * blocks(lagged[(lag - 1) * tc + r0:(lag - 1) * tc + r0 + nr, lanes])
        acc = acc.reshape(nr, dk)
        if r0 == 0:
            head = cw[kc - 1][:, lanes] * head_ref[pad:2 * pad, lanes]
            for i in range(kc - 1):
                lag = kc - 1 - i
                head = head + cw[i][:, lanes] * head_ref[pad - lag:2 * pad - lag, lanes]
            acc = jnp.concatenate([head, acc[pad:]], axis=0)
        y = _silu(acc)
        if col < 2 * hd:
            y = y * lax.rsqrt(jnp.sum(y * y, axis=-1, keepdims=True) + EPS)
        if col < hd:
            y = y * (dk ** -0.5)
        return [y[i * CHUNK:(i + 1) * CHUNK] for i in range(len(chunks))]

    def chain_stages(chunks, cols, res):
        pairs = [(ci, c, h) for ci, c in enumerate(chunks) for h in range(hd)]
        n = len(pairs)
        q = [cols[h][ci] for ci, c, h in pairs]
        k = [cols[hd + h][ci] for ci, c, h in pairs]
        v = [cols[2 * hd + h][ci] for ci, c, h in pairs]
        beta = [beta_all[cr(c), h:h + 1] for ci, c, h in pairs]
        gam = [gam_c[c][:, hd + h:hd + h + 1] for ci, c, h in pairs]
        gam_row = [gam_tc[c][hd + h:hd + h + 1, :] for ci, c, h in pairs]
        dmask = [jnp.where(incl, jnp.exp(gam[i] - gam_row[i]), 0.0) for i in range(n)]
        k_beta = [k[i] * beta[i] for i in range(n)]
        egam = [jnp.exp(gam[i]) for i in range(n)]
        kk = [_mm_nt(jnp.concatenate([k_beta[i], q[i]], axis=0), k[i]) for i in range(n)]
        low = [jnp.where(strict, kk[i][:CHUNK] * dmask[i], 0.0) for i in range(n)]
        res["attn"] = [kk[i][CHUNK:] * dmask[i] for i in range(n)]
        rhs = [jnp.concatenate([v[i] * beta[i], k_beta[i] * egam[i]], axis=1).astype(BF16) for i in range(n)]
        yield
        lowb = [low[i].astype(BF16) for i in range(n)]
        pw = [_mm(lowb[i], lowb[i]).astype(BF16) for i in range(n)]
        tinv = [eye - low[i] for i in range(n)]
        for s in range(5):
            yield
            if s < 4:
                stk = [_mm(jnp.concatenate([tinv[i].astype(BF16), pw[i]], axis=0), pw[i]) for i in range(n)]
                tinv = [tinv[i] + stk[i][:CHUNK] for i in range(n)]
                pw = [stk[i][CHUNK:].astype(BF16) for i in range(n)]
            else:
                tinv = [tinv[i] + _mm(tinv[i], pw[i]) for i in range(n)]
        yield
        sol = [_mm(tinv[i], rhs[i]).astype(BF16) for i in range(n)]
        q_dec = [q[i] * egam[i] for i in range(n)]
        gam_last = [gam[i][CHUNK - 1:CHUNK, :] for i in range(n)]
        k_dec = [k[i] * jnp.exp(gam_last[i] - gam[i]) for i in range(n)]
        yield
        ks = [_mm_tn(k_dec[i], sol[i]) for i in range(n)]
        aw = [_mm(res["attn"][i], sol[i]) for i in range(n)]
        res["lhs"] = [jnp.concatenate([ks[i][:, dv:], q_dec[i] - aw[i][:, dv:]], axis=0).astype(BF16)
                      for i in range(n)]
        res["s_add"] = [ks[i][:, :dv] for i in range(n)]
        res["o_add"] = [aw[i][:, :dv] for i in range(n)]
        res["decay"] = [jnp.exp(gam_last[i]) for i in range(n)]

    ncols = 3 * hd
    cols = [conv_piece(groups[0], col) for col in range(ncols)]
    st = [s_ref[h] for h in range(hd)]
    for gi, chunks in enumerate(groups):
        res = {}
        nxt = groups[gi + 1] if gi + 1 < len(groups) else None
        nxt_cols = []
        for _ in chain_stages(chunks, cols, res):
            if nxt is not None and len(nxt_cols) < ncols:
                nxt_cols.append(conv_piece(nxt, len(nxt_cols)))
        while nxt is not None and len(nxt_cols) < ncols:
            nxt_cols.append(conv_piece(nxt, len(nxt_cols)))
        cols = nxt_cols
        lhs, s_add, o_add, decay = res["lhs"], res["s_add"], res["o_add"], res["decay"]
        for ci, c in enumerate(chunks):
            ids = [ci * hd + h for h in range(hd)]
            prod = [_mm(lhs[i], st[h]) for h, i in enumerate(ids)]
            o = [prod[h][dk:] + o_add[i] for h, i in enumerate(ids)]
            st = [decay[i] * st[h] - prod[h][:dk] + s_add[i] for h, i in enumerate(ids)]
            for h in range(hd):
                ms = jnp.mean(o[h] * o[h], axis=-1, keepdims=True)
                cl = slice(h * dv, (h + 1) * dv)
                o_ref[cr(c), cl] = (o[h] * lax.rsqrt(ms + EPS) * ng * _silu(og_ref[cr(c), cl].astype(F32))).astype(o_ref.dtype)
    for h in range(hd):
        s_ref[h] = st[h]
    head_ref[0:pad, :] = qkv_ref[tc - 2 * pad:tc, :].astype(F32)[pad:]


def gdn_mix(zm, conv_w_pad, shift_m, alog_row, dtb_row, ng, bsz, seq, tc, nsub):
    t = zm.shape[0]
    blk_rows = tc * nsub
    per_b = seq // blk_rows
    hd, dk, dv = GDN_HEADS, GDN_DK, GDN_DV
    qkv_w = 3 * hd * dk

    def zspec(width, off):
        blk = off // width
        return pl.BlockSpec((blk_rows, width), lambda b, i: (b * per_b + i, blk))

    def full(a):
        return pl.BlockSpec(a.shape, lambda b, i: (0,) * a.ndim)

    return pl.pallas_call(
        functools.partial(_gdn_kernel, nc=tc // CHUNK, nsub=nsub),
        grid=(bsz, per_b),
        in_specs=[zspec(qkv_w, ZM_DQKV), zspec(hd * dv, ZM_DOG), zspec(LANE, ZM_DBA),
                  full(conv_w_pad), full(shift_m), full(alog_row), full(dtb_row), full(ng)],
        out_specs=pl.BlockSpec((blk_rows, hd * dv), lambda b, i: (b * per_b + i, 0)),
        out_shape=jax.ShapeDtypeStruct((t, hd * dv), F32),
        scratch_shapes=[pltpu.VMEM((2 * GDN_PAD, qkv_w), F32), pltpu.VMEM((hd, dk, dv), F32)],
        compiler_params=_params("parallel", "arbitrary"),
        name="gdn_mix",
    )(zm, zm, zm, conv_w_pad, shift_m, alog_row, dtb_row, ng)


def _merge_kernel(h3_ref, ygla_ref, ygdn_ref, yst_ref, wglu_ref, wzg_ref, wbg_ref, wbs_ref, wbd_ref, o_ref,
                  s5t_ref, *, jt, bw, tn):
    n_rows = jt * LANE
    d = h3_ref.shape[-1]
    h = h3_ref[...].reshape(n_rows, d).astype(BF16)
    a_gla = ygla_ref[...].reshape(n_rows, bw).astype(BF16)
    a_gdn = ygdn_ref[...].reshape(n_rows, bw).astype(BF16)
    for jj in range(jt):
        ys = _gelu_tanh(yst_ref[jj].astype(F32))
        glu = _mm_tn(ys, wglu_ref[...])
        s5t_ref[jj * LANE:(jj + 1) * LANE, :] = glu[:, :bw] * _sigmoid(glu[:, bw:])
    a_s5 = pltpu.einshape("(jr)d->rjd", s5t_ref[...], j=jt).reshape(n_rows, bw).astype(BF16)

    def branch(b, cols, a, wb_ref):
        gate = _sigmoid(jnp.dot(h, wzg_ref[:, b * d + cols.start:b * d + cols.stop], preferred_element_type=F32))
        return gate * jnp.dot(a, wb_ref[:, cols], preferred_element_type=F32)

    for c0 in range(0, d, tn):
        cols = slice(c0, c0 + tn)
        m = branch(0, cols, a_gla, wbg_ref) + branch(1, cols, a_s5, wbs_ref) + branch(2, cols, a_gdn, wbd_ref)
        o_ref[:, :, cols] = m.reshape(LANE, jt, tn)


def merge_gate(h3, ygla3, ygdn3, yst, wglu, wzg, wbg, wbs, wbd, layer, jt, tn):
    r, _, d = h3.shape
    bw = ygla3.shape[2]

    def view(width):
        return pl.BlockSpec((LANE, jt, width), lambda i, j: (i, j, 0))

    def layer_w(a):
        return pl.BlockSpec((None,) + a.shape[1:], lambda i, j: (layer, 0, 0))

    return pl.pallas_call(
        functools.partial(_merge_kernel, jt=jt, bw=bw, tn=tn),
        grid=(r // LANE, CHUNK // jt),
        in_specs=[view(d), view(bw), view(bw),
                  pl.BlockSpec((jt, yst.shape[1], LANE), lambda i, j: (j, 0, i)),
                  layer_w(wglu), layer_w(wzg), layer_w(wbg), layer_w(wbs), layer_w(wbd)],
        out_specs=view(d),
        out_shape=jax.ShapeDtypeStruct((r, CHUNK, d), F32),
        scratch_shapes=[pltpu.VMEM((jt * LANE, bw), F32)],
        compiler_params=_params("parallel", "parallel"),
        name="merge_gate",
    )(h3, ygla3, ygdn3, yst, wglu, wzg, wbg, wbs, wbd)


def _ffn_kernel(x_ref, m_ref, wout_ref, gt1_ref, g_ref, sc_ref, sh_ref, gt2_ref, win_ref, wo_ref,
                fg_ref, o_ref, *, final_norm, tf):
    dff = wo_ref.shape[0]
    x1 = x_ref[...] + gt1_ref[...] * _mm(m_ref[...], wout_ref[...])
    h = _norm_mod(x1, g_ref[...], sc_ref[...], sh_ref[...]).astype(BF16)
    acc = None
    for f0 in range(0, dff, tf):
        a = jnp.dot(h, win_ref[:, f0:f0 + tf], preferred_element_type=F32)
        b = jnp.dot(h, win_ref[:, dff + f0:dff + f0 + tf], preferred_element_type=F32)
        part = _mm(_silu(a) * b, wo_ref[f0:f0 + tf, :])
        acc = part if acc is None else acc + part
    y = x1 + gt2_ref[...] * acc
    if final_norm:
        ms = jnp.mean(y * y, axis=-1, keepdims=True)
        y = y * lax.rsqrt(ms + EPS) * fg_ref[...]
    o_ref[...] = y


def ffn_block(x2, merged, w_o, gt1, g, sc, sh, gt2, w_in, w_out, final_g, layer, seq, tm, tf, final_norm):
    t, d = x2.shape
    dff = w_out.shape[1]
    per_b = seq // tm

    def bspec():
        return pl.BlockSpec((None, 1, d), lambda i: (i // per_b, 0, 0))

    def const(a):
        return pl.BlockSpec(a.shape, lambda i: (0,) * a.ndim)

    def layer_w(a):
        return pl.BlockSpec((None,) + a.shape[1:], lambda i: (layer, 0, 0))

    row = pl.BlockSpec((tm, d), lambda i: (i, 0))
    return pl.pallas_call(
        functools.partial(_ffn_kernel, final_norm=final_norm, tf=tf),
        grid=(t // tm,),
        in_specs=[row, row, layer_w(w_o), bspec(), const(g), bspec(), bspec(), bspec(),
                  layer_w(w_in), layer_w(w_out), const(final_g)],
        out_specs=row,
        out_shape=jax.ShapeDtypeStruct((t, d), F32),
        compiler_params=_params("parallel"),
        name="ffn_block",
    )(x2, merged, w_o, gt1, g, sc, sh, gt2, w_in, w_out, final_g)


def _wprep_kernel(w_ref, zm_ref, zg_ref, s5t_ref, *, segs, pads, zg_off, s5_off):
    for dst, width in pads:
        zm_ref[0, :, dst:dst + width] = jnp.zeros((zm_ref.shape[1], width), BF16)
    for src, width, dst in segs:
        zm_ref[0, :, dst:dst + width] = w_ref[0, :, src:src + width].astype(BF16)
    zg_ref[0] = w_ref[0, :, zg_off:zg_off + zg_ref.shape[2]].astype(BF16)
    s5t_ref[0] = w_ref[0, :, s5_off:s5_off + s5t_ref.shape[1]].astype(F32).T.astype(BF16)


def prep_in_weights(w_in, segs, pads, zg_off, zg_w, s5_off, s5_w, tr):
    depth, d, d_in = w_in.shape
    return pl.pallas_call(
        functools.partial(_wprep_kernel, segs=segs, pads=pads, zg_off=zg_off, s5_off=s5_off),
        grid=(depth, d // tr),
        in_specs=[pl.BlockSpec((1, tr, d_in), lambda i, j: (i, j, 0))],
        out_specs=[pl.BlockSpec((1, tr, ZM_WIDTH), lambda i, j: (i, j, 0)),
                   pl.BlockSpec((1, tr, zg_w), lambda i, j: (i, j, 0)),
                   pl.BlockSpec((1, s5_w, tr), lambda i, j: (i, 0, j))],
        out_shape=[jax.ShapeDtypeStruct((depth, d, ZM_WIDTH), BF16),
                   jax.ShapeDtypeStruct((depth, d, zg_w), BF16),
                   jax.ShapeDtypeStruct((depth, s5_w, d), BF16)],
        compiler_params=_params("parallel", "parallel"),
        name="prep_in_weights",
    )(w_in)


def _forward(x, c, w_ada, b_ada, norm1_g, w_in, gla_w_lr, gla_b_lr, gla_norm_g,
             s5_lambda_re, s5_lambda_im, s5_log_dt, s5_b_re, s5_b_im, s5_c_re, s5_c_im,
             s5_d, s5_w_glu, gdn_conv_w, gdn_a_log, gdn_dt_bias, gdn_norm_g,
             w_branch_gla, w_branch_s5, w_branch_gdn, w_out, norm2_g, w_ffn_in, w_ffn_out,
             final_g):
    bsz, seq, d = x.shape
    depth = w_ada.shape[0]
    t = bsz * seq
    nchunk = seq // CHUNK
    r = bsz * nchunk
    gqk, gw = GLA_HEADS * GLA_DK, GLA_HEADS * GLA_DV
    s5w = S5_GROUPS * S5_GROUP_CH
    dqkv, dw = 3 * GDN_HEADS * GDN_DK, GDN_HEADS * GDN_DV

    mod = ada_modulation(c, w_ada, b_ada).reshape(depth, bsz, 6, 1, d)
    s5m, s5et, s5ft, s5ac = s5_params(s5_lambda_re, s5_lambda_im, s5_log_dt, s5_b_re, s5_b_im,
                                      s5_c_re, s5_c_im, s5_d)

    o_gq, o_gk, o_gv = 0, gqk, 2 * gqk
    o_glr = o_gv + gw
    o_gog = o_glr + GLA_LOWRANK
    o_s5 = o_gog + gw
    o_dqkv = o_s5 + s5w
    o_dbeta = o_dqkv + dqkv
    o_da = o_dbeta + GDN_HEADS
    o_dog = o_da + GDN_HEADS
    o_zg = o_dog + dw

    segs = ((o_gq, gqk, ZM_GQ), (o_gk, gqk, ZM_GK), (o_gv, gw, ZM_GV), (o_gog, gw, ZM_GOG),
            (o_dqkv, dqkv, ZM_DQKV), (o_dog, dw, ZM_DOG), (o_glr, GLA_LOWRANK, ZM_GLR),
            (o_dbeta, 2 * GDN_HEADS, ZM_DBA))
    pads = ((ZM_GLR, LANE), (ZM_DBA, LANE))
    w_in_b = jnp.pad(w_in.astype(BF16), ((0, 0), (0, 0), (0, -w_in.shape[2] % LANE)))
    w_zm_all, w_zg_all, w_s5t_all = prep_in_weights(w_in_b, segs, pads, o_zg, N_BRANCH * d, o_s5, s5w, tr=WPREP_ROWS)
    s5_w_glu, w_branch_gla, w_branch_s5, w_branch_gdn, w_out, w_ffn_in, w_ffn_out = [
        a.astype(BF16) for a in (s5_w_glu, w_branch_gla, w_branch_s5, w_branch_gdn, w_out, w_ffn_in, w_ffn_out)]

    lag_row = jnp.arange((GDN_CONV - 1) * GDN_TC)
    gdn_shift = (jnp.arange(GDN_TC)[None, :]
                 == (lag_row % GDN_TC - lag_row // GDN_TC - 1)[:, None]).astype(BF16)

    x2 = x.reshape(t, d)
    for i in range(depth):

        sh1, sc1, gt1, sh2, sc2, gt2 = [mod[i, :, k] for k in range(6)]
        g1 = norm1_g[i].reshape(1, d)

        zm, hf = norm_mod_matmul(x2, g1, sc1, sh1, w_zm_all, i, seq, tm=INPROJ_TM, tn=ZM_WIDTH)
        h3 = hf.reshape(r, CHUNK, d)
        ut = s5_inproj(h3, w_s5t_all, i, jt=TOK_TILE, rt=r)

        wlr_pad = jnp.pad(gla_w_lr[i], ((0, LANE - GLA_LOWRANK), (0, 0)))
        y_gla = gla_mix(zm, wlr_pad, gla_b_lr[i].reshape(1, gqk), gla_norm_g[i].reshape(1, GLA_DV),
                        bsz, seq, tc=GLA_TC, nsub=GLA_NSUB)
        yst = s5_mix(ut, s5m, s5et, s5ft, s5ac, i, nchunk, gsub=S5_GSUB)
        conv_pad = jnp.broadcast_to(gdn_conv_w[i][:, None, :], (GDN_CONV, 8, dqkv))
        alog_row = jnp.pad(gdn_a_log[i], (GDN_HEADS, LANE - 2 * GDN_HEADS)).reshape(1, LANE)
        dtb_row = jnp.pad(gdn_dt_bias[i], (GDN_HEADS, LANE - 2 * GDN_HEADS)).reshape(1, LANE)
        y_gdn = gdn_mix(zm, conv_pad, gdn_shift, alog_row, dtb_row, gdn_norm_g[i].reshape(1, GDN_DV),
                        bsz, seq, tc=GDN_TC, nsub=GDN_NSUB)

        merged = merge_gate(h3, y_gla.reshape(r, CHUNK, gw), y_gdn.reshape(r, CHUNK, dw),
                            yst, s5_w_glu, w_zg_all, w_branch_gla, w_branch_s5, w_branch_gdn, i,
                            jt=TOK_TILE, tn=MERGE_TN)
        x2 = ffn_block(x2, merged.reshape(t, d), w_out, gt1,
                       norm2_g[i].reshape(1, d), sc2, sh2, gt2,
                       w_ffn_in, w_ffn_out,
                       final_g.reshape(1, d), i, seq, tm=FFN_TM, tf=FFN_TF, final_norm=(i == depth - 1))
    return x2.reshape(bsz, seq, d)


def kernel(x, c, w_ada, b_ada, norm1_g, w_in, gla_w_lr, gla_b_lr, gla_norm_g, s5_lambda_re, s5_lambda_im, s5_log_dt, s5_b_re, s5_b_im, s5_c_re, s5_c_im, s5_d, s5_w_glu, gdn_conv_w, gdn_a_log, gdn_dt_bias, gdn_norm_g, w_branch_gla, w_branch_s5, w_branch_gdn, w_out, norm2_g, w_ffn_in, w_ffn_out, final_g):
    return _forward(x, c, w_ada, b_ada, norm1_g, w_in, gla_w_lr, gla_b_lr, gla_norm_g,
                    s5_lambda_re, s5_lambda_im, s5_log_dt, s5_b_re, s5_b_im, s5_c_re, s5_c_im,
                    s5_d, s5_w_glu, gdn_conv_w, gdn_a_log, gdn_dt_bias, gdn_norm_g,
                    w_branch_gla, w_branch_s5, w_branch_gdn, w_out, norm2_g, w_ffn_in, w_ffn_out,
                    final_g)
```

```python
import functools
import math

import jax
import jax.numpy as jnp
from jax import lax
from jax.experimental import pallas as pl
from jax.experimental.pallas import tpu as pltpu

F32 = jnp.float32
BF16 = jnp.bfloat16
HI = lax.Precision.HIGHEST

EPS = 1e-6
CHUNK = 64
LANE = 128
VMEM_LIMIT = 56 * 1024 * 1024

GLA_HEADS, GLA_DK, GLA_DV, GLA_LOWRANK = 4, 64, 128, 16
GLA_GATE_NORM = 16.0
S5_GROUPS, S5_GROUP_CH, S5_STATE = 32, 16, 64
GDN_HEADS, GDN_DK, GDN_DV, GDN_CONV = 4, 128, 128, 4
N_BRANCH = 3

INPROJ_TM = 512
TOK_TILE = 8
GLA_TC, GLA_NSUB = 512, 2
GDN_TC, GDN_NSUB = 256, 4
S5_GSUB = 4
MERGE_TN = 256
FFN_TM, FFN_TF = 512, 256
WPREP_ROWS = 256

ZM_GQ, ZM_GK, ZM_GV, ZM_GOG = 0, 256, 512, 1024
ZM_DQKV, ZM_DOG, ZM_GLR, ZM_DBA = 1536, 3072, 3584, 3712
ZM_WIDTH = 3840


def _mm(a, b):
    return jnp.dot(a.astype(BF16), b.astype(BF16), preferred_element_type=F32)


def _mm_nt(a, b):
    return lax.dot_general(a.astype(BF16), b.astype(BF16), (((1,), (1,)), ((), ())),
                           preferred_element_type=F32)


def _mm_tn(a, b):
    return lax.dot_general(a.astype(BF16), b.astype(BF16), (((0,), (0,)), ((), ())),
                           preferred_element_type=F32)


def _mm_hi(a, b):
    return jnp.dot(a, b, precision=HI, preferred_element_type=F32)


def _split3(x):
    hi = x.astype(BF16)
    r1 = x - hi.astype(F32)
    mid = r1.astype(BF16)
    return hi, mid, (r1 - mid.astype(F32)).astype(BF16)


def _select_rows(m01, x):
    return sum(jnp.dot(m01, t, preferred_element_type=F32) for t in _split3(x))


def _select_cols(x, m01):
    return sum(jnp.dot(t, m01, preferred_element_type=F32) for t in _split3(x))


def _sigmoid(x):
    return 0.5 + 0.5 * jnp.tanh(0.5 * x)


def _silu(x):
    h = 0.5 * x
    return h + h * jnp.tanh(h)


def _softplus(x):
    return jnp.maximum(x, 0.0) + jnp.log(1.0 + jnp.exp(-jnp.abs(x)))


def _log_sigmoid(x):
    return -_softplus(-x)


def _gelu_tanh(x):
    c = math.sqrt(2.0 / math.pi)
    return 0.5 * x * (1.0 + jnp.tanh(c * (x + 0.044715 * (x * x * x))))


def _norm_mod(x, g, sc, sh):
    ms = jnp.mean(x * x, axis=-1, keepdims=True)
    return (x * lax.rsqrt(ms + EPS) * g) * (1.0 + sc) + sh


def _tok_major(ref):
    return pltpu.einshape("rjd->(jr)d", ref[...])


def _params(*sem):
    return pltpu.CompilerParams(dimension_semantics=sem, vmem_limit_bytes=VMEM_LIMIT)


def _ada_kernel(c_ref, w_ref, b_ref, o_ref):
    @pl.when(pl.program_id(1) == 0)
    def _():
        o_ref[0] = jnp.broadcast_to(b_ref[0], o_ref.shape[1:])

    o_ref[0] += jnp.dot(_silu(c_ref[...]), w_ref[0], preferred_element_type=F32)


def ada_modulation(c, w_ada, b_ada):
    depth, d, d6 = w_ada.shape
    bsz = c.shape[0]
    rows = -(-bsz // 8) * 8
    tk = d // 4
    c_pad = jnp.pad(c, ((0, rows - bsz), (0, 0)))
    out = pl.pallas_call(
        _ada_kernel,
        grid=(depth, d // tk),
        in_specs=[pl.BlockSpec((rows, tk), lambda i, k: (0, k)),
                  pl.BlockSpec((1, tk, d6), lambda i, k: (i, k, 0)),
                  pl.BlockSpec((1, 1, d6), lambda i, k: (i, 0, 0))],
        out_specs=pl.BlockSpec((1, rows, d6), lambda i, k: (i, 0, 0)),
        out_shape=jax.ShapeDtypeStruct((depth, rows, d6), F32),
        compiler_params=_params("parallel", "arbitrary"),
        name="ada_modulation",
    )(c_pad, w_ada, b_ada.reshape(depth, 1, d6))
    return out[:, :bsz]


def _inproj_kernel(x_ref, g_ref, sc_ref, sh_ref, w_ref, o_ref, hf_ref, h_ref):
    @pl.when(pl.program_id(1) == 0)
    def _():
        h = _norm_mod(x_ref[...], g_ref[...], sc_ref[...], sh_ref[...])
        hf_ref[...] = h
        h_ref[...] = h.astype(BF16)

    o_ref[...] = jnp.dot(h_ref[...], w_ref[...], preferred_element_type=F32).astype(o_ref.dtype)


def norm_mod_matmul(x2, g, sc, sh, w, layer, seq, tm, tn):
    t, d = x2.shape
    c = w.shape[2]
    per_b = seq // tm
    return pl.pallas_call(
        _inproj_kernel,
        grid=(t // tm, c // tn),
        in_specs=[pl.BlockSpec((tm, d), lambda i, j: (i, 0)),
                  pl.BlockSpec((1, d), lambda i, j: (0, 0)),
                  pl.BlockSpec((None, 1, d), lambda i, j: (i // per_b, 0, 0)),
                  pl.BlockSpec((None, 1, d), lambda i, j: (i // per_b, 0, 0)),
                  pl.BlockSpec((None, d, tn), lambda i, j: (layer, 0, j))],
        out_specs=[pl.BlockSpec((tm, tn), lambda i, j: (i, j)), pl.BlockSpec((tm, d), lambda i, j: (i, 0))],
        out_shape=[jax.ShapeDtypeStruct((t, c), BF16), jax.ShapeDtypeStruct((t, d), F32)],
        scratch_shapes=[pltpu.VMEM((tm, d), BF16)],
        compiler_params=_params("parallel", "arbitrary"),
        name="norm_mod_matmul",
    )(x2, g, sc, sh, w)


def _s5_inproj_kernel(h_ref, wt_ref, o_ref, *, jt):
    rt = h_ref.shape[0]
    h = _tok_major(h_ref).astype(BF16)
    ut = lax.dot_general(wt_ref[...], h, (((1,), (1,)), ((), ())), preferred_element_type=F32)
    for jj in range(jt):
        o_ref[jj] = ut[:, jj * rt:(jj + 1) * rt].astype(BF16)


def s5_inproj(h3, wt, layer, jt, rt):
    r, _, d = h3.shape
    s5w = wt.shape[1]
    return pl.pallas_call(
        functools.partial(_s5_inproj_kernel, jt=jt),
        grid=(r // rt, CHUNK // jt),
        in_specs=[pl.BlockSpec((rt, jt, d), lambda i, j: (i, j, 0)),
                  pl.BlockSpec((None, s5w, d), lambda i, j: (layer, 0, 0))],
        out_specs=pl.BlockSpec((jt, s5w, rt), lambda i, j: (j, 0, i)),
        out_shape=jax.ShapeDtypeStruct((CHUNK, s5w, r), BF16),
        compiler_params=_params("parallel", "parallel"),
        name="s5_inproj",
    )(h3, wt)


S5_CW = CHUNK * S5_GROUP_CH


def _cpow(lr_dt, li_dt, e):
    mag = jnp.exp(lr_dt * e)
    ang = li_dt * e
    return mag * jnp.cos(ang), mag * jnp.sin(ang)


def _s5_param_kernel(ldt_ref, lrc_ref, lic_ref, lrr_ref, lir_ref, bre_ref, bim_ref,
                     c1_ref, c2_ref, dcol_ref, m_ref, et_ref, ft_ref, ac_ref):
    p, cw, h = S5_STATE, S5_CW, S5_GROUP_CH
    dt = jnp.exp(ldt_ref[0])
    lrc, lic = lrc_ref[0], lic_ref[0]
    ab_re, ab_im = _cpow(lrc * dt, lic * dt, 1.0)
    den = lrc * lrc + lic * lic
    nr, ni = ab_re - 1.0, ab_im
    w_re = (nr * lrc + ni * lic) / den
    w_im = (ni * lrc - nr * lic) / den
    trow = lax.broadcasted_iota(jnp.int32, (h, cw), 0)
    tcol = lax.broadcasted_iota(jnp.int32, (h, cw), 1)
    tile_l = (trow == tcol % h).astype(BF16)
    bre, bim = _select_cols(bre_ref[0], tile_l), _select_cols(bim_ref[0], tile_l)
    bb_re = w_re * bre - w_im * bim
    bb_im = w_re * bim + w_im * bre
    lane = lax.broadcasted_iota(jnp.int32, (1, LANE), 1)
    e_m = jnp.where(lane < CHUNK, CHUNK - 1 - lane, 0).astype(F32)
    pd_re, pd_im = _cpow(lrc * dt, lic * dt, e_m)
    xrow = lax.broadcasted_iota(jnp.int32, (LANE, cw), 0)
    xcol = lax.broadcasted_iota(jnp.int32, (LANE, cw), 1)
    expand_l = (xrow == xcol // h).astype(BF16)
    p_re, p_im = _select_cols(pd_re, expand_l), _select_cols(pd_im, expand_l)
    e_re = p_re * bb_re - p_im * bb_im
    e_im = p_re * bb_im + p_im * bb_re
    et = jnp.concatenate([e_re, e_im], axis=0)
    et_ref[0] = et.astype(BF16)
    a_re, a_im = _cpow(lrc * dt, lic * dt, float(CHUNK))
    ac_ref[0] = jnp.concatenate([a_re, a_im], axis=0)
    sgn = jnp.where(lax.broadcasted_iota(jnp.int32, (1, 2 * p), 1) < p, 1.0, -1.0)
    krev = _mm_hi(c1_ref[0] * sgn, et)
    row = lax.broadcasted_iota(jnp.int32, (h, cw), 0)
    col = lax.broadcasted_iota(jnp.int32, (h, cw), 1)
    krev = krev + jnp.where(col == (cw - h) + row, dcol_ref[0], 0.0)
    rrev = jnp.concatenate([krev, jnp.zeros_like(krev)], axis=1)
    per_tile = LANE // h
    rolled = [rrev if r == 0 else pltpu.roll(rrev, 2 * cw - r * h, axis=1) for r in range(per_tile)]
    for i in range(CHUNK):
        s = (CHUNK - 1 - i) * h
        a, r = s // LANE, (s % LANE) // h
        m_ref[0, i * h:(i + 1) * h, :] = rolled[r][:, a * LANE:a * LANE + cw].astype(BF16)
    f_i = (lax.broadcasted_iota(jnp.int32, (CHUNK, 1), 0) + 1).astype(F32)
    qd_re, qd_im = _cpow(lrr_ref[0] * dt, lir_ref[0] * dt, f_i)
    ft = (c1_ref[0] * sgn)[None] * qd_re[:, None, :] - c2_ref[0][None] * qd_im[:, None, :]
    ft_ref[0] = ft.reshape(cw, 2 * p).astype(BF16)


def s5_params(lam_re, lam_im, log_dt, b_re, b_im, c_re, c_im, dpar):
    ng = lam_re.shape[0] * lam_re.shape[1]
    p, h, cw = S5_STATE, S5_GROUP_CH, S5_CW
    lam_re = lam_re.reshape(ng, p)
    lam_im = lam_im.reshape(ng, p)
    c_re = c_re.reshape(ng, h, p)
    c_im = c_im.reshape(ng, h, p)
    c1 = jnp.concatenate([c_re, c_im], axis=-1)
    c2 = jnp.concatenate([c_im, c_re], axis=-1)
    args = (log_dt.reshape(ng, 1, 1),
            lam_re.reshape(ng, p, 1), lam_im.reshape(ng, p, 1),
            jnp.tile(lam_re.reshape(ng, 1, p), (1, 1, 2)), jnp.tile(lam_im.reshape(ng, 1, p), (1, 1, 2)),
            b_re.reshape(ng, p, h), b_im.reshape(ng, p, h), c1, c2,
            dpar.reshape(ng, h, 1))

    def spec(a):
        return pl.BlockSpec((1,) + a.shape[1:], lambda i: (i, 0, 0))

    return pl.pallas_call(
        _s5_param_kernel,
        grid=(ng,),
        in_specs=[spec(a) for a in args],
        out_specs=[pl.BlockSpec((1, cw, cw), lambda i: (i, 0, 0)),
                   pl.BlockSpec((1, 2 * p, cw), lambda i: (i, 0, 0)),
                   pl.BlockSpec((1, cw, 2 * p), lambda i: (i, 0, 0)),
                   pl.BlockSpec((1, 2 * p, 1), lambda i: (i, 0, 0))],
        out_shape=[jax.ShapeDtypeStruct((ng, cw, cw), BF16),
                   jax.ShapeDtypeStruct((ng, 2 * p, cw), BF16),
                   jax.ShapeDtypeStruct((ng, cw, 2 * p), BF16),
                   jax.ShapeDtypeStruct((ng, 2 * p, 1), F32)],
        compiler_params=_params("parallel"),
        name="s5_params",
    )(*args)


S5_MBLOCKS = 4


def _s5_mix_kernel(u_ref, *refs, nchunk, gsub):
    m_refs, (et_ref, ft_ref, ac_ref, y_ref) = refs[:S5_MBLOCKS], refs[S5_MBLOCKS:]
    p, h = S5_STATE, S5_GROUP_CH
    r = u_ref.shape[-1]
    ks = range(gsub)
    u = [u_ref[:, k * h:(k + 1) * h, :].reshape(S5_CW, r) for k in ks]
    s = [jnp.dot(et_ref[k], u[k], preferred_element_type=F32) for k in ks]
    s_re, s_im = [s[k][:p] for k in ks], [s[k][p:] for k in ks]
    a_re, a_im = [ac_ref[k][:p] for k in ks], [ac_ref[k][p:] for k in ks]
    n_idx = lax.broadcasted_iota(jnp.int32, (1, r), 1) % nchunk
    shift = 1
    while shift < nchunk:
        keep = n_idx >= shift
        t_re = [jnp.where(keep, pltpu.roll(s_re[k], shift, axis=1), 0.0) for k in ks]
        t_im = [jnp.where(keep, pltpu.roll(s_im[k], shift, axis=1), 0.0) for k in ks]
        s_re, s_im = ([s_re[k] + a_re[k] * t_re[k] - a_im[k] * t_im[k] for k in ks],
                      [s_im[k] + a_re[k] * t_im[k] + a_im[k] * t_re[k] for k in ks])
        a_re, a_im = ([a_re[k] * a_re[k] - a_im[k] * a_im[k] for k in ks], [2.0 * a_re[k] * a_im[k] for k in ks])
        shift *= 2
    keep = n_idx >= 1
    for k in ks:
        h_prev = jnp.concatenate([jnp.where(keep, pltpu.roll(s_re[k], 1, axis=1), 0.0),
                                  jnp.where(keep, pltpu.roll(s_im[k], 1, axis=1), 0.0)], axis=0)
        rb_rows = S5_CW // S5_MBLOCKS
        y = jnp.concatenate([jnp.dot(m_refs[rb][k], u[k][:(rb + 1) * rb_rows], preferred_element_type=F32)
                             for rb in range(S5_MBLOCKS)], axis=0)
        y = y + jnp.dot(ft_ref[k], h_prev.astype(BF16), preferred_element_type=F32)
        y_ref[:, k * h:(k + 1) * h, :] = y.reshape(CHUNK, h, r).astype(y_ref.dtype)


def s5_mix(ut, m, et, ft, ac, layer, nchunk, gsub):
    _, s5w, r = ut.shape
    g, h, p, cw = S5_GROUPS, S5_GROUP_CH, S5_STATE, S5_CW
    base = layer * g // gsub
    return pl.pallas_call(
        functools.partial(_s5_mix_kernel, nchunk=nchunk, gsub=gsub),
        grid=(g // gsub,),
        in_specs=[pl.BlockSpec((CHUNK, gsub * h, r), lambda i: (0, i, 0)),
                  *[pl.BlockSpec((gsub, cw // S5_MBLOCKS, (rb + 1) * cw // S5_MBLOCKS),
                                 lambda i, rb=rb: (base + i, rb, 0)) for rb in range(S5_MBLOCKS)],
                  pl.BlockSpec((gsub, 2 * p, cw), lambda i: (base + i, 0, 0)),
                  pl.BlockSpec((gsub, cw, 2 * p), lambda i: (base + i, 0, 0)),
                  pl.BlockSpec((gsub, 2 * p, 1), lambda i: (base + i, 0, 0))],
        out_specs=pl.BlockSpec((CHUNK, gsub * h, r), lambda i: (0, i, 0)),
        out_shape=jax.ShapeDtypeStruct((CHUNK, s5w, r), BF16),
        compiler_params=_params("parallel"),
        name="s5_mix",
    )(ut, *([m] * S5_MBLOCKS), et, ft, ac)


def _gla_kernel(q_ref, k_ref, v_ref, og_ref, lr_ref, wlr_ref, blr_ref, ng_ref, o_ref, st_ref, *, nc, nsub):
    @pl.when(pl.program_id(1) == 0)
    def _():
        st_ref[...] = jnp.zeros_like(st_ref)

    tc = nc * CHUNK

    def tile(s, carry):
        rows = pl.ds(pl.multiple_of(s * tc, tc), tc)
        _gla_tile(q_ref.at[rows], k_ref.at[rows], v_ref.at[rows], og_ref.at[rows], lr_ref.at[rows],
                  wlr_ref, blr_ref, ng_ref, o_ref.at[rows], st_ref, nc=nc)
        return carry

    lax.fori_loop(0, nsub, tile, 0, unroll=2)


def _gla_tile(q_ref, k_ref, v_ref, og_ref, lr_ref, wlr_ref, blr_ref, ng_ref, o_ref, st_ref, *, nc):
    hd, dk, dv = GLA_HEADS, GLA_DK, GLA_DV
    ri = lax.broadcasted_iota(jnp.int32, (CHUNK, CHUNK), 0)
    ci = lax.broadcasted_iota(jnp.int32, (CHUNK, CHUNK), 1)
    incl = ri >= ci
    ltri = incl.astype(BF16)
    lane_k = lax.broadcasted_iota(jnp.int32, (1, hd * dk), 1)
    srow = lax.broadcasted_iota(jnp.int32, (hd * dv, hd * dk), 0)
    scol = lax.broadcasted_iota(jnp.int32, (hd * dv, hd * dk), 1)
    same_head = (srow // dv) == (scol // dk)
    wlr, blr, ng = wlr_ref[...], blr_ref[...], ng_ref[...]

    cs = range(nc)
    cr = lambda c: slice(c * CHUNK, (c + 1) * CHUNK)
    g = [_log_sigmoid(_mm(lr_ref[cr(c), :], wlr) + blr) * (1.0 / GLA_GATE_NORM) for c in cs]
    bc = [_select_rows(ltri, g[c]) for c in cs]
    bl = [bc[c][CHUNK - 1:CHUNK, :] for c in cs]
    q_e = [q_ref[cr(c), :].astype(F32) * (dk ** -0.5) * jnp.exp(bc[c]) for c in cs]
    k_e = [k_ref[cr(c), :].astype(F32) * jnp.exp(-bc[c]) for c in cs]
    k_d = [k_ref[cr(c), :].astype(F32) * jnp.exp(bl[c] - bc[c]) for c in cs]
    kv = [jnp.where(same_head, _mm_tn(v_ref[cr(c), :], k_d[c]), 0.0) for c in cs]
    sts = []
    st = st_ref[...]
    for c in cs:
        sts.append(st)
        st = jnp.exp(bl[c]) * st + kv[c]
    st_ref[...] = st
    o_inter = [_mm_nt(q_e[c], sts[c]) for c in cs]
    q_heads = [jnp.concatenate([jnp.where((lane_k // dk) == h, q_e[c], 0.0) for h in range(hd)], axis=0)
               for c in cs]
    sc_all = [_mm_nt(q_heads[c], k_e[c]) for c in cs]
    sc = [[jnp.where(incl, sc_all[c][h * CHUNK:(h + 1) * CHUNK], 0.0) for h in range(hd)] for c in cs]
    for c in cs:
        for h in range(hd):
            cols = slice(h * dv, (h + 1) * dv)
            oh = _mm(sc[c][h], v_ref[cr(c), cols]) + o_inter[c][:, cols]
            ms = jnp.mean(oh * oh, axis=-1, keepdims=True)
            o_ref[cr(c), cols] = (oh * lax.rsqrt(ms + EPS) * ng * _silu(og_ref[cr(c), cols].astype(F32))).astype(o_ref.dtype)


def gla_mix(zm, wlr_pad, blr, ng, bsz, seq, tc, nsub):
    t = zm.shape[0]
    blk_rows = tc * nsub
    per_b = seq // blk_rows
    hd, dk, dv = GLA_HEADS, GLA_DK, GLA_DV

    def zspec(width, off):
        blk = off // width
        return pl.BlockSpec((blk_rows, width), lambda b, i: (b * per_b + i, blk))

    def full(a):
        return pl.BlockSpec(a.shape, lambda b, i: (0,) * a.ndim)

    return pl.pallas_call(
        functools.partial(_gla_kernel, nc=tc // CHUNK, nsub=nsub),
        grid=(bsz, per_b),
        in_specs=[zspec(hd * dk, ZM_GQ), zspec(hd * dk, ZM_GK), zspec(hd * dv, ZM_GV),
                  zspec(hd * dv, ZM_GOG), zspec(LANE, ZM_GLR), full(wlr_pad), full(blr), full(ng)],
        out_specs=pl.BlockSpec((blk_rows, hd * dv), lambda b, i: (b * per_b + i, 0)),
        out_shape=jax.ShapeDtypeStruct((t, hd * dv), F32),
        scratch_shapes=[pltpu.VMEM((hd * dv, hd * dk), F32)],
        compiler_params=_params("parallel", "arbitrary"),
        name="gla_mix",
    )(zm, zm, zm, zm, zm, wlr_pad, blr, ng)


GDN_PAD = 8


def _gdn_kernel(qkv_ref, og_ref, ba_ref, cw_ref, shift_ref, alog_ref, dtb_ref, ng_ref, o_ref,
                head_ref, s_ref, *, nc, nsub):
    @pl.when(pl.program_id(1) == 0)
    def _():
        s_ref[...] = jnp.zeros_like(s_ref)
        head_ref[0:GDN_PAD, :] = jnp.zeros((GDN_PAD, head_ref.shape[1]), F32)

    tc = nc * CHUNK

    def tile(s, carry):
        rows = pl.ds(pl.multiple_of(s * tc, tc), tc)
        _gdn_tile(qkv_ref.at[rows], og_ref.at[rows], ba_ref.at[rows], cw_ref, shift_ref, alog_ref, dtb_ref,
                  ng_ref, o_ref.at[rows], head_ref, s_ref, nc=nc)
        return carry

    lax.fori_loop(0, nsub, tile, 0, unroll=4)


def _gdn_tile(qkv_ref, og_ref, ba_ref, cw_ref, shift_ref, alog_ref, dtb_ref, ng_ref, o_ref,
              head_ref, s_ref, *, nc):
    hd, dk, dv, kc = GDN_HEADS, GDN_DK, GDN_DV, GDN_CONV
    tc = nc * CHUNK
    pad = GDN_PAD

    head_ref[pad:2 * pad, :] = qkv_ref[0:2 * pad, :].astype(F32)[0:pad]
    lagged = jnp.dot(shift_ref[...], qkv_ref[...], preferred_element_type=F32)

    ri = lax.broadcasted_iota(jnp.int32, (CHUNK, CHUNK), 0)
    ci = lax.broadcasted_iota(jnp.int32, (CHUNK, CHUNK), 1)
    incl = ri >= ci
    strict = ri > ci
    ltri = incl.astype(BF16)
    eye = (ri == ci).astype(F32)
    cw = [cw_ref[i] for i in range(kc)]
    ng = ng_ref[...]

    ba = ba_ref[...].astype(F32)
    beta_all = _sigmoid(ba)
    g_all = -jnp.exp(alog_ref[...]) * _softplus(ba + dtb_ref[...])

    cr = lambda c: slice(c * CHUNK, (c + 1) * CHUNK)
    gam_c = [_select_rows(ltri, g_all[cr(c), :]) for c in range(nc)]
    gam_tc = [g.T for g in gam_c]

    def conv_piece(col):
        lanes = slice(col * dk, (col + 1) * dk)
        blocks = lambda a: a.reshape(tc // 8, 8, dk)
        acc = cw[kc - 1][:, lanes] * blocks(qkv_ref[:, lanes].astype(F32))
        head = cw[kc - 1][:, lanes] * head_ref[pad:2 * pad, lanes]
        for i in range(kc - 1):
            lag = kc - 1 - i
            acc = acc + cw[i][:, lanes] * blocks(lagged[(lag - 1) * tc:lag * tc, lanes])
            head = head + cw[i][:, lanes] * head_ref[pad - lag:2 * pad - lag, lanes]
        acc = jnp.concatenate([head, acc.reshape(tc, dk)[pad:]], axis=0)
        y = _silu(acc)
        if col < 2 * hd:
            y = y * lax.rsqrt(jnp.sum(y * y, axis=-1, keepdims=True) + EPS)
        if col < hd:
            y = y * (dk ** -0.5)
        return y

    cols = [conv_piece(col) for col in range(3 * hd)]

    pairs = [(c, h) for c in range(nc) for h in range(hd)]
    n = len(pairs)
    q = [cols[h][cr(c)] for c, h in pairs]
    k = [cols[hd + h][cr(c)] for c, h in pairs]
    v = [cols[2 * hd + h][cr(c)] for c, h in pairs]
    beta = [beta_all[cr(c), h:h + 1] for c, h in pairs]
    gam = [gam_c[c][:, hd + h:hd + h + 1] for c, h in pairs]
    gam_row = [gam_tc[c][hd + h:hd + h + 1, :] for c, h in pairs]
    dmask = [jnp.where(incl, jnp.exp(gam[i] - gam_row[i]), 0.0) for i in range(n)]
    k_beta = [k[i] * beta[i] for i in range(n)]
    egam = [jnp.exp(gam[i]) for i in range(n)]
    kk = [_mm_nt(jnp.concatenate([k_beta[i], q[i]], axis=0), k[i]) for i in range(n)]
    low = [jnp.where(strict, kk[i][:CHUNK] * dmask[i], 0.0) for i in range(n)]
    attn = [kk[i][CHUNK:] * dmask[i] for i in range(n)]
    rhs = [jnp.concatenate([v[i] * beta[i], k_beta[i] * egam[i]], axis=1).astype(BF16) for i in range(n)]
    lowb = [low[i].astype(BF16) for i in range(n)]
    pw = [_mm(lowb[i], lowb[i]).astype(BF16) for i in range(n)]
    tinv = [eye - low[i] for i in range(n)]
    for s in range(5):
        if s < 4:
            stk = [_mm(jnp.concatenate([tinv[i].astype(BF16), pw[i]], axis=0), pw[i]) for i in range(n)]
            tinv = [tinv[i] + stk[i][:CHUNK] for i in range(n)]
            pw = [stk[i][CHUNK:].astype(BF16) for i in range(n)]
        else:
            tinv = [tinv[i] + _mm(tinv[i], pw[i]) for i in range(n)]
    sol = [_mm(tinv[i], rhs[i]).astype(BF16) for i in range(n)]
    q_dec = [q[i] * egam[i] for i in range(n)]
    gam_last = [gam[i][CHUNK - 1:CHUNK, :] for i in range(n)]
    k_dec = [k[i] * jnp.exp(gam_last[i] - gam[i]) for i in range(n)]
    ks = [_mm_tn(k_dec[i], sol[i]) for i in range(n)]
    aw = [_mm(attn[i], sol[i]) for i in range(n)]
    lhs = [jnp.concatenate([ks[i][:, dv:], q_dec[i] - aw[i][:, dv:]], axis=0).astype(BF16)
           for i in range(n)]
    s_add = [ks[i][:, :dv] for i in range(n)]
    o_add = [aw[i][:, :dv] for i in range(n)]
    decay = [jnp.exp(gam_last[i]) for i in range(n)]

    st = [s_ref[h] for h in range(hd)]
    for c in range(nc):
        ids = [c * hd + h for h in range(hd)]
        prod = [_mm(lhs[i], st[h]) for h, i in enumerate(ids)]
        o = [prod[h][dk:] + o_add[i] for h, i in enumerate(ids)]
        st = [decay[i] * st[h] - prod[h][:dk] + s_add[i] for h, i in enumerate(ids)]
        for h in range(hd):
            ms = jnp.mean(o[h] * o[h], axis=-1, keepdims=True)
            cl = slice(h * dv, (h + 1) * dv)
            o_ref[cr(c), cl] = (o[h] * lax.rsqrt(ms + EPS) * ng * _silu(og_ref[cr(c), cl].astype(F32))).astype(o_ref.dtype)
    for h in range(hd):
        s_ref[h] = st[h]
    head_ref[0:pad, :] = qkv_ref[tc - 2 * pad:tc, :].astype(F32)[pad:]


def gdn_mix(zm, conv_w_pad, shift_m, alog_row, dtb_row, ng, bsz, seq, tc, nsub):
    t = zm.shape[0]
    blk_rows = tc * nsub
    per_b = seq // blk_rows
    hd, dk, dv = GDN_HEADS, GDN_DK, GDN_DV
    qkv_w = 3 * hd * dk

    def zspec(width, off):
        blk = off // width
        return pl.BlockSpec((blk_rows, width), lambda b, i: (b * per_b + i, blk))

    def full(a):
        return pl.BlockSpec(a.shape, lambda b, i: (0,) * a.ndim)

    return pl.pallas_call(
        functools.partial(_gdn_kernel, nc=tc // CHUNK, nsub=nsub),
        grid=(bsz, per_b),
        in_specs=[zspec(qkv_w, ZM_DQKV), zspec(hd * dv, ZM_DOG), zspec(LANE, ZM_DBA),
                  full(conv_w_pad), full(shift_m), full(alog_row), full(dtb_row), full(ng)],
        out_specs=pl.BlockSpec((blk_rows, hd * dv), lambda b, i: (b * per_b + i, 0)),
        out_shape=jax.ShapeDtypeStruct((t, hd * dv), F32),
        scratch_shapes=[pltpu.VMEM((2 * GDN_PAD, qkv_w), F32), pltpu.VMEM((hd, dk, dv), F32)],
        compiler_params=_params("parallel", "arbitrary"),
        name="gdn_mix",
    )(zm, zm, zm, conv_w_pad, shift_m, alog_row, dtb_row, ng)


def _merge_kernel(h3_ref, ygla_ref, ygdn_ref, yst_ref, wglu_ref, wzg_ref, wbg_ref, wbs_ref, wbd_ref, o_ref,
                  s5t_ref, *, jt, bw, tn):
    n_rows = jt * LANE
    d = h3_ref.shape[-1]
    h = h3_ref[...].reshape(n_rows, d).astype(BF16)
    a_gla = ygla_ref[...].reshape(n_rows, bw).astype(BF16)
    a_gdn = ygdn_ref[...].reshape(n_rows, bw).astype(BF16)
    ys = jnp.concatenate([_gelu_tanh(yst_ref[jj].astype(F32)) for jj in range(jt)], axis=1)
    glu = _mm_tn(ys, wglu_ref[...])
    s5t_ref[...] = glu[:, :bw] * _sigmoid(glu[:, bw:])
    a_s5 = pltpu.einshape("(jr)d->rjd", s5t_ref[...], j=jt).reshape(n_rows, bw).astype(BF16)

    def branch(b, cols, a, wb_ref):
        gate = _sigmoid(jnp.dot(h, wzg_ref[:, b * d + cols.start:b * d + cols.stop], preferred_element_type=F32))
        return gate * jnp.dot(a, wb_ref[:, cols], preferred_element_type=F32)

    for c0 in range(0, d, tn):
        cols = slice(c0, c0 + tn)
        m = branch(0, cols, a_gla, wbg_ref) + branch(1, cols, a_s5, wbs_ref) + branch(2, cols, a_gdn, wbd_ref)
        o_ref[:, :, cols] = m.reshape(LANE, jt, tn)


def merge_gate(h3, ygla3, ygdn3, yst, wglu, wzg, wbg, wbs, wbd, layer, jt, tn):
    r, _, d = h3.shape
    bw = ygla3.shape[2]

    def view(width):
        return pl.BlockSpec((LANE, jt, width), lambda i, j: (i, j, 0))

    def layer_w(a):
        return pl.BlockSpec((None,) + a.shape[1:], lambda i, j: (layer, 0, 0))

    return pl.pallas_call(
        functools.partial(_merge_kernel, jt=jt, bw=bw, tn=tn),
        grid=(r // LANE, CHUNK // jt),
        in_specs=[view(d), view(bw), view(bw),
                  pl.BlockSpec((jt, yst.shape[1], LANE), lambda i, j: (j, 0, i)),
                  layer_w(wglu), layer_w(wzg), layer_w(wbg), layer_w(wbs), layer_w(wbd)],
        out_specs=view(d),
        out_shape=jax.ShapeDtypeStruct((r, CHUNK, d), F32),
        scratch_shapes=[pltpu.VMEM((jt * LANE, bw), F32)],
        compiler_params=_params("parallel", "parallel"),
        name="merge_gate",
    )(h3, ygla3, ygdn3, yst, wglu, wzg, wbg, wbs, wbd)


def _ffn_kernel(x_ref, m_ref, wout_ref, gt1_ref, g_ref, sc_ref, sh_ref, gt2_ref, win_ref, wo_ref,
                fg_ref, o_ref, *, final_norm, tf):
    dff = wo_ref.shape[0]
    x1 = x_ref[...] + gt1_ref[...] * _mm(m_ref[...], wout_ref[...])
    h = _norm_mod(x1, g_ref[...], sc_ref[...], sh_ref[...]).astype(BF16)
    acc = None
    for f0 in range(0, dff, tf):
        a = jnp.dot(h, win_ref[:, f0:f0 + tf], preferred_element_type=F32)
        b = jnp.dot(h, win_ref[:, dff + f0:dff + f0 + tf], preferred_element_type=F32)
        part = _mm(_silu(a) * b, wo_ref[f0:f0 + tf, :])
        acc = part if acc is None else acc + part
    y = x1 + gt2_ref[...] * acc
    if final_norm:
        ms = jnp.mean(y * y, axis=-1, keepdims=True)
        y = y * lax.rsqrt(ms + EPS) * fg_ref[...]
    o_ref[...] = y


def ffn_block(x2, merged, w_o, gt1, g, sc, sh, gt2, w_in, w_out, final_g, layer, seq, tm, tf, final_norm):
    t, d = x2.shape
    dff = w_out.shape[1]
    per_b = seq // tm

    def bspec():
        return pl.BlockSpec((None, 1, d), lambda i: (i // per_b, 0, 0))

    def const(a):
        return pl.BlockSpec(a.shape, lambda i: (0,) * a.ndim)

    def layer_w(a):
        return pl.BlockSpec((None,) + a.shape[1:], lambda i: (layer, 0, 0))

    row = pl.BlockSpec((tm, d), lambda i: (i, 0))
    return pl.pallas_call(
        functools.partial(_ffn_kernel, final_norm=final_norm, tf=tf),
        grid=(t // tm,),
        in_specs=[row, row, layer_w(w_o), bspec(), const(g), bspec(), bspec(), bspec(),
                  layer_w(w_in), layer_w(w_out), const(final_g)],
        out_specs=row,
        out_shape=jax.ShapeDtypeStruct((t, d), F32),
        compiler_params=_params("parallel"),
        name="ffn_block",
    )(x2, merged, w_o, gt1, g, sc, sh, gt2, w_in, w_out, final_g)


def _wprep_kernel(w_ref, zm_ref, zg_ref, s5t_ref, *, segs, pads, zg_off, s5_off):
    for dst, width in pads:
        zm_ref[0, :, dst:dst + width] = jnp.zeros((zm_ref.shape[1], width), BF16)
    for src, width, dst in segs:
        zm_ref[0, :, dst:dst + width] = w_ref[0, :, src:src + width].astype(BF16)
    zg_ref[0] = w_ref[0, :, zg_off:zg_off + zg_ref.shape[2]].astype(BF16)
    s5t_ref[0] = w_ref[0, :, s5_off:s5_off + s5t_ref.shape[1]].astype(F32).T.astype(BF16)


def prep_in_weights(w_in, segs, pads, zg_off, zg_w, s5_off, s5_w, tr):
    depth, d, d_in = w_in.shape
    return pl.pallas_call(
        functools.partial(_wprep_kernel, segs=segs, pads=pads, zg_off=zg_off, s5_off=s5_off),
        grid=(depth, d // tr),
        in_specs=[pl.BlockSpec((1, tr, d_in), lambda i, j: (i, j, 0))],
        out_specs=[pl.BlockSpec((1, tr, ZM_WIDTH), lambda i, j: (i, j, 0)),
                   pl.BlockSpec((1, tr, zg_w), lambda i, j: (i, j, 0)),
                   pl.BlockSpec((1, s5_w, tr), lambda i, j: (i, 0, j))],
        out_shape=[jax.ShapeDtypeStruct((depth, d, ZM_WIDTH), BF16),
                   jax.ShapeDtypeStruct((depth, d, zg_w), BF16),
                   jax.ShapeDtypeStruct((depth, s5_w, d), BF16)],
        compiler_params=_params("parallel", "parallel"),
        name="prep_in_weights",
    )(w_in)


def _forward(x, c, w_ada, b_ada, norm1_g, w_in, gla_w_lr, gla_b_lr, gla_norm_g,
             s5_lambda_re, s5_lambda_im, s5_log_dt, s5_b_re, s5_b_im, s5_c_re, s5_c_im,
             s5_d, s5_w_glu, gdn_conv_w, gdn_a_log, gdn_dt_bias, gdn_norm_g,
             w_branch_gla, w_branch_s5, w_branch_gdn, w_out, norm2_g, w_ffn_in, w_ffn_out,
             final_g):
    bsz, seq, d = x.shape
    depth = w_ada.shape[0]
    t = bsz * seq
    nchunk = seq // CHUNK
    r = bsz * nchunk
    gqk, gw = GLA_HEADS * GLA_DK, GLA_HEADS * GLA_DV
    s5w = S5_GROUPS * S5_GROUP_CH
    dqkv, dw = 3 * GDN_HEADS * GDN_DK, GDN_HEADS * GDN_DV

    mod = ada_modulation(c, w_ada, b_ada).reshape(depth, bsz, 6, 1, d)
    s5m, s5et, s5ft, s5ac = s5_params(s5_lambda_re, s5_lambda_im, s5_log_dt, s5_b_re, s5_b_im,
                                      s5_c_re, s5_c_im, s5_d)

    o_gq, o_gk, o_gv = 0, gqk, 2 * gqk
    o_glr = o_gv + gw
    o_gog = o_glr + GLA_LOWRANK
    o_s5 = o_gog + gw
    o_dqkv = o_s5 + s5w
    o_dbeta = o_dqkv + dqkv
    o_da = o_dbeta + GDN_HEADS
    o_dog = o_da + GDN_HEADS
    o_zg = o_dog + dw

    segs = ((o_gq, gqk, ZM_GQ), (o_gk, gqk, ZM_GK), (o_gv, gw, ZM_GV), (o_gog, gw, ZM_GOG),
            (o_dqkv, dqkv, ZM_DQKV), (o_dog, dw, ZM_DOG), (o_glr, GLA_LOWRANK, ZM_GLR),
            (o_dbeta, 2 * GDN_HEADS, ZM_DBA))
    pads = ((ZM_GLR, LANE), (ZM_DBA, LANE))
    w_in_b = jnp.pad(w_in.astype(BF16), ((0, 0), (0, 0), (0, -w_in.shape[2] % LANE)))
    w_zm_all, w_zg_all, w_s5t_all = prep_in_weights(w_in_b, segs, pads, o_zg, N_BRANCH * d, o_s5, s5w, tr=WPREP_ROWS)
    s5_w_glu, w_branch_gla, w_branch_s5, w_branch_gdn, w_out, w_ffn_in, w_ffn_out = [
        a.astype(BF16) for a in (s5_w_glu, w_branch_gla, w_branch_s5, w_branch_gdn, w_out, w_ffn_in, w_ffn_out)]

    lag_row = jnp.arange((GDN_CONV - 1) * GDN_TC)
    gdn_shift = (jnp.arange(GDN_TC)[None, :]
                 == (lag_row % GDN_TC - lag_row // GDN_TC - 1)[:, None]).astype(BF16)

    x2 = x.reshape(t, d)
    for i in range(depth):

        sh1, sc1, gt1, sh2, sc2, gt2 = [mod[i, :, k] for k in range(6)]
        g1 = norm1_g[i].reshape(1, d)

        zm, hf = norm_mod_matmul(x2, g1, sc1, sh1, w_zm_all, i, seq, tm=INPROJ_TM, tn=ZM_WIDTH)
        h3 = hf.reshape(r, CHUNK, d)
        ut = s5_inproj(h3, w_s5t_all, i, jt=TOK_TILE, rt=r)

        wlr_pad = jnp.pad(gla_w_lr[i], ((0, LANE - GLA_LOWRANK), (0, 0)))
        y_gla = gla_mix(zm, wlr_pad, gla_b_lr[i].reshape(1, gqk), gla_norm_g[i].reshape(1, GLA_DV),
                        bsz, seq, tc=GLA_TC, nsub=GLA_NSUB)
        yst = s5_mix(ut, s5m, s5et, s5ft, s5ac, i, nchunk, gsub=S5_GSUB)
        conv_pad = jnp.broadcast_to(gdn_conv_w[i][:, None, :], (GDN_CONV, 8, dqkv))
        alog_row = jnp.pad(gdn_a_log[i], (GDN_HEADS, LANE - 2 * GDN_HEADS)).reshape(1, LANE)
        dtb_row = jnp.pad(gdn_dt_bias[i], (GDN_HEADS, LANE - 2 * GDN_HEADS)).reshape(1, LANE)
        y_gdn = gdn_mix(zm, conv_pad, gdn_shift, alog_row, dtb_row, gdn_norm_g[i].reshape(1, GDN_DV),
                        bsz, seq, tc=GDN_TC, nsub=GDN_NSUB)

        merged = merge_gate(h3, y_gla.reshape(r, CHUNK, gw), y_gdn.reshape(r, CHUNK, dw),
                            yst, s5_w_glu, w_zg_all, w_branch_gla, w_branch_s5, w_branch_gdn, i,
                            jt=TOK_TILE, tn=MERGE_TN)
        x2 = ffn_block(x2, merged.reshape(t, d), w_out, gt1,
                       norm2_g[i].reshape(1, d), sc2, sh2, gt2,
                       w_ffn_in, w_ffn_out,
                       final_g.reshape(1, d), i, seq, tm=FFN_TM, tf=FFN_TF, final_norm=(i == depth - 1))
    return x2.reshape(bsz, seq, d)


def kernel(x, c, w_ada, b_ada, norm1_g, w_in, gla_w_lr, gla_b_lr, gla_norm_g, s5_lambda_re, s5_lambda_im, s5_log_dt, s5_b_re, s5_b_im, s5_c_re, s5_c_im, s5_d, s5_w_glu, gdn_conv_w, gdn_a_log, gdn_dt_bias, gdn_norm_g, w_branch_gla, w_branch_s5, w_branch_gdn, w_out, norm2_g, w_ffn_in, w_ffn_out, final_g):
    return _forward(x, c, w_ada, b_ada, norm1_g, w_in, gla_w_lr, gla_b_lr, gla_norm_g,
                    s5_lambda_re, s5_lambda_im, s5_log_dt, s5_b_re, s5_b_im, s5_c_re, s5_c_im,
                    s5_d, s5_w_glu, gdn_conv_w, gdn_a_log, gdn_dt_bias, gdn_norm_g,
                    w_branch_gla, w_branch_s5, w_branch_gdn, w_out, norm2_g, w_ffn_in, w_ffn_out,
                    final_g)
```
